```python
import jax, jax.numpy as jnp
from jax import lax
import numpy as np

D_MODEL = 1024
BATCH = 2
SEQ = 8192
DEPTH = 2

CTX_LEN = 256
GRID_W = 64
HEAD_DIM = 64
EPS = 1e-6
FNET_GROUPS = 4
FNET_GROUP_DIM = 64
FNET_W = FNET_GROUPS * FNET_GROUP_DIM
RET_HEADS = 4
RET_W = RET_HEADS * HEAD_DIM
RET_CHUNK = 128
ATT_Q_HEADS = 8
ATT_KV_HEADS = 2
ATT_GROUP = ATT_Q_HEADS // ATT_KV_HEADS
ATT_QW = ATT_Q_HEADS * HEAD_DIM
ATT_KVW = ATT_KV_HEADS * HEAD_DIM
ATT_BLOCK = 128
ROPE_THETA = 10000.0
CONV_W = 256
CONV_K = 31
N_BRANCH = 4
IN_SIZES = (FNET_W, RET_W, RET_W, RET_W, RET_W, ATT_QW, ATT_KVW, ATT_KVW, 2 * CONV_W, N_BRANCH * D_MODEL)
IN_COLS = FNET_W + 4 * RET_W + ATT_QW + 2 * ATT_KVW + 2 * CONV_W + N_BRANCH * D_MODEL
FFN_DIM = 2816
N_EXPERTS = 8
TOP_K = 2
MOE_BLOCK = 128
N_DENSE = (DEPTH + 1) // 2
N_MOE = DEPTH // 2

kernel_name = "hybrid_fnet_retention_gqa_conformer_moe_dit"


def rms_norm(x, g):
    xf = x.astype(jnp.float32)
    y = xf * lax.rsqrt(jnp.mean(xf * xf, axis=-1, keepdims=True) + EPS)
    return (y * g.astype(jnp.float32)).astype(x.dtype)


def layer_norm(x, g, b):
    xf = x.astype(jnp.float32)
    mu = jnp.mean(xf, axis=-1, keepdims=True)
    var = jnp.mean(jnp.square(xf - mu), axis=-1, keepdims=True)
    y = (xf - mu) * lax.rsqrt(var + EPS) * g.astype(jnp.float32) + b.astype(jnp.float32)
    return y.astype(x.dtype)


def modulate(h, shift, scale):
    return h * (1.0 + scale) + shift


def rope_tables(rows):
    row = jnp.broadcast_to(jnp.arange(rows)[:, None], (rows, GRID_W)).reshape(-1).astype(jnp.float32)
    col = jnp.broadcast_to(jnp.arange(GRID_W)[None, :], (rows, GRID_W)).reshape(-1).astype(jnp.float32)
    n_axis = HEAD_DIM // 4
    inv = ROPE_THETA ** (-jnp.arange(n_axis, dtype=jnp.float32) / n_axis)
    ang = jnp.concatenate([row[:, None] * inv, col[:, None] * inv], axis=-1)
    return jnp.cos(ang), jnp.sin(ang)


def apply_rope(x, cos, sin):
    xf = x.astype(jnp.float32)
    x1, x2 = xf[..., :HEAD_DIM // 2], xf[..., HEAD_DIM // 2:]
    c, s = cos[None, :, None, :], sin[None, :, None, :]
    return jnp.concatenate([x1 * c - x2 * s, x1 * s + x2 * c], axis=-1).astype(x.dtype)


def fourier_mix(u):
    b, l, _ = u.shape
    ug = u.astype(jnp.float32).reshape(b, l, FNET_GROUPS, FNET_GROUP_DIM)
    y = jnp.fft.fft2(ug, axes=(1, 3), norm="ortho").real
    return y.reshape(b, l, FNET_W).astype(u.dtype)


def retention_scan(q, k, v, log_gamma, state0):
    b, l, h, d = q.shape
    n = l // RET_CHUNK
    pos = jnp.arange(RET_CHUNK, dtype=jnp.float32)
    diff = pos[:, None] - pos[None, :]
    decay = jnp.where(diff[None] >= 0.0,
                      jnp.exp(jnp.maximum(diff, 0.0)[None] * log_gamma[:, None, None]), 0.0)
    xi = jnp.exp((pos[:, None] + 1.0) * log_gamma[None, :])
    zeta = jnp.exp((RET_CHUNK - 1.0 - pos[:, None]) * log_gamma[None, :])
    g_chunk = jnp.exp(RET_CHUNK * log_gamma)

    def chunks(t):
        return t.reshape(b, n, RET_CHUNK, h, d).transpose(1, 0, 2, 3, 4)

    def step(state, qkv):
        qc, kc, vc = qkv
        att = jnp.einsum('bihd,bjhd->bhij', qc, kc) * decay
        o = (jnp.einsum('bhij,bjhe->bihe', att, vc)
             + jnp.einsum('bihd,bhde->bihe', qc, state) * xi[None, :, :, None])
        state = (state * g_chunk[None, :, None, None]
                 + jnp.einsum('bjhd,bjhe->bhde', kc * zeta[None, :, :, None], vc))
        return state, o

    state, o = lax.scan(step, state0, (chunks(q), chunks(k), chunks(v)))
    return o.transpose(1, 0, 2, 3, 4).reshape(b, l, h, d), state


def bidir_retention(q_c, k_c, v_c, q_l, k_l, v_l, lg_f, lg_b):
    b = q_l.shape[0]
    zero = jnp.zeros((b, RET_HEADS, HEAD_DIM, HEAD_DIM), jnp.float32)

    def flip(t):
        return t[:, ::-1]

    oc_f, sc_f = retention_scan(q_c, k_c, v_c, lg_f, zero)
    ol_f, _ = retention_scan(q_l, k_l, v_l, lg_f, sc_f)
    oc_b, sc_b = retention_scan(flip(q_c), flip(k_c), flip(v_c), lg_b, zero)
    ol_b, _ = retention_scan(flip(q_l), flip(k_l), flip(v_l), lg_b, sc_b)
    return oc_f + flip(oc_b), ol_f + flip(ol_b)


def retention_out(o, gate, g, w):
    b, l = o.shape[:2]
    y = rms_norm(o, g.reshape(RET_HEADS, HEAD_DIM)).reshape(b, l, RET_W).astype(gate.dtype)
    return (y * jax.nn.silu(gate)) @ w


def gqa_attend(q, k, v):
    b, lq = q.shape[:2]
    qg = q.reshape(b, lq, ATT_KV_HEADS, ATT_GROUP, HEAD_DIM)
    s = jnp.einsum('bqkgd,bskd->bkgqs', qg, k, preferred_element_type=jnp.float32) * (HEAD_DIM ** -0.5)
    p = jax.nn.softmax(s, axis=-1).astype(v.dtype)
    return jnp.einsum('bkgqs,bskd->bqkgd', p, v).reshape(b, lq, ATT_QW)


def latent_attention(q, k_all, v_all):
    b, s = q.shape[:2]
    nb = s // ATT_BLOCK
    qb = q.reshape(b, nb, ATT_BLOCK, ATT_Q_HEADS, HEAD_DIM).transpose(1, 0, 2, 3, 4)
    o = lax.map(lambda qblk: gqa_attend(qblk, k_all, v_all), qb)
    return o.transpose(1, 0, 2, 3).reshape(b, s, ATT_QW)


def conformer_conv(u, dw_w, dw_b, ln_g, ln_b, w_out):
    a, gt = jnp.split(u, 2, axis=-1)
    y = a * jax.nn.sigmoid(gt)
    y = lax.conv_general_dilated(y, dw_w[:, None, :], window_strides=(1,),
                                 padding=[(CONV_K // 2, CONV_K // 2)],
                                 dimension_numbers=('NWC', 'WIO', 'NWC'),
                                 feature_group_count=CONV_W) + dw_b
    return jax.nn.silu(layer_norm(y, ln_g, ln_b)) @ w_out


def merge_branches(branches, gate_pre, w_out):
    b, l, _ = gate_pre.shape
    g = jax.nn.sigmoid(gate_pre).reshape(b, l, N_BRANCH, D_MODEL)
    y0, y1, y2, y3 = branches
    merged = g[:, :, 0] * y0 + g[:, :, 1] * y1 + g[:, :, 2] * y2 + g[:, :, 3] * y3
    return merged @ w_out


def token_mixers(h_l, h_c, cos, sin, w_in, fnet_w, dec_f, dec_b, ret_g, ret_w,
                 qn_g, kn_g, att_w, dw_w, dw_b, cln_g, cln_b, conv_out, w_out, with_ctx):
    b, n_lat, _ = h_l.shape
    n_ctx = h_c.shape[1]
    splits = [int(s) for s in np.cumsum(IN_SIZES)[:-1]]
    fn_l, rq_l, rk_l, rv_l, rg_l, aq_l, ak_l, av_l, cv_l, gt_l = jnp.split(h_l @ w_in, splits, axis=-1)
    fn_c, rq_c, rk_c, rv_c, rg_c, aq_c, ak_c, av_c, cv_c, gt_c = jnp.split(h_c @ w_in, splits, axis=-1)

    def rheads(t, n):
        return t.reshape(b, n, RET_HEADS, HEAD_DIM).astype(jnp.float32)

    k_scale = HEAD_DIM ** -0.5
    lg_f, lg_b = jax.nn.log_sigmoid(dec_f.astype(jnp.float32)), jax.nn.log_sigmoid(dec_b.astype(jnp.float32))
    ro_c, ro_l = bidir_retention(
        rheads(rq_c, n_ctx), rheads(rk_c, n_ctx) * k_scale, rheads(rv_c, n_ctx),
        apply_rope(rheads(rq_l, n_lat), cos, sin), apply_rope(rheads(rk_l, n_lat), cos, sin) * k_scale,
        rheads(rv_l, n_lat), lg_f, lg_b)
    y_ret_l = retention_out(ro_l, rg_l, ret_g, ret_w)

    def aheads(t, n, hh):
        return t.reshape(b, n, hh, HEAD_DIM)

    q_l = apply_rope(rms_norm(aheads(aq_l, n_lat, ATT_Q_HEADS), qn_g), cos, sin)
    k_l = apply_rope(rms_norm(aheads(ak_l, n_lat, ATT_KV_HEADS), kn_g), cos, sin)
    k_c = rms_norm(aheads(ak_c, n_ctx, ATT_KV_HEADS), kn_g)
    v_l, v_c = aheads(av_l, n_lat, ATT_KV_HEADS), aheads(av_c, n_ctx, ATT_KV_HEADS)
    k_all = jnp.concatenate([k_c, k_l], axis=1)
    v_all = jnp.concatenate([v_c, v_l], axis=1)
    y_att_l = latent_attention(q_l, k_all, v_all) @ att_w

    y_fn_l = fourier_mix(fn_l) @ fnet_w
    y_cv_l = conformer_conv(cv_l, dw_w, dw_b, cln_g, cln_b, conv_out)
    out_l = merge_branches((y_fn_l, y_ret_l, y_att_l, y_cv_l), gt_l, w_out)

    if not with_ctx:
        return out_l, None

    q_c = rms_norm(aheads(aq_c, n_ctx, ATT_Q_HEADS), qn_g)
    y_att_c = gqa_attend(q_c, k_c, v_c) @ att_w
    y_ret_c = retention_out(ro_c, rg_c, ret_g, ret_w)
    y_fn_c = fourier_mix(fn_c) @ fnet_w
    y_cv_c = conformer_conv(cv_c, dw_w, dw_b, cln_g, cln_b, conv_out)
    out_c = merge_branches((y_fn_c, y_ret_c, y_att_c, y_cv_c), gt_c, w_out)
    return out_l, out_c


def swiglu(x, wg, wu, wd):
    return (jax.nn.silu(x @ wg) * (x @ wu)) @ wd


def moe_swiglu(x, router_w, wg, wu, wd):
    b, l, d = x.shape
    xt = x.reshape(-1, d)
    n_tok = xt.shape[0]
    logits = jnp.dot(xt, router_w, preferred_element_type=jnp.float32)
    top_v, top_e = lax.top_k(logits, TOP_K)
    top_w = jax.nn.softmax(top_v, axis=-1)
    e = top_e.reshape(-1)
    tok = jnp.repeat(jnp.arange(n_tok, dtype=jnp.int32), TOP_K)
    w = top_w.reshape(-1)
    n_assign = e.shape[0]
    order = jnp.argsort(e)
    e_s, tok_s, w_s = e[order], tok[order], w[order]
    counts = jnp.bincount(e, length=N_EXPERTS)
    start = jnp.cumsum(counts) - counts
    padded = (counts + MOE_BLOCK - 1) // MOE_BLOCK * MOE_BLOCK
    pad_end = jnp.cumsum(padded)
    pad_start = pad_end - padded
    dest = pad_start[e_s] + jnp.arange(n_assign) - start[e_s]
    n_blocks = -(-n_assign // MOE_BLOCK) + N_EXPERTS
    n_rows = n_blocks * MOE_BLOCK
    buf_tok = jnp.full((n_rows,), n_tok, jnp.int32).at[dest].set(tok_s)
    buf_w = jnp.zeros((n_rows,), jnp.float32).at[dest].set(w_s)
    blk_e = jnp.minimum(jnp.searchsorted(pad_end, jnp.arange(n_blocks) * MOE_BLOCK, side='right'),
                        N_EXPERTS - 1)
    x_pad = jnp.concatenate([xt, jnp.zeros((1, d), xt.dtype)], axis=0)
    xb = x_pad[buf_tok].reshape(n_blocks, MOE_BLOCK, d)

    def expert_block(args):
        xblk, ei = args
        return swiglu(xblk, wg[ei], wu[ei], wd[ei])

    yb = lax.map(expert_block, (xb, blk_e)).reshape(n_rows, d)
    y = jnp.zeros((n_tok + 1, d), jnp.float32).at[buf_tok].add(yb.astype(jnp.float32) * buf_w[:, None])
    return y[:n_tok].astype(x.dtype).reshape(b, l, d)


def setup_inputs(seed: int = 0) -> dict:
    key = jax.random.key(seed)
    keys = iter(jax.random.split(key, 40))
    f32 = jnp.float32

    def nrm(shape, scale):
        return jax.random.normal(next(keys), shape, f32) * scale

    def gain(shape):
        return 1.0 + nrm(shape, 0.02)

    gamma_logit = jnp.log(2.0 ** (5.0 + jnp.arange(RET_HEADS, dtype=f32)) - 1.0)
    return {
        "x": nrm((BATCH, SEQ, D_MODEL), 1.0),
        "c": nrm((BATCH, D_MODEL), 1.0),
        "ctx": nrm((BATCH, CTX_LEN, D_MODEL), 1.0),
        "c_ctx": nrm((D_MODEL,), 1.0),
        "ada_w": nrm((DEPTH, D_MODEL, 6 * D_MODEL), 0.5 * D_MODEL ** -0.5),
        "ada_b": nrm((DEPTH, 6 * D_MODEL), 0.02),
        "norm1_g": gain((DEPTH, D_MODEL)),
        "norm2_g": gain((DEPTH, D_MODEL)),
        "w_in": nrm((DEPTH, D_MODEL, IN_COLS), D_MODEL ** -0.5),
        "fnet_w": nrm((DEPTH, FNET_W, D_MODEL), FNET_W ** -0.5),
        "ret_decay_fwd": gamma_logit[None, :] + nrm((DEPTH, RET_HEADS), 0.05),
        "ret_decay_bwd": gamma_logit[None, :] + nrm((DEPTH, RET_HEADS), 0.05),
        "ret_norm_g": gain((DEPTH, RET_W)),
        "ret_w": nrm((DEPTH, RET_W, D_MODEL), RET_W ** -0.5),
        "attn_qn_g": gain((DEPTH, HEAD_DIM)),
        "attn_kn_g": gain((DEPTH, HEAD_DIM)),
        "attn_w": nrm((DEPTH, ATT_QW, D_MODEL), ATT_QW ** -0.5),
        "conv_dw_w": nrm((DEPTH, CONV_K, CONV_W), CONV_K ** -0.5),
        "conv_dw_b": nrm((DEPTH, CONV_W), 0.02),
        "conv_ln_g": gain((DEPTH, CONV_W)),
        "conv_ln_b": nrm((DEPTH, CONV_W), 0.02),
        "conv_w_out": nrm((DEPTH, CONV_W, D_MODEL), CONV_W ** -0.5),
        "w_out": nrm((DEPTH, D_MODEL, D_MODEL), D_MODEL ** -0.5),
        "ffn_w_gate": nrm((N_DENSE, D_MODEL, FFN_DIM), D_MODEL ** -0.5),
        "ffn_w_up": nrm((N_DENSE, D_MODEL, FFN_DIM), D_MODEL ** -0.5),
        "ffn_w_down": nrm((N_DENSE, FFN_DIM, D_MODEL), FFN_DIM ** -0.5),
        "router_w": nrm((N_MOE, D_MODEL, N_EXPERTS), D_MODEL ** -0.5),
        "moe_w_gate": nrm((N_MOE, N_EXPERTS, D_MODEL, FFN_DIM), D_MODEL ** -0.5),
        "moe_w_up": nrm((N_MOE, N_EXPERTS, D_MODEL, FFN_DIM), D_MODEL ** -0.5),
        "moe_w_down": nrm((N_MOE, N_EXPERTS, FFN_DIM, D_MODEL), FFN_DIM ** -0.5),
    }


def reference(x, c, ctx, c_ctx, ada_w, ada_b, norm1_g, norm2_g, w_in, fnet_w, ret_decay_fwd, ret_decay_bwd,
              ret_norm_g, ret_w, attn_qn_g, attn_kn_g, attn_w, conv_dw_w, conv_dw_b, conv_ln_g, conv_ln_b,
              conv_w_out, w_out, ffn_w_gate, ffn_w_up, ffn_w_down, router_w, moe_w_gate, moe_w_up, moe_w_down):
    n_lat = x.shape[1]
    rows = n_lat // GRID_W
    cos, sin = rope_tables(rows)
    n_ctx = ctx.shape[1]
    silu_c = jax.nn.silu(c)
    silu_cc = jax.nn.silu(c_ctx)
    h_lat, h_ctx = x, ctx
    for i in range(DEPTH):
        with_ctx = i < DEPTH - 1
        mod_l = jnp.split((silu_c @ ada_w[i] + ada_b[i])[:, None, :], 6, axis=-1)
        mod_c = jnp.split((silu_cc @ ada_w[i] + ada_b[i])[None, None, :], 6, axis=-1)
        a_l = modulate(rms_norm(h_lat, norm1_g[i]), mod_l[0], mod_l[1])
        a_c = modulate(rms_norm(h_ctx, norm1_g[i]), mod_c[0], mod_c[1])
        m_l, m_c = token_mixers(a_l, a_c, cos, sin, w_in[i], fnet_w[i], ret_decay_fwd[i], ret_decay_bwd[i],
                                ret_norm_g[i], ret_w[i], attn_qn_g[i], attn_kn_g[i], attn_w[i],
                                conv_dw_w[i], conv_dw_b[i], conv_ln_g[i], conv_ln_b[i], conv_w_out[i],
                                w_out[i], with_ctx)
        h_lat = h_lat + mod_l[2] * m_l
        j = i // 2
        if i % 2 == 0:
            ffn = lambda t, j=j: swiglu(t, ffn_w_gate[j], ffn_w_up[j], ffn_w_down[j])
        else:
            ffn = lambda t, j=j: moe_swiglu(t, router_w[j], moe_w_gate[j], moe_w_up[j], moe_w_down[j])
        f_l = modulate(rms_norm(h_lat, norm2_g[i]), mod_l[3], mod_l[4])
        if with_ctx:
            h_ctx = h_ctx + mod_c[2] * m_c
            f_c = modulate(rms_norm(h_ctx, norm2_g[i]), mod_c[3], mod_c[4])
            y = ffn(jnp.concatenate([f_c, f_l], axis=1))
            h_ctx = h_ctx + mod_c[5] * y[:, :n_ctx]
            h_lat = h_lat + mod_l[5] * y[:, n_ctx:]
        else:
            h_lat = h_lat + mod_l[5] * ffn(f_l)
    return h_lat
```

```python
import functools
import math

import numpy as np
import jax
import jax.numpy as jnp
from jax import lax
from jax.experimental import pallas as pl
from jax.experimental.pallas import tpu as pltpu

f32 = jnp.float32
bf16 = jnp.bfloat16

D = 1024
B = 2
S = 8192
C = 256
T = S + C
DEPTH = 2
GRID_W = 64
HD = 64
EPS = 1e-6
RET_HEADS = 4
RET_CHUNK = 128
N_CHUNK = T // RET_CHUNK
N_LAT_CHUNK = S // RET_CHUNK
ATT_Q_HEADS = 8
ATT_KV_HEADS = 2
ATT_GROUP = 4
CONV_K = 31
CONV_W = 256
HALO = 16
IN_COLS = 6656
FFN = 2816
FFN_CHUNK = 256
N_EXPERTS = 8
MOE_BLOCK = 128
ROPE_THETA = 10000.0

TM = 256
NT = T // TM
NL = S // TM
TK = 256

N_TOK = B * S
N_ASSIGN = N_TOK * 2
N_BLOCKS = N_ASSIGN // MOE_BLOCK + N_EXPERTS
N_ROWS = N_BLOCKS * MOE_BLOCK
CMB = 128

FN1 = 64
FN2 = 128
FN_STEP = 8
FN_COLS = 4096

VMEM_LIMIT = 56 * 1024 * 1024


def _cparams(*sem):
    return pltpu.CompilerParams(dimension_semantics=sem, vmem_limit_bytes=VMEM_LIMIT)


def _const_spec(shape):
    nd = len(shape)
    return pl.BlockSpec(shape, lambda *_: (0,) * nd, pipeline_mode=pl.Buffered(1))


def _mod_spec(k):
    return pl.BlockSpec((1, 1, D), lambda b, i: (jnp.where(i == NL, 2, b) * 6 + k, 0, 0))


def _dot(a, b):
    return jnp.dot(a, b, preferred_element_type=f32)


def _silu(x):
    return x * jax.nn.sigmoid(x)


def _dft_tables():
    n1 = np.arange(FN1)[:, None, None]
    k2 = np.arange(FN2)[None, :, None]
    n2 = np.arange(FN2)[None, None, :]
    ph = 2.0 * np.pi * (((n1 + FN1 * n2) * k2) % S) / S
    g = np.concatenate([np.cos(ph), -np.sin(ph)], axis=1)
    k1 = np.arange(FN1)[:, None]
    m1 = np.arange(FN1)[None, :]
    ph = 2.0 * np.pi * ((k1 * m1) % FN1) / FN1
    c64, s64 = np.cos(ph), np.sin(ph)
    m2 = np.block([[c64, s64], [-s64, c64]])
    kc = np.arange(C)[:, None]
    nc = np.arange(C)[None, :]
    ph = 2.0 * np.pi * ((kc * nc) % C) / C
    fc = np.concatenate([np.cos(ph), -np.sin(ph)], axis=0)
    eye4 = np.eye(4)
    cbd = np.kron(eye4, c64)
    sbd = np.kron(eye4, s64)
    bd = np.kron(np.eye(8), np.ones((HD, HD)))
    return g, m2, fc, cbd, sbd, bd


_G_NP, _M2_NP, _FC_NP, _CBD_NP, _SBD_NP, _BD_NP = _dft_tables()


def _rope_tables():
    rows = S // GRID_W
    row = jnp.broadcast_to(jnp.arange(rows)[:, None], (rows, GRID_W)).reshape(-1).astype(f32)
    col = jnp.broadcast_to(jnp.arange(GRID_W)[None, :], (rows, GRID_W)).reshape(-1).astype(f32)
    n_axis = HD // 4
    inv = ROPE_THETA ** (-jnp.arange(n_axis, dtype=f32) / n_axis)
    ang = jnp.concatenate([row[:, None] * inv, col[:, None] * inv], axis=-1)
    cos, sin = jnp.cos(ang), jnp.sin(ang)
    cos = jnp.concatenate([cos, jnp.ones((C, HD // 2), f32)], axis=0)
    sin = jnp.concatenate([sin, jnp.zeros((C, HD // 2), f32)], axis=0)
    cos128 = jnp.concatenate([cos, cos, cos, cos], axis=-1)
    sin128 = jnp.concatenate([-sin, sin, -sin, sin], axis=-1)
    return cos128, sin128


def _ada_kernel(c_ref, w_ref, b_ref, o_ref):
    s = _silu(c_ref[...])
    o_ref[0] = jnp.dot(s, w_ref[0], preferred_element_type=f32,
                       precision=lax.Precision.HIGHEST) + b_ref[0]


def _ada(c_rows, ada_w, ada_b):
    tn = 1536
    return pl.pallas_call(
        _ada_kernel,
        grid=(DEPTH, 6 * D // tn),
        in_specs=[pl.BlockSpec((8, D), lambda l, j: (0, 0)),
                  pl.BlockSpec((1, D, tn), lambda l, j: (l, 0, j)),
                  pl.BlockSpec((1, 1, tn), lambda l, j: (l, 0, j))],
        out_specs=pl.BlockSpec((1, 8, tn), lambda l, j: (l, 0, j)),
        out_shape=jax.ShapeDtypeStruct((DEPTH, 8, 6 * D), f32),
        compiler_params=_cparams("parallel", "parallel"),
        name="ada",
    )(c_rows, ada_w, ada_b.reshape(DEPTH, 1, 6 * D))


def _norm_mod(x, g, shift, scale):
    ms = jnp.mean(x * x, axis=-1, keepdims=True)
    return (x * lax.rsqrt(ms + EPS) * g) * (1.0 + scale) + shift


def _swap_halves(x):
    w = x.shape[-1]
    lane = lax.broadcasted_iota(jnp.int32, x.shape, 1)
    fwd = pltpu.roll(x, w - HD // 2, 1)
    bwd = pltpu.roll(x, HD // 2, 1)
    return jnp.where((lane % HD) < HD // 2, fwd, bwd)


def _rope(x, cos, sin):
    return x * cos + _swap_halves(x) * sin


def _head_rms(x, gain, bd):
    x2 = x * x
    hi = x2.astype(bf16)
    lo = (x2 - hi.astype(f32)).astype(bf16)
    ss = _dot(hi, bd) + _dot(lo, bd)
    return x * lax.rsqrt(ss * (1.0 / HD) + EPS) * gain


def _proj_in_kernel(h_ref, sh_ref, sc_ref, g_ref, w_ref, cos_ref, sin_ref, qg_ref, kg_ref, bd_ref,
                    fn_ref, rq_ref, rk_ref, rv_ref, rg_ref, aq_ref, akt_ref, av_ref, cv_ref, gt_ref):
    a = _norm_mod(h_ref[0], g_ref[...], sh_ref[0], sc_ref[0]).astype(bf16)

    def proj(c0, c1):
        return _dot(a, w_ref[:, c0:c1])

    cos = cos_ref[...]
    sin = sin_ref[...]
    cos2 = jnp.concatenate([cos, cos], axis=-1)
    sin2 = jnp.concatenate([sin, sin], axis=-1)
    cos4 = jnp.concatenate([cos2, cos2], axis=-1)
    sin4 = jnp.concatenate([sin2, sin2], axis=-1)

    fn_ref[0] = proj(0, 256).astype(bf16)
    rq_ref[0] = _rope(proj(256, 512), cos2, sin2).astype(bf16)
    rk_ref[0] = (_rope(proj(512, 768), cos2, sin2) * (HD ** -0.5)).astype(bf16)
    rv_ref[0] = proj(768, 1024).astype(bf16)
    rg_ref[0] = proj(1024, 1280).astype(bf16)

    q = _head_rms(proj(1280, 1792), qg_ref[...], bd_ref[...])
    q = (_rope(q, cos4, sin4) * (HD ** -0.5)).astype(bf16)
    for h in range(ATT_Q_HEADS):
        aq_ref[0, h] = q[:, h * HD:(h + 1) * HD]
    k = _head_rms(proj(1792, 1920), kg_ref[...], bd_ref[:2 * HD, :2 * HD])
    kt = _rope(k, cos, sin).T
    v = proj(1920, 2048).astype(bf16)
    for g in range(ATT_KV_HEADS):
        akt_ref[0, g, 0] = kt[g * HD:(g + 1) * HD, :].astype(bf16)
        av_ref[0, g] = v[:, g * HD:(g + 1) * HD]
    cv_ref[0] = proj(2048, 2560).astype(bf16)
    for j in range(4):
        gt_ref[0, :, j * D:(j + 1) * D] = proj(2560 + j * D, 2560 + (j + 1) * D).astype(bf16)


def _proj_in(h, mod, g1, w_in, cos128, sin128, qg, kg, bd):
    tile = lambda w: pl.BlockSpec((1, TM, w), lambda b, i: (b, i, 0))
    out_shapes = [
        jax.ShapeDtypeStruct((B, T, 256), bf16),
        jax.ShapeDtypeStruct((B, T, 256), bf16),
        jax.ShapeDtypeStruct((B, T, 256), bf16),
        jax.ShapeDtypeStruct((B, T, 256), bf16),
        jax.ShapeDtypeStruct((B, T, 256), bf16),
        jax.ShapeDtypeStruct((B, ATT_Q_HEADS, T, HD), bf16),
        jax.ShapeDtypeStruct((B, ATT_KV_HEADS, NT, HD, TM), bf16),
        jax.ShapeDtypeStruct((B, ATT_KV_HEADS, T, HD), bf16),
        jax.ShapeDtypeStruct((B, T, 512), bf16),
        jax.ShapeDtypeStruct((B, T, 4 * D), bf16),
    ]
    out_specs = [
        tile(256), tile(256), tile(256), tile(256), tile(256),
        pl.BlockSpec((1, ATT_Q_HEADS, TM, HD), lambda b, i: (b, 0, i, 0)),
        pl.BlockSpec((1, ATT_KV_HEADS, 1, HD, TM), lambda b, i: (b, 0, i, 0, 0)),
        pl.BlockSpec((1, ATT_KV_HEADS, TM, HD), lambda b, i: (b, 0, i, 0)),
        tile(512), tile(4 * D),
    ]
    return pl.pallas_call(
        _proj_in_kernel,
        grid=(B, NT),
        in_specs=[tile(D), _mod_spec(0), _mod_spec(1), _const_spec((1, D)), _const_spec((D, IN_COLS)),
                  pl.BlockSpec((TM, 128), lambda b, i: (i, 0)),
                  pl.BlockSpec((TM, 128), lambda b, i: (i, 0)),
                  _const_spec((1, 512)), _const_spec((1, 128)), _const_spec((512, 512))],
        out_specs=out_specs,
        out_shape=out_shapes,
        compiler_params=_cparams("parallel", "parallel"),
        name="proj_in",
    )(h, mod, mod, g1, w_in, cos128, sin128, qg, kg, bd)


def _attn_kernel(q_ref, k_ref, v_ref, o_ref):
    is_ctx = pl.program_id(2) == NL
    j0 = jnp.where(is_ctx, NL, 0)
    j1 = NT
    q = q_ref[0].reshape(ATT_GROUP * TM, HD)

    def body(j, carry):
        m, l, acc = carry
        kt = k_ref[0, 0, j]
        v = v_ref[0, 0, pl.ds(pl.multiple_of(j * TK, TK), TK), :]
        s = _dot(q, kt)
        m_new = jnp.maximum(m, jnp.max(s, axis=-1, keepdims=True))
        alpha = jnp.exp(m - m_new)
        p = jnp.exp(s - m_new)
        l = alpha * l + jnp.sum(p, axis=-1, keepdims=True)
        acc = alpha * acc + _dot(p.astype(bf16), v)
        return m_new, l, acc

    m0 = jnp.full((ATT_GROUP * TM, 1), -1e30, f32)
    l0 = jnp.zeros((ATT_GROUP * TM, 1), f32)
    a0 = jnp.zeros((ATT_GROUP * TM, HD), f32)
    _, l, acc = lax.fori_loop(j0, j1, body, (m0, l0, a0))
    o = (acc / l).reshape(ATT_GROUP, TM, HD)
    o_ref[0] = jnp.concatenate([o[g] for g in range(ATT_GROUP)], axis=-1).astype(bf16)


def _attention(aq, akt, av, n_tiles):
    return pl.pallas_call(
        _attn_kernel,
        grid=(B, ATT_KV_HEADS, n_tiles),
        in_specs=[pl.BlockSpec((1, ATT_GROUP, TM, HD), lambda b, g, i: (b, g, i, 0)),
                  pl.BlockSpec((1, 1, NT, HD, TK), lambda b, g, i: (b, g, 0, 0, 0)),
                  pl.BlockSpec((1, 1, T, HD), lambda b, g, i: (b, g, 0, 0))],
        out_specs=pl.BlockSpec((1, TM, ATT_GROUP * HD), lambda b, g, i: (b, i, g)),
        out_shape=jax.ShapeDtypeStruct((B, n_tiles * TM, ATT_Q_HEADS * HD), bf16),
        compiler_params=_cparams("parallel", "parallel", "arbitrary"),
        name="attention",
    )(aq, akt, av)


def _ret_state_update(s_ref, g_ref, k, v, zeta):
    kzt = (k.astype(f32) * zeta).T
    for h in range(RET_HEADS):
        rows = slice(h * HD, (h + 1) * HD)
        upd = _dot(kzt[rows, :].astype(bf16), v[:, rows])
        s_ref[rows, :] = g_ref[rows, :] * s_ref[rows, :] + upd


def _ret_bwd_kernel(k_ref, v_ref, zeta_ref, g_ref, sb_ref, s_ref):
    @pl.when(pl.program_id(1) == 0)
    def _():
        s_ref[...] = jnp.zeros_like(s_ref)

    sb_ref[0, 0] = s_ref[...]
    _ret_state_update(s_ref, g_ref, k_ref[0], v_ref[0], zeta_ref[...])


def _ret_bwd_states(rk, rv, zeta_b, g_b):
    chunk = lambda b, t: (b, N_CHUNK - 1 - t, 0)
    return pl.pallas_call(
        _ret_bwd_kernel,
        grid=(B, N_CHUNK),
        in_specs=[pl.BlockSpec((1, RET_CHUNK, 256), chunk),
                  pl.BlockSpec((1, RET_CHUNK, 256), chunk),
                  _const_spec((RET_CHUNK, 256)), _const_spec((256, HD))],
        out_specs=pl.BlockSpec((1, 1, 256, HD), lambda b, t: (b, N_CHUNK - 1 - t, 0, 0)),
        out_shape=jax.ShapeDtypeStruct((B, N_CHUNK, 256, HD), f32),
        scratch_shapes=[pltpu.VMEM((256, HD), f32)],
        compiler_params=_cparams("parallel", "arbitrary"),
        name="ret_bwd_states",
    )(rk, rv, zeta_b, g_b)


def _ret_fwd_kernel(q_ref, k_ref, v_ref, gate_ref, sb_ref, dmask_ref, xif_ref, xib_ref, zeta_ref, g_ref,
                    ng_ref, o_ref, s_ref):
    @pl.when(pl.program_id(1) == 0)
    def _():
        s_ref[...] = jnp.zeros_like(s_ref)

    q = q_ref[0]
    k = k_ref[0]
    v = v_ref[0]
    qf = q.astype(f32)
    q_xf = (qf * xif_ref[...]).astype(bf16)
    q_xb = (qf * xib_ref[...]).astype(bf16)
    outs = []
    for h in range(RET_HEADS):
        cols = slice(h * HD, (h + 1) * HD)
        s = lax.dot_general(q[:, cols], k[:, cols], (((1,), (1,)), ((), ())), preferred_element_type=f32)
        att = (s * dmask_ref[h]).astype(bf16)
        o = (_dot(att, v[:, cols])
             + _dot(q_xf[:, cols], s_ref[cols, :].astype(bf16))
             + _dot(q_xb[:, cols], sb_ref[0, 0, cols, :].astype(bf16)))
        ms = jnp.mean(o * o, axis=-1, keepdims=True)
        outs.append(o * lax.rsqrt(ms + EPS))
    y = jnp.concatenate(outs, axis=-1) * ng_ref[...]
    o_ref[0] = (y * _silu(gate_ref[0].astype(f32))).astype(bf16)
    _ret_state_update(s_ref, g_ref, k, v, zeta_ref[...])


def _retention(rq, rk, rv, rg, sb, dmask, xi_f, xi_b, zeta_f, g_f, ng):
    chunk = lambda b, t: (b, (t + N_LAT_CHUNK) % N_CHUNK, 0)
    blk = pl.BlockSpec((1, RET_CHUNK, 256), chunk)
    return pl.pallas_call(
        _ret_fwd_kernel,
        grid=(B, N_CHUNK),
        in_specs=[blk, blk, blk, blk,
                  pl.BlockSpec((1, 1, 256, HD), lambda b, t: (b, (t + N_LAT_CHUNK) % N_CHUNK, 0, 0)),
                  _const_spec((RET_HEADS, RET_CHUNK, RET_CHUNK)),
                  _const_spec((RET_CHUNK, 256)), _const_spec((RET_CHUNK, 256)), _const_spec((RET_CHUNK, 256)),
                  _const_spec((256, HD)), _const_spec((1, 256))],
        out_specs=blk,
        out_shape=jax.ShapeDtypeStruct((B, T, 256), bf16),
        scratch_shapes=[pltpu.VMEM((256, HD), f32)],
        compiler_params=_cparams("parallel", "arbitrary"),
        name="retention",
    )(rq, rk, rv, rg, sb, dmask, xi_f, xi_b, zeta_f, g_f, ng)


def _ret_tables(dec_f, dec_b):
    lg_f = jax.nn.log_sigmoid(dec_f.astype(f32))
    lg_b = jax.nn.log_sigmoid(dec_b.astype(f32))
    pos = jnp.arange(RET_CHUNK, dtype=f32)
    diff = pos[:, None] - pos[None, :]
    d_f = jnp.where(diff[None] >= 0.0, jnp.exp(jnp.maximum(diff, 0.0)[None] * lg_f[:, None, None]), 0.0)
    d_b = jnp.where(diff[None] <= 0.0, jnp.exp(jnp.maximum(-diff, 0.0)[None] * lg_b[:, None, None]), 0.0)
    lanes = lambda t: jnp.repeat(t, HD, axis=1)
    xi_f = lanes(jnp.exp((pos[:, None] + 1.0) * lg_f[None, :]))
    xi_b = lanes(jnp.exp((RET_CHUNK - pos[:, None]) * lg_b[None, :]))
    zeta_f = lanes(jnp.exp((RET_CHUNK - 1.0 - pos[:, None]) * lg_f[None, :]))
    zeta_b = lanes(jnp.exp(pos[:, None] * lg_b[None, :]))
    rows = lambda t: jnp.broadcast_to(jnp.repeat(t, HD)[:, None], (RET_HEADS * HD, HD))
    g_f = rows(jnp.exp(RET_CHUNK * lg_f))
    g_b = rows(jnp.exp(RET_CHUNK * lg_b))
    return d_f + d_b, xi_f, xi_b, zeta_f, zeta_b, g_f, g_b


def _conv_kernel(prev_ref, cur_ref, next_ref, w_ref, b_ref, lg_ref, lb_ref, o_ref, win_ref):
    i = pl.program_id(1)

    def glu(u):
        u = u.astype(f32)
        return u[:, :CONV_W] * jax.nn.sigmoid(u[:, CONV_W:])

    has_prev = jnp.logical_and(i >= 1, i < NL).astype(f32)
    has_next = (i < NL - 1).astype(f32)
    win_ref[0:HALO, :] = glu(prev_ref[0, TM - HALO:TM, :]) * has_prev
    win_ref[HALO:HALO + TM, :] = glu(cur_ref[0])
    win_ref[HALO + TM:2 * HALO + TM, :] = glu(next_ref[0, 0:HALO, :]) * has_next
    acc = jnp.zeros((TM, CONV_W), f32)
    off = HALO - CONV_K // 2
    for k in range(CONV_K):
        acc = acc + w_ref[k:k + 1, :] * win_ref[off + k:off + k + TM, :]
    y = acc + b_ref[...]
    mu = jnp.mean(y, axis=-1, keepdims=True)
    var = jnp.mean(jnp.square(y - mu), axis=-1, keepdims=True)
    y = (y - mu) * lax.rsqrt(var + EPS) * lg_ref[...] + lb_ref[...]
    o_ref[0] = _silu(y).astype(bf16)


def _conv(cv, dw_w, dw_b, ln_g, ln_b, n_tiles):
    return pl.pallas_call(
        _conv_kernel,
        grid=(B, n_tiles),
        in_specs=[pl.BlockSpec((1, TM, 512), lambda b, i: (b, jnp.maximum(i - 1, 0), 0)),
                  pl.BlockSpec((1, TM, 512), lambda b, i: (b, i, 0)),
                  pl.BlockSpec((1, TM, 512), lambda b, i: (b, jnp.minimum(i + 1, NT - 1), 0)),
                  _const_spec((CONV_K, CONV_W)), _const_spec((1, CONV_W)),
                  _const_spec((1, CONV_W)), _const_spec((1, CONV_W))],
        out_specs=pl.BlockSpec((1, TM, CONV_W), lambda b, i: (b, i, 0)),
        out_shape=jax.ShapeDtypeStruct((B, n_tiles * TM, CONV_W), bf16),
        scratch_shapes=[pltpu.VMEM((TM + 2 * HALO, CONV_W), f32)],
        compiler_params=_cparams("parallel", "parallel"),
        name="conv",
    )(cv, cv, cv, dw_w, dw_b, ln_g, ln_b)


def _fn_stage1_kernel(x_ref, g_ref, a_ref):
    for s in range(FN_STEP):
        res = _dot(g_ref[s], x_ref[0, :, s * 256:(s + 1) * 256])
        a_ref[0, 0, s] = res[:FN2].astype(bf16)
        a_ref[0, 1, s] = res[FN2:].astype(bf16)


def _fn_stage2_kernel(a_ref, m_ref, p_ref):
    res = _dot(m_ref[...], a_ref[0])
    p_ref[0, 0, :FN1] = res[:FN1].astype(bf16)
    p_ref[0, 1, :FN1] = res[FN1:].astype(bf16)
    if p_ref.shape[2] > FN1:
        p_ref[0, :, FN1:] = jnp.zeros((2, p_ref.shape[2] - FN1, p_ref.shape[3]), bf16)


def _fn_ctx_kernel(u_ref, f_ref, _p_in, p_ref):
    res = _dot(f_ref[...], u_ref[0])
    p_ref[0, 0] = res[:C].astype(bf16)
    p_ref[0, 1] = res[C:].astype(bf16)


def _fnet_dft(fn, g_tab, m2, fc, with_ctx):
    p_rows = T if with_ctx else S
    x = fn.reshape(B, T // FN1, FN1 * 256)
    a = pl.pallas_call(
        _fn_stage1_kernel,
        grid=(B, FN1 // FN_STEP),
        in_specs=[pl.BlockSpec((1, FN2, FN_STEP * 256), lambda b, j: (b, 0, j)),
                  pl.BlockSpec((FN_STEP, 2 * FN2, FN2), lambda b, j: (j, 0, 0))],
        out_specs=pl.BlockSpec((1, 2, FN_STEP, FN2, 256), lambda b, j: (b, 0, j, 0, 0)),
        out_shape=jax.ShapeDtypeStruct((B, 2, FN1, FN2, 256), bf16),
        compiler_params=_cparams("parallel", "parallel"),
        name="fnet_stage1",
    )(x, g_tab)
    a = a.reshape(B, 2 * FN1, FN2 * 256)
    p = pl.pallas_call(
        _fn_stage2_kernel,
        grid=(B, FN2 * 256 // FN_COLS),
        in_specs=[pl.BlockSpec((1, 2 * FN1, FN_COLS), lambda b, j: (b, 0, j)),
                  _const_spec((2 * FN1, 2 * FN1))],
        out_specs=pl.BlockSpec((1, 2, p_rows // FN2, FN_COLS), lambda b, j: (b, 0, 0, j)),
        out_shape=jax.ShapeDtypeStruct((B, 2, p_rows // FN2, FN2 * 256), bf16),
        compiler_params=_cparams("parallel", "parallel"),
        name="fnet_stage2",
    )(a, m2)
    p = p.reshape(B, 2, p_rows, 256)
    if with_ctx:
        p = pl.pallas_call(
            _fn_ctx_kernel,
            grid=(B,),
            in_specs=[pl.BlockSpec((1, C, 256), lambda b: (b, NL, 0)),
                      _const_spec((2 * C, C)),
                      pl.BlockSpec(memory_space=pl.ANY)],
            out_specs=pl.BlockSpec((1, 2, C, 256), lambda b: (b, 0, NL, 0)),
            out_shape=jax.ShapeDtypeStruct((B, 2, T, 256), bf16),
            input_output_aliases={2: 0},
            compiler_params=_cparams("parallel"),
            name="fnet_ctx",
        )(fn, fc, p)
    return p


def _merge_kernel(h_ref, gate_ref, p_ref, r_ref, o_ref, cv_ref, gt_ref, cbd_ref, sbd_ref,
                  fw_ref, rw_ref, aw_ref, cw_ref, ow_ref, out_ref):
    i = pl.program_id(1)
    scale = jnp.where(i == NL, (HD * C) ** -0.5, (HD * S) ** -0.5)
    yfn = (_dot(p_ref[0, 0], cbd_ref[...]) + _dot(p_ref[0, 1], sbd_ref[...])) * scale
    branches = (
        (yfn.astype(bf16), fw_ref),
        (r_ref[0], rw_ref),
        (o_ref[0], aw_ref),
        (cv_ref[0], cw_ref),
    )
    merged = jnp.zeros((TM, D), f32)
    for j, (xin, w_ref) in enumerate(branches):
        g = jax.nn.sigmoid(gt_ref[0, :, j * D:(j + 1) * D].astype(f32))
        merged = merged + g * _dot(xin, w_ref[...])
    out_ref[0] = h_ref[0] + gate_ref[0] * _dot(merged.astype(bf16), ow_ref[...])


def _merge(h, mod, p, r, o, cvo, gt, cbd, sbd, fw, rw, aw, cw, ow, n_tiles):
    tile = lambda w: pl.BlockSpec((1, TM, w), lambda b, i: (b, i, 0))
    return pl.pallas_call(
        _merge_kernel,
        grid=(B, n_tiles),
        in_specs=[tile(D), _mod_spec(2),
                  pl.BlockSpec((1, 2, TM, 256), lambda b, i: (b, 0, i, 0)),
                  tile(256), tile(512), tile(256), tile(4 * D),
                  _const_spec((256, 256)), _const_spec((256, 256)),
                  _const_spec((256, D)), _const_spec((256, D)), _const_spec((512, D)),
                  _const_spec((256, D)), _const_spec((D, D))],
        out_specs=tile(D),
        out_shape=jax.ShapeDtypeStruct((B, n_tiles * TM, D), f32),
        compiler_params=_cparams("parallel", "parallel"),
        name="merge",
    )(h, mod, p, r, o, cvo, gt, cbd, sbd, fw, rw, aw, cw, ow)


def _swiglu_rows(a, wg_ref, wu_ref, wd_ref, h_ref):
    for c in range(FFN // FFN_CHUNK):
        cols = slice(c * FFN_CHUNK, (c + 1) * FFN_CHUNK)
        h_ref[:, cols] = (_silu(_dot(a, wg_ref[:, cols])) * _dot(a, wu_ref[:, cols])).astype(bf16)
    return _dot(h_ref[...], wd_ref[...])


def _ffn_kernel(h_ref, sh_ref, sc_ref, gate_ref, g_ref, wg_ref, wu_ref, wd_ref, out_ref, hid_ref):
    x = h_ref[0]
    a = _norm_mod(x, g_ref[...], sh_ref[0], sc_ref[0]).astype(bf16)
    out_ref[0] = x + gate_ref[0] * _swiglu_rows(a, wg_ref, wu_ref, wd_ref, hid_ref)


def _ffn(h, mod, g2, wg, wu, wd):
    tile = pl.BlockSpec((1, TM, D), lambda b, i: (b, i, 0))
    return pl.pallas_call(
        _ffn_kernel,
        grid=(B, NT),
        in_specs=[tile, _mod_spec(3), _mod_spec(4), _mod_spec(5), _const_spec((1, D)),
                  _const_spec((D, FFN)), _const_spec((D, FFN)), _const_spec((FFN, D))],
        out_specs=tile,
        out_shape=jax.ShapeDtypeStruct((B, T, D), f32),
        scratch_shapes=[pltpu.VMEM((TM, FFN), bf16)],
        compiler_params=_cparams("parallel", "parallel"),
        name="ffn",
    )(h, mod, mod, mod, g2, wg, wu, wd)


def _moe_prep_kernel(h_ref, sh_ref, sc_ref, g_ref, rw_ref, x_ref, e_ref, w_ref):
    t = pl.program_id(0)
    f = _norm_mod(h_ref[0], g_ref[...], sh_ref[0], sc_ref[0])
    x_ref[...] = jnp.where(t < B * NL, f, 0.0)
    logits = jnp.dot(f, rw_ref[...], preferred_element_type=f32, precision=lax.Precision.HIGHEST)
    lane = lax.broadcasted_iota(jnp.int32, logits.shape, 1)
    neg = jnp.float32(-jnp.inf)
    logits = jnp.where(lane < N_EXPERTS, logits, neg)
    m1 = jnp.max(logits, axis=-1, keepdims=True)
    i1 = jnp.min(jnp.where(logits == m1, lane, 128), axis=-1, keepdims=True)
    rest = jnp.where(lane == i1, neg, logits)
    m2 = jnp.max(rest, axis=-1, keepdims=True)
    i2 = jnp.min(jnp.where(rest == m2, lane, 128), axis=-1, keepdims=True)
    z = jnp.exp(m2 - m1)
    w1 = 1.0 / (1.0 + z)
    w2 = z / (1.0 + z)
    e_ref[...] = jnp.where(lane == 0, i1, jnp.where(lane == 1, i2, 0))
    w_ref[...] = jnp.where(lane == 0, w1, jnp.where(lane == 1, w2, 0.0))


def _moe_prep(h, mod, g2, router_pad):
    n_steps = B * NL + 1

    def tok_tile(t):
        tt = jnp.minimum(t, B * NL - 1)
        return tt // NL, tt % NL

    def mod_spec(k):
        return pl.BlockSpec((1, 1, D), lambda t: (tok_tile(t)[0] * 6 + k, 0, 0))

    rows = n_steps * TM
    return pl.pallas_call(
        _moe_prep_kernel,
        grid=(n_steps,),
        in_specs=[pl.BlockSpec((1, TM, D), lambda t: (*tok_tile(t), 0)),
                  mod_spec(3), mod_spec(4), _const_spec((1, D)), _const_spec((D, 128))],
        out_specs=[pl.BlockSpec((TM, D), lambda t: (t, 0)),
                   pl.BlockSpec((TM, 128), lambda t: (t, 0)),
                   pl.BlockSpec((TM, 128), lambda t: (t, 0))],
        out_shape=[jax.ShapeDtypeStruct((rows, D), f32),
                   jax.ShapeDtypeStruct((rows, 128), jnp.int32),
                   jax.ShapeDtypeStruct((rows, 128), f32)],
        compiler_params=_cparams("parallel"),
        name="moe_prep",
    )(h, mod, mod, g2, router_pad)


def _moe_gather_kernel(tok_ref, x_hbm, o_ref, sem):
    def row_copy(r):
        return pltpu.make_async_copy(x_hbm.at[pl.ds(tok_ref[0, 0, r], 1), :], o_ref.at[pl.ds(r, 1), :], sem)

    def issue(r, carry):
        row_copy(r).start()
        return carry

    lax.fori_loop(0, MOE_BLOCK, issue, 0)

    def drain(r, carry):
        row_copy(r).wait()
        return carry

    lax.fori_loop(0, MOE_BLOCK, drain, 0)


def _moe_gather(buf_tok, xt):
    return pl.pallas_call(
        _moe_gather_kernel,
        grid=(N_BLOCKS,),
        in_specs=[pl.BlockSpec((1, 1, MOE_BLOCK), lambda i: (i, 0, 0), memory_space=pltpu.SMEM),
                  pl.BlockSpec(memory_space=pl.ANY)],
        out_specs=pl.BlockSpec((MOE_BLOCK, D), lambda i: (i, 0)),
        out_shape=jax.ShapeDtypeStruct((N_ROWS, D), f32),
        scratch_shapes=[pltpu.SemaphoreType.DMA(())],
        compiler_params=_cparams("arbitrary"),
        name="moe_gather",
    )(buf_tok.reshape(N_BLOCKS, 1, MOE_BLOCK), xt)


def _moe_expert_kernel(be_ref, x_ref, bw_ref, wg_ref, wu_ref, wd_ref, y_ref, hid_ref):
    a = x_ref[...].astype(bf16)
    y_ref[...] = _swiglu_rows(a, wg_ref.at[0], wu_ref.at[0], wd_ref.at[0], hid_ref) * bw_ref[...]


def _moe_experts(blk_e, xb, buf_w, wg, wu, wd):
    grid_spec = pltpu.PrefetchScalarGridSpec(
        num_scalar_prefetch=1,
        grid=(N_BLOCKS,),
        in_specs=[pl.BlockSpec((MOE_BLOCK, D), lambda i, be: (i, 0)),
                  pl.BlockSpec((MOE_BLOCK, 1), lambda i, be: (i, 0)),
                  pl.BlockSpec((1, D, FFN), lambda i, be: (be[i], 0, 0)),
                  pl.BlockSpec((1, D, FFN), lambda i, be: (be[i], 0, 0)),
                  pl.BlockSpec((1, FFN, D), lambda i, be: (be[i], 0, 0))],
        out_specs=pl.BlockSpec((MOE_BLOCK, D), lambda i, be: (i, 0)),
        scratch_shapes=[pltpu.VMEM((MOE_BLOCK, FFN), bf16)],
    )
    return pl.pallas_call(
        _moe_expert_kernel,
        grid_spec=grid_spec,
        out_shape=jax.ShapeDtypeStruct((N_ROWS, D), f32),
        compiler_params=_cparams("arbitrary"),
        name="moe_experts",
    )(blk_e, xb, buf_w.reshape(N_ROWS, 1), wg, wu, wd)


def _moe_combine_kernel(p0_ref, p1_ref, h_ref, gate_ref, y_hbm, out_ref, y0_ref, y1_ref, sem):
    def copies(r):
        c0 = pltpu.make_async_copy(y_hbm.at[pl.ds(p0_ref[0, 0, r], 1), :], y0_ref.at[pl.ds(r, 1), :], sem.at[0])
        c1 = pltpu.make_async_copy(y_hbm.at[pl.ds(p1_ref[0, 0, r], 1), :], y1_ref.at[pl.ds(r, 1), :], sem.at[1])
        return c0, c1

    def issue(r, carry):
        c0, c1 = copies(r)
        c0.start()
        c1.start()
        return carry

    lax.fori_loop(0, CMB, issue, 0)

    def drain(r, carry):
        c0, c1 = copies(r)
        c0.wait()
        c1.wait()
        return carry

    lax.fori_loop(0, CMB, drain, 0)
    out_ref[0] = h_ref[0] + gate_ref[0] * (y0_ref[...] + y1_ref[...])


def _moe_combine(pos0, pos1, h, mod, yb):
    n_per_b = S // CMB
    smem = pl.BlockSpec((1, 1, CMB), lambda b, i: (b * n_per_b + i, 0, 0), memory_space=pltpu.SMEM)
    return pl.pallas_call(
        _moe_combine_kernel,
        grid=(B, n_per_b),
        in_specs=[smem, smem,
                  pl.BlockSpec((1, CMB, D), lambda b, i: (b, i, 0)),
                  pl.BlockSpec((1, 1, D), lambda b, i: (b * 6 + 5, 0, 0)),
                  pl.BlockSpec(memory_space=pl.ANY)],
        out_specs=pl.BlockSpec((1, CMB, D), lambda b, i: (b, i, 0)),
        out_shape=jax.ShapeDtypeStruct((B, S, D), f32),
        scratch_shapes=[pltpu.VMEM((CMB, D), f32), pltpu.VMEM((CMB, D), f32), pltpu.SemaphoreType.DMA((2,))],
        compiler_params=_cparams("arbitrary", "arbitrary"),
        name="moe_combine",
    )(pos0.reshape(N_TOK // CMB, 1, CMB), pos1.reshape(N_TOK // CMB, 1, CMB), h, mod, yb)


def _moe_routing(top_e):
    e = top_e.reshape(-1)
    onehot = (e[:, None] == jnp.arange(N_EXPERTS, dtype=jnp.int32)[None, :]).astype(jnp.int32)
    counts = jnp.sum(onehot, axis=0)
    rank = jnp.sum((jnp.cumsum(onehot, axis=0) - 1) * onehot, axis=1)
    padded = (counts + MOE_BLOCK - 1) // MOE_BLOCK * MOE_BLOCK
    pad_end = jnp.cumsum(padded)
    pad_start = pad_end - padded
    dest = (pad_start[e] + rank).astype(jnp.int32)
    tok = jnp.repeat(jnp.arange(N_TOK, dtype=jnp.int32), 2)
    buf_tok = jnp.full((N_ROWS,), N_TOK, jnp.int32).at[dest].set(tok)
    blk_e = jnp.minimum(jnp.searchsorted(pad_end, jnp.arange(N_BLOCKS) * MOE_BLOCK, side='right'),
                        N_EXPERTS - 1).astype(jnp.int32)
    return buf_tok, dest, blk_e


def _moe(h, mod, g2, router_w, wg, wu, wd):
    router_pad = jnp.pad(router_w, ((0, 0), (0, 128 - N_EXPERTS)))
    xt, e_pad, w_pad = _moe_prep(h, mod, g2, router_pad)
    top_e = e_pad[:N_TOK, :2]
    top_w = w_pad[:N_TOK, :2]
    buf_tok, dest, blk_e = _moe_routing(top_e)
    buf_w = jnp.zeros((N_ROWS,), f32).at[dest].set(top_w.reshape(-1))
    xb = _moe_gather(buf_tok, xt)
    yb = _moe_experts(blk_e, xb, buf_w, wg, wu, wd)
    return _moe_combine(dest[0::2], dest[1::2], h, mod, yb)


def _token_mixers(h, mod, i, with_ctx, tabs, norm1_g, w_in, fnet_w, ret_decay_fwd, ret_decay_bwd, ret_norm_g,
                  ret_w, attn_qn_g, attn_kn_g, attn_w, conv_dw_w, conv_dw_b, conv_ln_g, conv_ln_b, conv_w_out,
                  w_out):
    cos128, sin128, g_tab, m2, fc, cbd, sbd, bd = tabs
    n_tiles = NT if with_ctx else NL
    qg = jnp.tile(attn_qn_g[i].astype(f32), ATT_Q_HEADS)[None, :]
    kg = jnp.tile(attn_kn_g[i].astype(f32), ATT_KV_HEADS)[None, :]
    fn, rq, rk, rv, rg, aq, akt, av, cv, gt = _proj_in(
        h, mod, norm1_g[i][None, :], w_in[i].astype(bf16), cos128, sin128, qg, kg, bd)

    o = _attention(aq, akt, av, n_tiles)

    dmask, xi_f, xi_b, zeta_f, zeta_b, g_f, g_b = _ret_tables(ret_decay_fwd[i], ret_decay_bwd[i])
    sb = _ret_bwd_states(rk, rv, zeta_b, g_b)
    r = _retention(rq, rk, rv, rg, sb, dmask, xi_f, xi_b, zeta_f, g_f, ret_norm_g[i][None, :].astype(f32))

    cvo = _conv(cv, conv_dw_w[i], conv_dw_b[i][None, :], conv_ln_g[i][None, :], conv_ln_b[i][None, :], n_tiles)
    p = _fnet_dft(fn, g_tab, m2, fc, with_ctx)
    return _merge(h, mod, p, r, o, cvo, gt, cbd, sbd, fnet_w[i].astype(bf16), ret_w[i].astype(bf16),
                  attn_w[i].astype(bf16), conv_w_out[i].astype(bf16), w_out[i].astype(bf16), n_tiles)


def kernel(x, c, ctx, c_ctx, ada_w, ada_b, norm1_g, norm2_g, w_in, fnet_w, ret_decay_fwd, ret_decay_bwd, ret_norm_g, ret_w, attn_qn_g, attn_kn_g, attn_w, conv_dw_w, conv_dw_b, conv_ln_g, conv_ln_b, conv_w_out, w_out, ffn_w_gate, ffn_w_up, ffn_w_down, router_w, moe_w_gate, moe_w_up, moe_w_down):
    cos128, sin128 = _rope_tables()
    tabs = (cos128, sin128, jnp.asarray(_G_NP, bf16), jnp.asarray(_M2_NP, bf16), jnp.asarray(_FC_NP, bf16),
            jnp.asarray(_CBD_NP, bf16), jnp.asarray(_SBD_NP, bf16), jnp.asarray(_BD_NP, bf16))
    c_rows = jnp.concatenate([c, c_ctx[None, :], jnp.zeros((8 - B - 1, D), f32)], axis=0)
    mods = _ada(c_rows, ada_w, ada_b).reshape(DEPTH, 8 * 6, 1, D)
    h = jnp.concatenate([x, ctx], axis=1)
    for i in range(DEPTH):
        with_ctx = i < DEPTH - 1
        mod = mods[i]
        h = _token_mixers(h, mod, i, with_ctx, tabs, norm1_g, w_in, fnet_w, ret_decay_fwd, ret_decay_bwd,
                          ret_norm_g, ret_w, attn_qn_g, attn_kn_g, attn_w, conv_dw_w, conv_dw_b, conv_ln_g,
                          conv_ln_b, conv_w_out, w_out)
        j = i // 2
        g2 = norm2_g[i][None, :]
        if i % 2 == 0:
            h = _ffn(h, mod, g2, ffn_w_gate[j].astype(bf16), ffn_w_up[j].astype(bf16),
                     ffn_w_down[j].astype(bf16))
        else:
            h = _moe(h, mod, g2, router_w[j], moe_w_gate[j].astype(bf16), moe_w_up[j].astype(bf16),
                     moe_w_down[j].astype(bf16))
    return h
```

```python
import functools
import math

import numpy as np
import jax
import jax.numpy as jnp
from jax import lax
from jax.experimental import pallas as pl
from jax.experimental.pallas import tpu as pltpu

f32 = jnp.float32
bf16 = jnp.bfloat16

D = 1024
B = 2
S = 8192
C = 256
T = S + C
DEPTH = 2
GRID_W = 64
HD = 64
EPS = 1e-6
RET_HEADS = 4
RET_CHUNK = 128
N_CHUNK = T // RET_CHUNK
N_LAT_CHUNK = S // RET_CHUNK
ATT_Q_HEADS = 8
ATT_KV_HEADS = 2
ATT_GROUP = 4
CONV_K = 31
CONV_W = 256
HALO = 16
IN_COLS = 6656
FFN = 2816
FFN_CHUNK = 256
N_EXPERTS = 8
MOE_BLOCK = 128
ROPE_THETA = 10000.0

TM = 256
NT = T // TM
NL = S // TM
TK = 256
VROWS = HD + 16
LOG2E = math.log2(math.e)

N_TOK = B * S
N_ASSIGN = N_TOK * 2
N_BLOCKS = N_ASSIGN // MOE_BLOCK + N_EXPERTS
N_ROWS = N_BLOCKS * MOE_BLOCK
CMB = 128

FN1 = 64
FN2 = 128
FN_STEP = 8
FN_COLS = 4096

VMEM_LIMIT = 56 * 1024 * 1024


def _cparams(*sem):
    return pltpu.CompilerParams(dimension_semantics=sem, vmem_limit_bytes=VMEM_LIMIT)


def _const_spec(shape):
    nd = len(shape)
    return pl.BlockSpec(shape, lambda *_: (0,) * nd, pipeline_mode=pl.Buffered(1))


def _mod_spec(k):
    return pl.BlockSpec((1, 1, D), lambda b, i: (jnp.where(i == NL, 2, b) * 6 + k, 0, 0))


def _dot(a, b):
    return jnp.dot(a, b, preferred_element_type=f32)


def _silu(x):
    return x * jax.nn.sigmoid(x)


def _dft_tables():
    n1 = np.arange(FN1)[:, None, None]
    k2 = np.arange(FN2)[None, :, None]
    n2 = np.arange(FN2)[None, None, :]
    ph = 2.0 * np.pi * (((n1 + FN1 * n2) * k2) % S) / S
    g = np.concatenate([np.cos(ph), -np.sin(ph)], axis=1)
    k1 = np.arange(FN1)[:, None]
    m1 = np.arange(FN1)[None, :]
    ph = 2.0 * np.pi * ((k1 * m1) % FN1) / FN1
    c64, s64 = np.cos(ph), np.sin(ph)
    m2 = np.block([[c64, s64], [-s64, c64]])
    kc = np.arange(C)[:, None]
    nc = np.arange(C)[None, :]
    ph = 2.0 * np.pi * ((kc * nc) % C) / C
    fc = np.concatenate([np.cos(ph), -np.sin(ph)], axis=0)
    eye4 = np.eye(4)
    cbd = np.kron(eye4, c64)
    sbd = np.kron(eye4, s64)
    bd = np.kron(np.eye(8), np.ones((HD, HD)))
    return g, m2, fc, cbd, sbd, bd


_G_NP, _M2_NP, _FC_NP, _CBD_NP, _SBD_NP, _BD_NP = _dft_tables()


def _rope_tables():
    rows = S // GRID_W
    row = jnp.broadcast_to(jnp.arange(rows)[:, None], (rows, GRID_W)).reshape(-1).astype(f32)
    col = jnp.broadcast_to(jnp.arange(GRID_W)[None, :], (rows, GRID_W)).reshape(-1).astype(f32)
    n_axis = HD // 4
    inv = ROPE_THETA ** (-jnp.arange(n_axis, dtype=f32) / n_axis)
    ang = jnp.concatenate([row[:, None] * inv, col[:, None] * inv], axis=-1)
    cos, sin = jnp.cos(ang), jnp.sin(ang)
    cos = jnp.concatenate([cos, jnp.ones((C, HD // 2), f32)], axis=0)
    sin = jnp.concatenate([sin, jnp.zeros((C, HD // 2), f32)], axis=0)
    cos128 = jnp.concatenate([cos, cos, cos, cos], axis=-1)
    sin128 = jnp.concatenate([-sin, sin, -sin, sin], axis=-1)
    return cos128, sin128


def _ada_kernel(c_ref, w_ref, b_ref, o_ref):
    s = _silu(c_ref[...])
    o_ref[0] = jnp.dot(s, w_ref[0], preferred_element_type=f32,
                       precision=lax.Precision.HIGHEST) + b_ref[0]


def _ada(c_rows, ada_w, ada_b):
    tn = 1536
    return pl.pallas_call(
        _ada_kernel,
        grid=(DEPTH, 6 * D // tn),
        in_specs=[pl.BlockSpec((8, D), lambda l, j: (0, 0)),
                  pl.BlockSpec((1, D, tn), lambda l, j: (l, 0, j)),
                  pl.BlockSpec((1, 1, tn), lambda l, j: (l, 0, j))],
        out_specs=pl.BlockSpec((1, 8, tn), lambda l, j: (l, 0, j)),
        out_shape=jax.ShapeDtypeStruct((DEPTH, 8, 6 * D), f32),
        compiler_params=_cparams("parallel", "parallel"),
        name="ada",
    )(c_rows, ada_w, ada_b.reshape(DEPTH, 1, 6 * D))


def _norm_mod(x, g, shift, scale):
    ms = jnp.mean(x * x, axis=-1, keepdims=True)
    return (x * lax.rsqrt(ms + EPS) * g) * (1.0 + scale) + shift


def _swap_halves(x):
    w = x.shape[-1]
    lane = lax.broadcasted_iota(jnp.int32, x.shape, 1)
    fwd = pltpu.roll(x, w - HD // 2, 1)
    bwd = pltpu.roll(x, HD // 2, 1)
    return jnp.where((lane % HD) < HD // 2, fwd, bwd)


def _rope(x, cos, sin):
    return x * cos + _swap_halves(x) * sin


def _head_rms(x, gain, bd):
    x2 = x * x
    hi = x2.astype(bf16)
    lo = (x2 - hi.astype(f32)).astype(bf16)
    ss = _dot(hi, bd) + _dot(lo, bd)
    return x * lax.rsqrt(ss * (1.0 / HD) + EPS) * gain


def _proj_in_kernel(h_ref, sh_ref, sc_ref, g_ref, w_ref, cos_ref, sin_ref, qg_ref, kg_ref, bd_ref,
                    fn_ref, rq_ref, rk_ref, rv_ref, rg_ref, aq_ref, ak_ref, avt_ref, cv_ref, gt_ref):
    a = _norm_mod(h_ref[0], g_ref[...], sh_ref[0], sc_ref[0]).astype(bf16)

    def proj(c0, c1):
        return _dot(a, w_ref[:, c0:c1])

    cos = cos_ref[...]
    sin = sin_ref[...]
    cos2 = jnp.concatenate([cos, cos], axis=-1)
    sin2 = jnp.concatenate([sin, sin], axis=-1)
    cos4 = jnp.concatenate([cos2, cos2], axis=-1)
    sin4 = jnp.concatenate([sin2, sin2], axis=-1)

    fn_ref[0] = proj(0, 256).astype(bf16)
    rq_ref[0] = _rope(proj(256, 512), cos2, sin2).astype(bf16)
    rk_ref[0] = (_rope(proj(512, 768), cos2, sin2) * (HD ** -0.5)).astype(bf16)
    rv_ref[0] = proj(768, 1024).astype(bf16)
    rg_ref[0] = proj(1024, 1280).astype(bf16)

    q = _head_rms(proj(1280, 1792), qg_ref[...], bd_ref[...])
    qt = (_rope(q, cos4, sin4) * (HD ** -0.5 * LOG2E)).T
    for h in range(ATT_Q_HEADS):
        aq_ref[0, h] = qt[h * HD:(h + 1) * HD, :].astype(bf16)
    k = _head_rms(proj(1792, 1920), kg_ref[...], bd_ref[:2 * HD, :2 * HD])
    k = _rope(k, cos, sin).astype(bf16)
    vt = proj(1920, 2048).T
    row = lax.broadcasted_iota(jnp.int32, (VROWS - HD, TM), 0)
    ones_row = jnp.where(row == 0, 1.0, 0.0).astype(bf16)
    for g in range(ATT_KV_HEADS):
        ak_ref[0, g] = k[:, g * HD:(g + 1) * HD]
        avt_ref[0, g, 0, :HD] = vt[g * HD:(g + 1) * HD, :].astype(bf16)
        avt_ref[0, g, 0, HD:] = ones_row
    cv_ref[0] = proj(2048, 2560).astype(bf16)
    for j in range(4):
        gt_ref[0, :, j * D:(j + 1) * D] = proj(2560 + j * D, 2560 + (j + 1) * D).astype(bf16)


def _proj_in(h, mod, g1, w_in, cos128, sin128, qg, kg, bd):
    tile = lambda w: pl.BlockSpec((1, TM, w), lambda b, i: (b, i, 0))
    out_shapes = [
        jax.ShapeDtypeStruct((B, T, 256), bf16),
        jax.ShapeDtypeStruct((B, T, 256), bf16),
        jax.ShapeDtypeStruct((B, T, 256), bf16),
        jax.ShapeDtypeStruct((B, T, 256), bf16),
        jax.ShapeDtypeStruct((B, T, 256), bf16),
        jax.ShapeDtypeStruct((B, ATT_Q_HEADS, HD, T), bf16),
        jax.ShapeDtypeStruct((B, ATT_KV_HEADS, T, HD), bf16),
        jax.ShapeDtypeStruct((B, ATT_KV_HEADS, NT, VROWS, TM), bf16),
        jax.ShapeDtypeStruct((B, T, 512), bf16),
        jax.ShapeDtypeStruct((B, T, 4 * D), bf16),
    ]
    out_specs = [
        tile(256), tile(256), tile(256), tile(256), tile(256),
        pl.BlockSpec((1, ATT_Q_HEADS, HD, TM), lambda b, i: (b, 0, 0, i)),
        pl.BlockSpec((1, ATT_KV_HEADS, TM, HD), lambda b, i: (b, 0, i, 0)),
        pl.BlockSpec((1, ATT_KV_HEADS, 1, VROWS, TM), lambda b, i: (b, 0, i, 0, 0)),
        tile(512), tile(4 * D),
    ]
    return pl.pallas_call(
        _proj_in_kernel,
        grid=(B, NT),
        in_specs=[tile(D), _mod_spec(0), _mod_spec(1), _const_spec((1, D)), _const_spec((D, IN_COLS)),
                  pl.BlockSpec((TM, 128), lambda b, i: (i, 0)),
                  pl.BlockSpec((TM, 128), lambda b, i: (i, 0)),
                  _const_spec((1, 512)), _const_spec((1, 128)), _const_spec((512, 512))],
        out_specs=out_specs,
        out_shape=out_shapes,
        compiler_params=_cparams("parallel", "parallel"),
        name="proj_in",
    )(h, mod, mod, g1, w_in, cos128, sin128, qg, kg, bd)


def _attn_kernel(q_ref, k_ref, v_ref, o_ref, s_ref, p_ref):
    nq = ATT_GROUP * TM
    is_ctx = pl.program_id(2) == NL
    j0 = jnp.where(is_ctx, NL, 0)
    n_pairs = jnp.where(is_ctx, 0, (NT - 1) // 2)
    qt = jnp.concatenate([q_ref[0, h] for h in range(ATT_GROUP)], axis=-1)

    def scores(t, slot):
        k = k_ref[0, 0, pl.ds(pl.multiple_of(t * TK, TK), TK), :]
        s_ref[slot] = _dot(k, qt)

    def weighted_values(t, slot, alpha, acc):
        return alpha * acc + _dot(v_ref[0, 0, t], p_ref[slot])

    def softmax(slot, m):
        s = s_ref[slot]
        m_new = jnp.maximum(m, jnp.max(s, axis=0, keepdims=True))
        p_ref[slot] = jnp.exp2(s - m_new).astype(bf16)
        return m_new, jnp.exp2(m - m_new)

    def pair(i, carry):
        m, alpha, acc = carry
        t = j0 + 2 * i
        scores(t + 1, 1)
        acc = weighted_values(jnp.maximum(t - 1, j0), 1, alpha, acc)
        m, alpha = softmax(0, m)
        scores(t + 2, 0)
        acc = weighted_values(t, 0, alpha, acc)
        m, alpha = softmax(1, m)
        return m, alpha, acc

    scores(j0, 0)
    p_ref[1] = jnp.zeros((TK, nq), bf16)
    init = (jnp.full((1, nq), -1e30, f32), jnp.ones((1, nq), f32), jnp.zeros((VROWS, nq), f32))
    m, alpha, acc = lax.fori_loop(0, n_pairs, pair, init)
    last = NT - 1
    acc = weighted_values(jnp.maximum(last - 1, j0), 1, alpha, acc)
    m, alpha = softmax(0, m)
    acc = weighted_values(last, 0, alpha, acc)
    ot = acc[:HD] / acc[HD:HD + 1]
    o_ref[0] = jnp.concatenate([ot[:, h * TM:(h + 1) * TM].T for h in range(ATT_GROUP)],
                               axis=-1).astype(bf16)


def _attention(aq, ak, avt, n_tiles):
    return pl.pallas_call(
        _attn_kernel,
        grid=(B, ATT_KV_HEADS, n_tiles),
        in_specs=[pl.BlockSpec((1, ATT_GROUP, HD, TM), lambda b, g, i: (b, g, 0, i)),
                  pl.BlockSpec((1, 1, T, HD), lambda b, g, i: (b, g, 0, 0)),
                  pl.BlockSpec((1, 1, NT, VROWS, TK), lambda b, g, i: (b, g, 0, 0, 0))],
        out_specs=pl.BlockSpec((1, TM, ATT_GROUP * HD), lambda b, g, i: (b, i, g)),
        out_shape=jax.ShapeDtypeStruct((B, n_tiles * TM, ATT_Q_HEADS * HD), bf16),
        scratch_shapes=[pltpu.VMEM((2, TK, ATT_GROUP * TM), f32), pltpu.VMEM((2, TK, ATT_GROUP * TM), bf16)],
        compiler_params=_cparams("parallel", "parallel", "arbitrary"),
        name="attention",
    )(aq, ak, avt)


def _ret_state_update(s_ref, g_ref, k, v, zeta):
    kzt = (k.astype(f32) * zeta).T
    for h in range(RET_HEADS):
        rows = slice(h * HD, (h + 1) * HD)
        upd = _dot(kzt[rows, :].astype(bf16), v[:, rows])
        s_ref[rows, :] = g_ref[rows, :] * s_ref[rows, :] + upd


def _ret_bwd_kernel(k_ref, v_ref, zeta_ref, g_ref, sb_ref, s_ref):
    @pl.when(pl.program_id(1) == 0)
    def _():
        s_ref[...] = jnp.zeros_like(s_ref)

    sb_ref[0, 0] = s_ref[...]
    _ret_state_update(s_ref, g_ref, k_ref[0], v_ref[0], zeta_ref[...])


def _ret_bwd_states(rk, rv, zeta_b, g_b):
    chunk = lambda b, t: (b, N_CHUNK - 1 - t, 0)
    return pl.pallas_call(
        _ret_bwd_kernel,
        grid=(B, N_CHUNK),
        in_specs=[pl.BlockSpec((1, RET_CHUNK, 256), chunk),
                  pl.BlockSpec((1, RET_CHUNK, 256), chunk),
                  _const_spec((RET_CHUNK, 256)), _const_spec((256, HD))],
        out_specs=pl.BlockSpec((1, 1, 256, HD), lambda b, t: (b, N_CHUNK - 1 - t, 0, 0)),
        out_shape=jax.ShapeDtypeStruct((B, N_CHUNK, 256, HD), f32),
        scratch_shapes=[pltpu.VMEM((256, HD), f32)],
        compiler_params=_cparams("parallel", "arbitrary"),
        name="ret_bwd_states",
    )(rk, rv, zeta_b, g_b)


def _ret_fwd_kernel(q_ref, k_ref, v_ref, gate_ref, sb_ref, dmask_ref, xif_ref, xib_ref, zeta_ref, g_ref,
                    ng_ref, o_ref, s_ref):
    @pl.when(pl.program_id(1) == 0)
    def _():
        s_ref[...] = jnp.zeros_like(s_ref)

    q = q_ref[0]
    k = k_ref[0]
    v = v_ref[0]
    qf = q.astype(f32)
    q_xf = (qf * xif_ref[...]).astype(bf16)
    q_xb = (qf * xib_ref[...]).astype(bf16)
    outs = []
    for h in range(RET_HEADS):
        cols = slice(h * HD, (h + 1) * HD)
        s = lax.dot_general(q[:, cols], k[:, cols], (((1,), (1,)), ((), ())), preferred_element_type=f32)
        att = (s * dmask_ref[h]).astype(bf16)
        o = (_dot(att, v[:, cols])
             + _dot(q_xf[:, cols], s_ref[cols, :].astype(bf16))
             + _dot(q_xb[:, cols], sb_ref[0, 0, cols, :].astype(bf16)))
        ms = jnp.mean(o * o, axis=-1, keepdims=True)
        outs.append(o * lax.rsqrt(ms + EPS))
    y = jnp.concatenate(outs, axis=-1) * ng_ref[...]
    o_ref[0] = (y * _silu(gate_ref[0].astype(f32))).astype(bf16)
    _ret_state_update(s_ref, g_ref, k, v, zeta_ref[...])


def _retention(rq, rk, rv, rg, sb, dmask, xi_f, xi_b, zeta_f, g_f, ng):
    chunk = lambda b, t: (b, (t + N_LAT_CHUNK) % N_CHUNK, 0)
    blk = pl.BlockSpec((1, RET_CHUNK, 256), chunk)
    return pl.pallas_call(
        _ret_fwd_kernel,
        grid=(B, N_CHUNK),
        in_specs=[blk, blk, blk, blk,
                  pl.BlockSpec((1, 1, 256, HD), lambda b, t: (b, (t + N_LAT_CHUNK) % N_CHUNK, 0, 0)),
                  _const_spec((RET_HEADS, RET_CHUNK, RET_CHUNK)),
                  _const_spec((RET_CHUNK, 256)), _const_spec((RET_CHUNK, 256)), _const_spec((RET_CHUNK, 256)),
                  _const_spec((256, HD)), _const_spec((1, 256))],
        out_specs=blk,
        out_shape=jax.ShapeDtypeStruct((B, T, 256), bf16),
        scratch_shapes=[pltpu.VMEM((256, HD), f32)],
        compiler_params=_cparams("parallel", "arbitrary"),
        name="retention",
    )(rq, rk, rv, rg, sb, dmask, xi_f, xi_b, zeta_f, g_f, ng)


def _ret_tables(dec_f, dec_b):
    lg_f = jax.nn.log_sigmoid(dec_f.astype(f32))
    lg_b = jax.nn.log_sigmoid(dec_b.astype(f32))
    pos = jnp.arange(RET_CHUNK, dtype=f32)
    diff = pos[:, None] - pos[None, :]
    d_f = jnp.where(diff[None] >= 0.0, jnp.exp(jnp.maximum(diff, 0.0)[None] * lg_f[:, None, None]), 0.0)
    d_b = jnp.where(diff[None] <= 0.0, jnp.exp(jnp.maximum(-diff, 0.0)[None] * lg_b[:, None, None]), 0.0)
    lanes = lambda t: jnp.repeat(t, HD, axis=1)
    xi_f = lanes(jnp.exp((pos[:, None] + 1.0) * lg_f[None, :]))
    xi_b = lanes(jnp.exp((RET_CHUNK - pos[:, None]) * lg_b[None, :]))
    zeta_f = lanes(jnp.exp((RET_CHUNK - 1.0 - pos[:, None]) * lg_f[None, :]))
    zeta_b = lanes(jnp.exp(pos[:, None] * lg_b[None, :]))
    rows = lambda t: jnp.broadcast_to(jnp.repeat(t, HD)[:, None], (RET_HEADS * HD, HD))
    g_f = rows(jnp.exp(RET_CHUNK * lg_f))
    g_b = rows(jnp.exp(RET_CHUNK * lg_b))
    return d_f + d_b, xi_f, xi_b, zeta_f, zeta_b, g_f, g_b


def _conv_kernel(prev_ref, cur_ref, next_ref, w_ref, b_ref, lg_ref, lb_ref, o_ref, win_ref):
    i = pl.program_id(1)

    def glu(u):
        u = u.astype(f32)
        return u[:, :CONV_W] * jax.nn.sigmoid(u[:, CONV_W:])

    has_prev = jnp.logical_and(i >= 1, i < NL).astype(f32)
    has_next = (i < NL - 1).astype(f32)
    win_ref[0:HALO, :] = glu(prev_ref[0, TM - HALO:TM, :]) * has_prev
    win_ref[HALO:HALO + TM, :] = glu(cur_ref[0])
    win_ref[HALO + TM:2 * HALO + TM, :] = glu(next_ref[0, 0:HALO, :]) * has_next
    acc = jnp.zeros((TM, CONV_W), f32)
    off = HALO - CONV_K // 2
    for k in range(CONV_K):
        acc = acc + w_ref[k:k + 1, :] * win_ref[off + k:off + k + TM, :]
    y = acc + b_ref[...]
    mu = jnp.mean(y, axis=-1, keepdims=True)
    var = jnp.mean(jnp.square(y - mu), axis=-1, keepdims=True)
    y = (y - mu) * lax.rsqrt(var + EPS) * lg_ref[...] + lb_ref[...]
    o_ref[0] = _silu(y).astype(bf16)


def _conv(cv, dw_w, dw_b, ln_g, ln_b, n_tiles):
    return pl.pallas_call(
        _conv_kernel,
        grid=(B, n_tiles),
        in_specs=[pl.BlockSpec((1, TM, 512), lambda b, i: (b, jnp.maximum(i - 1, 0), 0)),
                  pl.BlockSpec((1, TM, 512), lambda b, i: (b, i, 0)),
                  pl.BlockSpec((1, TM, 512), lambda b, i: (b, jnp.minimum(i + 1, NT - 1), 0)),
                  _const_spec((CONV_K, CONV_W)), _const_spec((1, CONV_W)),
                  _const_spec((1, CONV_W)), _const_spec((1, CONV_W))],
        out_specs=pl.BlockSpec((1, TM, CONV_W), lambda b, i: (b, i, 0)),
        out_shape=jax.ShapeDtypeStruct((B, n_tiles * TM, CONV_W), bf16),
        scratch_shapes=[pltpu.VMEM((TM + 2 * HALO, CONV_W), f32)],
        compiler_params=_cparams("parallel", "parallel"),
        name="conv",
    )(cv, cv, cv, dw_w, dw_b, ln_g, ln_b)


def _fn_stage1_kernel(x_ref, g_ref, a_ref):
    for s in range(FN_STEP):
        res = _dot(g_ref[s], x_ref[0, :, s * 256:(s + 1) * 256])
        a_ref[0, 0, s] = res[:FN2].astype(bf16)
        a_ref[0, 1, s] = res[FN2:].astype(bf16)


def _fn_stage2_kernel(a_ref, m_ref, p_ref):
    res = _dot(m_ref[...], a_ref[0])
    p_ref[0, 0, :FN1] = res[:FN1].astype(bf16)
    p_ref[0, 1, :FN1] = res[FN1:].astype(bf16)
    if p_ref.shape[2] > FN1:
        p_ref[0, :, FN1:] = jnp.zeros((2, p_ref.shape[2] - FN1, p_ref.shape[3]), bf16)


def _fn_ctx_kernel(u_ref, f_ref, _p_in, p_ref):
    res = _dot(f_ref[...], u_ref[0])
    p_ref[0, 0] = res[:C].astype(bf16)
    p_ref[0, 1] = res[C:].astype(bf16)


def _fnet_dft(fn, g_tab, m2, fc, with_ctx):
    p_rows = T if with_ctx else S
    x = fn.reshape(B, T // FN1, FN1 * 256)
    a = pl.pallas_call(
        _fn_stage1_kernel,
        grid=(B, FN1 // FN_STEP),
        in_specs=[pl.BlockSpec((1, FN2, FN_STEP * 256), lambda b, j: (b, 0, j)),
                  pl.BlockSpec((FN_STEP, 2 * FN2, FN2), lambda b, j: (j, 0, 0))],
        out_specs=pl.BlockSpec((1, 2, FN_STEP, FN2, 256), lambda b, j: (b, 0, j, 0, 0)),
        out_shape=jax.ShapeDtypeStruct((B, 2, FN1, FN2, 256), bf16),
        compiler_params=_cparams("parallel", "parallel"),
        name="fnet_stage1",
    )(x, g_tab)
    a = a.reshape(B, 2 * FN1, FN2 * 256)
    p = pl.pallas_call(
        _fn_stage2_kernel,
        grid=(B, FN2 * 256 // FN_COLS),
        in_specs=[pl.BlockSpec((1, 2 * FN1, FN_COLS), lambda b, j: (b, 0, j)),
                  _const_spec((2 * FN1, 2 * FN1))],
        out_specs=pl.BlockSpec((1, 2, p_rows // FN2, FN_COLS), lambda b, j: (b, 0, 0, j)),
        out_shape=jax.ShapeDtypeStruct((B, 2, p_rows // FN2, FN2 * 256), bf16),
        compiler_params=_cparams("parallel", "parallel"),
        name="fnet_stage2",
    )(a, m2)
    p = p.reshape(B, 2, p_rows, 256)
    if with_ctx:
        p = pl.pallas_call(
            _fn_ctx_kernel,
            grid=(B,),
            in_specs=[pl.BlockSpec((1, C, 256), lambda b: (b, NL, 0)),
                      _const_spec((2 * C, C)),
                      pl.BlockSpec(memory_space=pl.ANY)],
            out_specs=pl.BlockSpec((1, 2, C, 256), lambda b: (b, 0, NL, 0)),
            out_shape=jax.ShapeDtypeStruct((B, 2, T, 256), bf16),
            input_output_aliases={2: 0},
            compiler_params=_cparams("parallel"),
            name="fnet_ctx",
        )(fn, fc, p)
    return p


def _merge_kernel(h_ref, gate_ref, p_ref, r_ref, o_ref, cv_ref, gt_ref, cbd_ref, sbd_ref,
                  fw_ref, rw_ref, aw_ref, cw_ref, ow_ref, out_ref):
    i = pl.program_id(1)
    scale = jnp.where(i == NL, (HD * C) ** -0.5, (HD * S) ** -0.5)
    yfn = (_dot(p_ref[0, 0], cbd_ref[...]) + _dot(p_ref[0, 1], sbd_ref[...])) * scale
    branches = (
        (yfn.astype(bf16), fw_ref),
        (r_ref[0], rw_ref),
        (o_ref[0], aw_ref),
        (cv_ref[0], cw_ref),
    )
    merged = jnp.zeros((TM, D), f32)
    for j, (xin, w_ref) in enumerate(branches):
        g = jax.nn.sigmoid(gt_ref[0, :, j * D:(j + 1) * D].astype(f32))
        merged = merged + g * _dot(xin, w_ref[...])
    out_ref[0] = h_ref[0] + gate_ref[0] * _dot(merged.astype(bf16), ow_ref[...])


def _merge(h, mod, p, r, o, cvo, gt, cbd, sbd, fw, rw, aw, cw, ow, n_tiles):
    tile = lambda w: pl.BlockSpec((1, TM, w), lambda b, i: (b, i, 0))
    return pl.pallas_call(
        _merge_kernel,
        grid=(B, n_tiles),
        in_specs=[tile(D), _mod_spec(2),
                  pl.BlockSpec((1, 2, TM, 256), lambda b, i: (b, 0, i, 0)),
                  tile(256), tile(512), tile(256), tile(4 * D),
                  _const_spec((256, 256)), _const_spec((256, 256)),
                  _const_spec((256, D)), _const_spec((256, D)), _const_spec((512, D)),
                  _const_spec((256, D)), _const_spec((D, D))],
        out_specs=tile(D),
        out_shape=jax.ShapeDtypeStruct((B, n_tiles * TM, D), f32),
        compiler_params=_cparams("parallel", "parallel"),
        name="merge",
    )(h, mod, p, r, o, cvo, gt, cbd, sbd, fw, rw, aw, cw, ow)


def _swiglu_rows(a, wg_ref, wu_ref, wd_ref, h_ref):
    for c in range(FFN // FFN_CHUNK):
        cols = slice(c * FFN_CHUNK, (c + 1) * FFN_CHUNK)
        h_ref[:, cols] = (_silu(_dot(a, wg_ref[:, cols])) * _dot(a, wu_ref[:, cols])).astype(bf16)
    return _dot(h_ref[...], wd_ref[...])


def _ffn_kernel(h_ref, sh_ref, sc_ref, gate_ref, g_ref, wg_ref, wu_ref, wd_ref, out_ref, hid_ref):
    x = h_ref[0]
    a = _norm_mod(x, g_ref[...], sh_ref[0], sc_ref[0]).astype(bf16)
    out_ref[0] = x + gate_ref[0] * _swiglu_rows(a, wg_ref, wu_ref, wd_ref, hid_ref)


def _ffn(h, mod, g2, wg, wu, wd):
    tile = pl.BlockSpec((1, TM, D), lambda b, i: (b, i, 0))
    return pl.pallas_call(
        _ffn_kernel,
        grid=(B, NT),
        in_specs=[tile, _mod_spec(3), _mod_spec(4), _mod_spec(5), _const_spec((1, D)),
                  _const_spec((D, FFN)), _const_spec((D, FFN)), _const_spec((FFN, D))],
        out_specs=tile,
        out_shape=jax.ShapeDtypeStruct((B, T, D), f32),
        scratch_shapes=[pltpu.VMEM((TM, FFN), bf16)],
        compiler_params=_cparams("parallel", "parallel"),
        name="ffn",
    )(h, mod, mod, mod, g2, wg, wu, wd)


def _moe_prep_kernel(h_ref, sh_ref, sc_ref, g_ref, rw_ref, x_ref, e_ref, w_ref):
    t = pl.program_id(0)
    f = _norm_mod(h_ref[0], g_ref[...], sh_ref[0], sc_ref[0])
    x_ref[...] = jnp.where(t < B * NL, f, 0.0)
    logits = jnp.dot(f, rw_ref[...], preferred_element_type=f32, precision=lax.Precision.HIGHEST)
    lane = lax.broadcasted_iota(jnp.int32, logits.shape, 1)
    neg = jnp.float32(-jnp.inf)
    logits = jnp.where(lane < N_EXPERTS, logits, neg)
    m1 = jnp.max(logits, axis=-1, keepdims=True)
    i1 = jnp.min(jnp.where(logits == m1, lane, 128), axis=-1, keepdims=True)
    rest = jnp.where(lane == i1, neg, logits)
    m2 = jnp.max(rest, axis=-1, keepdims=True)
    i2 = jnp.min(jnp.where(rest == m2, lane, 128), axis=-1, keepdims=True)
    z = jnp.exp(m2 - m1)
    w1 = 1.0 / (1.0 + z)
    w2 = z / (1.0 + z)
    e_ref[...] = jnp.where(lane == 0, i1, jnp.where(lane == 1, i2, 0))
    w_ref[...] = jnp.where(lane == 0, w1, jnp.where(lane == 1, w2, 0.0))


def _moe_prep(h, mod, g2, router_pad):
    n_steps = B * NL + 1

    def tok_tile(t):
        tt = jnp.minimum(t, B * NL - 1)
        return tt // NL, tt % NL

    def mod_spec(k):
        return pl.BlockSpec((1, 1, D), lambda t: (tok_tile(t)[0] * 6 + k, 0, 0))

    rows = n_steps * TM
    return pl.pallas_call(
        _moe_prep_kernel,
        grid=(n_steps,),
        in_specs=[pl.BlockSpec((1, TM, D), lambda t: (*tok_tile(t), 0)),
                  mod_spec(3), mod_spec(4), _const_spec((1, D)), _const_spec((D, 128))],
        out_specs=[pl.BlockSpec((TM, D), lambda t: (t, 0)),
                   pl.BlockSpec((TM, 128), lambda t: (t, 0)),
                   pl.BlockSpec((TM, 128), lambda t: (t, 0))],
        out_shape=[jax.ShapeDtypeStruct((rows, D), f32),
                   jax.ShapeDtypeStruct((rows, 128), jnp.int32),
                   jax.ShapeDtypeStruct((rows, 128), f32)],
        compiler_params=_cparams("parallel"),
        name="moe_prep",
    )(h, mod, mod, g2, router_pad)


def _moe_gather_kernel(tok_ref, x_hbm, o_ref, sem):
    def row_copy(r):
        return pltpu.make_async_copy(x_hbm.at[pl.ds(tok_ref[0, 0, r], 1), :], o_ref.at[pl.ds(r, 1), :], sem)

    def issue(r, carry):
        row_copy(r).start()
        return carry

    lax.fori_loop(0, MOE_BLOCK, issue, 0)

    def drain(r, carry):
        row_copy(r).wait()
        return carry

    lax.fori_loop(0, MOE_BLOCK, drain, 0)


def _moe_gather(buf_tok, xt):
    return pl.pallas_call(
        _moe_gather_kernel,
        grid=(N_BLOCKS,),
        in_specs=[pl.BlockSpec((1, 1, MOE_BLOCK), lambda i: (i, 0, 0), memory_space=pltpu.SMEM),
                  pl.BlockSpec(memory_space=pl.ANY)],
        out_specs=pl.BlockSpec((MOE_BLOCK, D), lambda i: (i, 0)),
        out_shape=jax.ShapeDtypeStruct((N_ROWS, D), f32),
        scratch_shapes=[pltpu.SemaphoreType.DMA(())],
        compiler_params=_cparams("arbitrary"),
        name="moe_gather",
    )(buf_tok.reshape(N_BLOCKS, 1, MOE_BLOCK), xt)


def _moe_expert_kernel(be_ref, x_ref, bw_ref, wg_ref, wu_ref, wd_ref, y_ref, hid_ref):
    a = x_ref[...].astype(bf16)
    y_ref[...] = _swiglu_rows(a, wg_ref.at[0], wu_ref.at[0], wd_ref.at[0], hid_ref) * bw_ref[...]


def _moe_experts(blk_e, xb, buf_w, wg, wu, wd):
    grid_spec = pltpu.PrefetchScalarGridSpec(
        num_scalar_prefetch=1,
        grid=(N_BLOCKS,),
        in_specs=[pl.BlockSpec((MOE_BLOCK, D), lambda i, be: (i, 0)),
                  pl.BlockSpec((MOE_BLOCK, 1), lambda i, be: (i, 0)),
                  pl.BlockSpec((1, D, FFN), lambda i, be: (be[i], 0, 0)),
                  pl.BlockSpec((1, D, FFN), lambda i, be: (be[i], 0, 0)),
                  pl.BlockSpec((1, FFN, D), lambda i, be: (be[i], 0, 0))],
        out_specs=pl.BlockSpec((MOE_BLOCK, D), lambda i, be: (i, 0)),
        scratch_shapes=[pltpu.VMEM((MOE_BLOCK, FFN), bf16)],
    )
    return pl.pallas_call(
        _moe_expert_kernel,
        grid_spec=grid_spec,
        out_shape=jax.ShapeDtypeStruct((N_ROWS, D), f32),
        compiler_params=_cparams("arbitrary"),
        name="moe_experts",
    )(blk_e, xb, buf_w.reshape(N_ROWS, 1), wg, wu, wd)


def _moe_combine_kernel(p0_ref, p1_ref, h_ref, gate_ref, y_hbm, out_ref, y0_ref, y1_ref, sem):
    def copies(r):
        c0 = pltpu.make_async_copy(y_hbm.at[pl.ds(p0_ref[0, 0, r], 1), :], y0_ref.at[pl.ds(r, 1), :], sem.at[0])
        c1 = pltpu.make_async_copy(y_hbm.at[pl.ds(p1_ref[0, 0, r], 1), :], y1_ref.at[pl.ds(r, 1), :], sem.at[1])
        return c0, c1

    def issue(r, carry):
        c0, c1 = copies(r)
        c0.start()
        c1.start()
        return carry

    lax.fori_loop(0, CMB, issue, 0)

    def drain(r, carry):
        c0, c1 = copies(r)
        c0.wait()
        c1.wait()
        return carry

    lax.fori_loop(0, CMB, drain, 0)
    out_ref[0] = h_ref[0] + gate_ref[0] * (y0_ref[...] + y1_ref[...])


def _moe_combine(pos0, pos1, h, mod, yb):
    n_per_b = S // CMB
    smem = pl.BlockSpec((1, 1, CMB), lambda b, i: (b * n_per_b + i, 0, 0), memory_space=pltpu.SMEM)
    return pl.pallas_call(
        _moe_combine_kernel,
        grid=(B, n_per_b),
        in_specs=[smem, smem,
                  pl.BlockSpec((1, CMB, D), lambda b, i: (b, i, 0)),
                  pl.BlockSpec((1, 1, D), lambda b, i: (b * 6 + 5, 0, 0)),
                  pl.BlockSpec(memory_space=pl.ANY)],
        out_specs=pl.BlockSpec((1, CMB, D), lambda b, i: (b, i, 0)),
        out_shape=jax.ShapeDtypeStruct((B, S, D), f32),
        scratch_shapes=[pltpu.VMEM((CMB, D), f32), pltpu.VMEM((CMB, D), f32), pltpu.SemaphoreType.DMA((2,))],
        compiler_params=_cparams("arbitrary", "arbitrary"),
        name="moe_combine",
    )(pos0.reshape(N_TOK // CMB, 1, CMB), pos1.reshape(N_TOK // CMB, 1, CMB), h, mod, yb)


def _moe_routing(top_e):
    e = top_e.reshape(-1)
    onehot = (e[:, None] == jnp.arange(N_EXPERTS, dtype=jnp.int32)[None, :]).astype(jnp.int32)
    counts = jnp.sum(onehot, axis=0)
    rank = jnp.sum((jnp.cumsum(onehot, axis=0) - 1) * onehot, axis=1)
    padded = (counts + MOE_BLOCK - 1) // MOE_BLOCK * MOE_BLOCK
    pad_end = jnp.cumsum(padded)
    pad_start = pad_end - padded
    dest = (pad_start[e] + rank).astype(jnp.int32)
    tok = jnp.repeat(jnp.arange(N_TOK, dtype=jnp.int32), 2)
    buf_tok = jnp.full((N_ROWS,), N_TOK, jnp.int32).at[dest].set(tok)
    blk_start = jnp.arange(N_BLOCKS, dtype=jnp.int32) * MOE_BLOCK
    blk_e = jnp.minimum(jnp.sum((pad_end[None, :] <= blk_start[:, None]).astype(jnp.int32), axis=1),
                        N_EXPERTS - 1).astype(jnp.int32)
    return buf_tok, dest, blk_e


def _moe(h, mod, g2, router_w, wg, wu, wd):
    router_pad = jnp.pad(router_w, ((0, 0), (0, 128 - N_EXPERTS)))
    xt, e_pad, w_pad = _moe_prep(h, mod, g2, router_pad)
    top_e = e_pad[:N_TOK, :2]
    top_w = w_pad[:N_TOK, :2]
    buf_tok, dest, blk_e = _moe_routing(top_e)
    buf_w = jnp.zeros((N_ROWS,), f32).at[dest].set(top_w.reshape(-1))
    xb = _moe_gather(buf_tok, xt)
    yb = _moe_experts(blk_e, xb, buf_w, wg, wu, wd)
    return _moe_combine(dest[0::2], dest[1::2], h, mod, yb)


def _token_mixers(h, mod, i, with_ctx, tabs, norm1_g, w_in, fnet_w, ret_decay_fwd, ret_decay_bwd, ret_norm_g,
                  ret_w, attn_qn_g, attn_kn_g, attn_w, conv_dw_w, conv_dw_b, conv_ln_g, conv_ln_b, conv_w_out,
                  w_out):
    cos128, sin128, g_tab, m2, fc, cbd, sbd, bd = tabs
    n_tiles = NT if with_ctx else NL
    qg = jnp.tile(attn_qn_g[i].astype(f32), ATT_Q_HEADS)[None, :]
    kg = jnp.tile(attn_kn_g[i].astype(f32), ATT_KV_HEADS)[None, :]
    fn, rq, rk, rv, rg, aq, ak, avt, cv, gt = _proj_in(
        h, mod, norm1_g[i][None, :], w_in[i].astype(bf16), cos128, sin128, qg, kg, bd)

    o = _attention(aq, ak, avt, n_tiles)

    dmask, xi_f, xi_b, zeta_f, zeta_b, g_f, g_b = _ret_tables(ret_decay_fwd[i], ret_decay_bwd[i])
    sb = _ret_bwd_states(rk, rv, zeta_b, g_b)
    r = _retention(rq, rk, rv, rg, sb, dmask, xi_f, xi_b, zeta_f, g_f, ret_norm_g[i][None, :].astype(f32))

    cvo = _conv(cv, conv_dw_w[i], conv_dw_b[i][None, :], conv_ln_g[i][None, :], conv_ln_b[i][None, :], n_tiles)
    p = _fnet_dft(fn, g_tab, m2, fc, with_ctx)
    return _merge(h, mod, p, r, o, cvo, gt, cbd, sbd, fnet_w[i].astype(bf16), ret_w[i].astype(bf16),
                  attn_w[i].astype(bf16), conv_w_out[i].astype(bf16), w_out[i].astype(bf16), n_tiles)


def kernel(x, c, ctx, c_ctx, ada_w, ada_b, norm1_g, norm2_g, w_in, fnet_w, ret_decay_fwd, ret_decay_bwd, ret_norm_g, ret_w, attn_qn_g, attn_kn_g, attn_w, conv_dw_w, conv_dw_b, conv_ln_g, conv_ln_b, conv_w_out, w_out, ffn_w_gate, ffn_w_up, ffn_w_down, router_w, moe_w_gate, moe_w_up, moe_w_down):
    cos128, sin128 = _rope_tables()
    as_bf16 = lambda t: jnp.asarray(t, f32).astype(bf16)
    tabs = (cos128, sin128, as_bf16(_G_NP), as_bf16(_M2_NP), as_bf16(_FC_NP), as_bf16(_CBD_NP),
            as_bf16(_SBD_NP), as_bf16(_BD_NP))
    c_rows = jnp.concatenate([c, c_ctx[None, :], jnp.zeros((8 - B - 1, D), f32)], axis=0)
    mods = _ada(c_rows, ada_w, ada_b).reshape(DEPTH, 8 * 6, 1, D)
    h = jnp.concatenate([x, ctx], axis=1)
    for i in range(DEPTH):
        with_ctx = i < DEPTH - 1
        mod = mods[i]
        h = _token_mixers(h, mod, i, with_ctx, tabs, norm1_g, w_in, fnet_w, ret_decay_fwd, ret_decay_bwd,
                          ret_norm_g, ret_w, attn_qn_g, attn_kn_g, attn_w, conv_dw_w, conv_dw_b, conv_ln_g,
                          conv_ln_b, conv_w_out, w_out)
        j = i // 2
        g2 = norm2_g[i][None, :]
        if i % 2 == 0:
            h = _ffn(h, mod, g2, ffn_w_gate[j].astype(bf16), ffn_w_up[j].astype(bf16),
                     ffn_w_down[j].astype(bf16))
        else:
            h = _moe(h, mod, g2, router_w[j], moe_w_gate[j].astype(bf16), moe_w_up[j].astype(bf16),
                     moe_w_down[j].astype(bf16))
    return h
```

```python
import functools
import math

import numpy as np
import jax
import jax.numpy as jnp
from jax import lax
from jax.experimental import pallas as pl
from jax.experimental.pallas import tpu as pltpu

f32 = jnp.float32
bf16 = jnp.bfloat16

D = 1024
B = 2
S = 8192
C = 256
T = S + C
DEPTH = 2
GRID_W = 64
HD = 64
EPS = 1e-6
RET_HEADS = 4
RET_CHUNK = 128
N_CHUNK = T // RET_CHUNK
N_LAT_CHUNK = S // RET_CHUNK
ATT_Q_HEADS = 8
ATT_KV_HEADS = 2
ATT_GROUP = 4
CONV_K = 31
CONV_W = 256
HALO = 16
IN_COLS = 6656
FFN = 2816
FFN_CHUNK = 256
N_EXPERTS = 8
MOE_BLOCK = 128
ROPE_THETA = 10000.0

TM = 256
NT = T // TM
NL = S // TM
TK = 256
VROWS = HD + 16
LOG2E = math.log2(math.e)

N_TOK = B * S
N_ASSIGN = N_TOK * 2
N_BLOCKS = N_ASSIGN // MOE_BLOCK + N_EXPERTS
N_ROWS = N_BLOCKS * MOE_BLOCK
CMB = 256

FN1 = 64
FN2 = 128
FN_STEP = 8
FN_COLS = 4096

VMEM_LIMIT = 56 * 1024 * 1024


def _cparams(*sem):
    return pltpu.CompilerParams(dimension_semantics=sem, vmem_limit_bytes=VMEM_LIMIT)


def _const_spec(shape):
    nd = len(shape)
    return pl.BlockSpec(shape, lambda *_: (0,) * nd, pipeline_mode=pl.Buffered(1))


def _mod_spec(k):
    return pl.BlockSpec((1, 1, D), lambda b, i: (jnp.where(i == NL, 2, b) * 6 + k, 0, 0))


def _dot(a, b):
    return jnp.dot(a, b, preferred_element_type=f32)


def _silu(x):
    return x * jax.nn.sigmoid(x)


def _dft_tables():
    n1 = np.arange(FN1)[:, None, None]
    k2 = np.arange(FN2)[None, :, None]
    n2 = np.arange(FN2)[None, None, :]
    ph = 2.0 * np.pi * (((n1 + FN1 * n2) * k2) % S) / S
    g = np.concatenate([np.cos(ph), -np.sin(ph)], axis=1)
    k1 = np.arange(FN1)[:, None]
    m1 = np.arange(FN1)[None, :]
    ph = 2.0 * np.pi * ((k1 * m1) % FN1) / FN1
    c64, s64 = np.cos(ph), np.sin(ph)
    m2 = np.block([[c64, s64], [-s64, c64]])
    kc = np.arange(C)[:, None]
    nc = np.arange(C)[None, :]
    ph = 2.0 * np.pi * ((kc * nc) % C) / C
    fc = np.concatenate([np.cos(ph), -np.sin(ph)], axis=0)
    eye4 = np.eye(4)
    cbd = np.kron(eye4, c64)
    sbd = np.kron(eye4, s64)
    bd = np.kron(np.eye(8), np.ones((HD, HD)))
    return g, m2, fc, cbd, sbd, bd


_G_NP, _M2_NP, _FC_NP, _CBD_NP, _SBD_NP, _BD_NP = _dft_tables()


def _rope_tables():
    rows = S // GRID_W
    row = jnp.broadcast_to(jnp.arange(rows)[:, None], (rows, GRID_W)).reshape(-1).astype(f32)
    col = jnp.broadcast_to(jnp.arange(GRID_W)[None, :], (rows, GRID_W)).reshape(-1).astype(f32)
    n_axis = HD // 4
    inv = ROPE_THETA ** (-jnp.arange(n_axis, dtype=f32) / n_axis)
    ang = jnp.concatenate([row[:, None] * inv, col[:, None] * inv], axis=-1)
    cos, sin = jnp.cos(ang), jnp.sin(ang)
    cos = jnp.concatenate([cos, jnp.ones((C, HD // 2), f32)], axis=0)
    sin = jnp.concatenate([sin, jnp.zeros((C, HD // 2), f32)], axis=0)
    cos128 = jnp.concatenate([cos, cos, cos, cos], axis=-1)
    sin128 = jnp.concatenate([-sin, sin, -sin, sin], axis=-1)
    return cos128, sin128


def _ada_kernel(c_ref, w_ref, b_ref, o_ref):
    s = _silu(c_ref[...])
    o_ref[0] = jnp.dot(s, w_ref[0], preferred_element_type=f32,
                       precision=lax.Precision.HIGHEST) + b_ref[0]


def _ada(c_rows, ada_w, ada_b):
    tn = 1536
    return pl.pallas_call(
        _ada_kernel,
        grid=(DEPTH, 6 * D // tn),
        in_specs=[pl.BlockSpec((8, D), lambda l, j: (0, 0)),
                  pl.BlockSpec((1, D, tn), lambda l, j: (l, 0, j)),
                  pl.BlockSpec((1, 1, tn), lambda l, j: (l, 0, j))],
        out_specs=pl.BlockSpec((1, 8, tn), lambda l, j: (l, 0, j)),
        out_shape=jax.ShapeDtypeStruct((DEPTH, 8, 6 * D), f32),
        compiler_params=_cparams("parallel", "parallel"),
        name="ada",
    )(c_rows, ada_w, ada_b.reshape(DEPTH, 1, 6 * D))


def _norm_mod(x, g, shift, scale):
    ms = jnp.mean(x * x, axis=-1, keepdims=True)
    return (x * lax.rsqrt(ms + EPS) * g) * (1.0 + scale) + shift


def _swap_halves(x):
    w = x.shape[-1]
    lane = lax.broadcasted_iota(jnp.int32, x.shape, 1)
    fwd = pltpu.roll(x, w - HD // 2, 1)
    bwd = pltpu.roll(x, HD // 2, 1)
    return jnp.where((lane % HD) < HD // 2, fwd, bwd)


def _rope(x, cos, sin):
    return x * cos + _swap_halves(x) * sin


def _head_rms(x, gain, bd):
    x2 = x * x
    hi = x2.astype(bf16)
    lo = (x2 - hi.astype(f32)).astype(bf16)
    ss = _dot(hi, bd) + _dot(lo, bd)
    return x * lax.rsqrt(ss * (1.0 / HD) + EPS) * gain


def _proj_in_kernel(h_ref, sh_ref, sc_ref, g_ref, w_ref, cos_ref, sin_ref, qg_ref, kg_ref, bd_ref,
                    fn_ref, rq_ref, rk_ref, rv_ref, rg_ref, aq_ref, ak_ref, avt_ref, cv_ref, gt_ref):
    a = _norm_mod(h_ref[0], g_ref[...], sh_ref[0], sc_ref[0]).astype(bf16)

    def proj(c0, c1):
        return _dot(a, w_ref[:, c0:c1])

    cos = cos_ref[...]
    sin = sin_ref[...]
    cos2 = jnp.concatenate([cos, cos], axis=-1)
    sin2 = jnp.concatenate([sin, sin], axis=-1)
    cos4 = jnp.concatenate([cos2, cos2], axis=-1)
    sin4 = jnp.concatenate([sin2, sin2], axis=-1)

    fn_ref[0] = proj(0, 256).astype(bf16)
    rq_ref[0] = _rope(proj(256, 512), cos2, sin2).astype(bf16)
    rk_ref[0] = (_rope(proj(512, 768), cos2, sin2) * (HD ** -0.5)).astype(bf16)
    rv_ref[0] = proj(768, 1024).astype(bf16)
    rg_ref[0] = proj(1024, 1280).astype(bf16)

    q = _head_rms(proj(1280, 1792), qg_ref[...], bd_ref[...])
    qt = (_rope(q, cos4, sin4) * (HD ** -0.5 * LOG2E)).T
    for h in range(ATT_Q_HEADS):
        aq_ref[0, h] = qt[h * HD:(h + 1) * HD, :].astype(bf16)
    k = _head_rms(proj(1792, 1920), kg_ref[...], bd_ref[:2 * HD, :2 * HD])
    k = _rope(k, cos, sin).astype(bf16)
    vt = proj(1920, 2048).T
    row = lax.broadcasted_iota(jnp.int32, (VROWS - HD, TM), 0)
    ones_row = jnp.where(row == 0, 1.0, 0.0).astype(bf16)
    for g in range(ATT_KV_HEADS):
        ak_ref[0, g] = k[:, g * HD:(g + 1) * HD]
        avt_ref[0, g, 0, :HD] = vt[g * HD:(g + 1) * HD, :].astype(bf16)
        avt_ref[0, g, 0, HD:] = ones_row
    cv_ref[0] = proj(2048, 2560).astype(bf16)
    for j in range(4):
        gt_ref[0, :, j * D:(j + 1) * D] = proj(2560 + j * D, 2560 + (j + 1) * D).astype(bf16)


def _proj_in(h, mod, g1, w_in, cos128, sin128, qg, kg, bd):
    tile = lambda w: pl.BlockSpec((1, TM, w), lambda b, i: (b, i, 0))
    out_shapes = [
        jax.ShapeDtypeStruct((B, T, 256), bf16),
        jax.ShapeDtypeStruct((B, T, 256), bf16),
        jax.ShapeDtypeStruct((B, T, 256), bf16),
        jax.ShapeDtypeStruct((B, T, 256), bf16),
        jax.ShapeDtypeStruct((B, T, 256), bf16),
        jax.ShapeDtypeStruct((B, ATT_Q_HEADS, HD, T), bf16),
        jax.ShapeDtypeStruct((B, ATT_KV_HEADS, T, HD), bf16),
        jax.ShapeDtypeStruct((B, ATT_KV_HEADS, NT, VROWS, TM), bf16),
        jax.ShapeDtypeStruct((B, T, 512), bf16),
        jax.ShapeDtypeStruct((B, T, 4 * D), bf16),
    ]
    out_specs = [
        tile(256), tile(256), tile(256), tile(256), tile(256),
        pl.BlockSpec((1, ATT_Q_HEADS, HD, TM), lambda b, i: (b, 0, 0, i)),
        pl.BlockSpec((1, ATT_KV_HEADS, TM, HD), lambda b, i: (b, 0, i, 0)),
        pl.BlockSpec((1, ATT_KV_HEADS, 1, VROWS, TM), lambda b, i: (b, 0, i, 0, 0)),
        tile(512), tile(4 * D),
    ]
    return pl.pallas_call(
        _proj_in_kernel,
        grid=(B, NT),
        in_specs=[tile(D), _mod_spec(0), _mod_spec(1), _const_spec((1, D)), _const_spec((D, IN_COLS)),
                  pl.BlockSpec((TM, 128), lambda b, i: (i, 0)),
                  pl.BlockSpec((TM, 128), lambda b, i: (i, 0)),
                  _const_spec((1, 512)), _const_spec((1, 128)), _const_spec((512, 512))],
        out_specs=out_specs,
        out_shape=out_shapes,
        compiler_params=_cparams("parallel", "parallel"),
        name="proj_in",
    )(h, mod, mod, g1, w_in, cos128, sin128, qg, kg, bd)


def _attn_kernel(q_ref, k_ref, v_ref, o_ref, qt_ref, s_ref, p_ref):
    nq = ATT_GROUP * TM
    groups = range(ATT_KV_HEADS)
    is_ctx = pl.program_id(1) == NL
    j0 = jnp.where(is_ctx, NL, 0)
    n_pairs = jnp.where(is_ctx, 0, (NT - 1) // 2)
    for g in groups:
        for h in range(ATT_GROUP):
            qt_ref[g, :, h * TM:(h + 1) * TM] = q_ref[0, g * ATT_GROUP + h]

    def scores(t, slot):
        for g in groups:
            k = k_ref[0, g, pl.ds(pl.multiple_of(t * TK, TK), TK), :]
            s_ref[g, slot] = _dot(k, qt_ref[g])

    def weighted_values(t, slot, alphas, accs):
        return [alphas[g] * accs[g] + _dot(v_ref[0, g, t], p_ref[g, slot]) for g in groups]

    def softmax(slot, ms):
        new_ms, alphas = [], []
        for g in groups:
            s = s_ref[g, slot]
            m_new = jnp.maximum(ms[g], jnp.max(s, axis=0, keepdims=True))
            p_ref[g, slot] = jnp.exp2(s - m_new).astype(bf16)
            new_ms.append(m_new)
            alphas.append(jnp.exp2(ms[g] - m_new))
        return new_ms, alphas

    def pair(i, carry):
        ms, alphas, accs = carry
        t = j0 + 2 * i
        scores(t + 1, 1)
        accs = weighted_values(jnp.maximum(t - 1, j0), 1, alphas, accs)
        ms, alphas = softmax(0, ms)
        scores(t + 2, 0)
        accs = weighted_values(t, 0, alphas, accs)
        ms, alphas = softmax(1, ms)
        return ms, alphas, accs

    scores(j0, 0)
    for g in groups:
        p_ref[g, 1] = jnp.zeros((TK, nq), bf16)
    init = ([jnp.full((1, nq), -1e30, f32) for _ in groups], [jnp.ones((1, nq), f32) for _ in groups],
            [jnp.zeros((VROWS, nq), f32) for _ in groups])
    ms, alphas, accs = lax.fori_loop(0, n_pairs, pair, init)
    last = NT - 1
    accs = weighted_values(jnp.maximum(last - 1, j0), 1, alphas, accs)
    ms, alphas = softmax(0, ms)
    accs = weighted_values(last, 0, alphas, accs)
    outs = []
    for g in groups:
        ot = accs[g][:HD] / accs[g][HD:HD + 1]
        outs += [ot[:, h * TM:(h + 1) * TM].T for h in range(ATT_GROUP)]
    o_ref[0] = jnp.concatenate(outs, axis=-1).astype(bf16)


def _attention(aq, ak, avt, n_tiles):
    nq = ATT_GROUP * TM
    return pl.pallas_call(
        _attn_kernel,
        grid=(B, n_tiles),
        in_specs=[pl.BlockSpec((1, ATT_Q_HEADS, HD, TM), lambda b, i: (b, 0, 0, i)),
                  pl.BlockSpec((1, ATT_KV_HEADS, T, HD), lambda b, i: (b, 0, 0, 0)),
                  pl.BlockSpec((1, ATT_KV_HEADS, NT, VROWS, TK), lambda b, i: (b, 0, 0, 0, 0))],
        out_specs=pl.BlockSpec((1, TM, ATT_Q_HEADS * HD), lambda b, i: (b, i, 0)),
        out_shape=jax.ShapeDtypeStruct((B, n_tiles * TM, ATT_Q_HEADS * HD), bf16),
        scratch_shapes=[pltpu.VMEM((ATT_KV_HEADS, HD, nq), bf16),
                        pltpu.VMEM((ATT_KV_HEADS, 2, TK, nq), f32), pltpu.VMEM((ATT_KV_HEADS, 2, TK, nq), bf16)],
        compiler_params=_cparams("parallel", "arbitrary"),
        name="attention",
    )(aq, ak, avt)


def _ret_state_update(s_ref, g_ref, k, v, zeta):
    kzt = (k.astype(f32) * zeta).T
    for h in range(RET_HEADS):
        rows = slice(h * HD, (h + 1) * HD)
        upd = _dot(kzt[rows, :].astype(bf16), v[:, rows])
        s_ref[rows, :] = g_ref[rows, :] * s_ref[rows, :] + upd


def _ret_bwd_kernel(k_ref, v_ref, zeta_ref, g_ref, sb_ref, s_ref):
    @pl.when(pl.program_id(1) == 0)
    def _():
        s_ref[...] = jnp.zeros_like(s_ref)

    sb_ref[0, 0] = s_ref[...]
    _ret_state_update(s_ref, g_ref, k_ref[0], v_ref[0], zeta_ref[...])


def _ret_bwd_states(rk, rv, zeta_b, g_b):
    chunk = lambda b, t: (b, N_CHUNK - 1 - t, 0)
    return pl.pallas_call(
        _ret_bwd_kernel,
        grid=(B, N_CHUNK),
        in_specs=[pl.BlockSpec((1, RET_CHUNK, 256), chunk),
                  pl.BlockSpec((1, RET_CHUNK, 256), chunk),
                  _const_spec((RET_CHUNK, 256)), _const_spec((256, HD))],
        out_specs=pl.BlockSpec((1, 1, 256, HD), lambda b, t: (b, N_CHUNK - 1 - t, 0, 0)),
        out_shape=jax.ShapeDtypeStruct((B, N_CHUNK, 256, HD), f32),
        scratch_shapes=[pltpu.VMEM((256, HD), f32)],
        compiler_params=_cparams("parallel", "arbitrary"),
        name="ret_bwd_states",
    )(rk, rv, zeta_b, g_b)


def _ret_fwd_kernel(q_ref, k_ref, v_ref, gate_ref, sb_ref, dmask_ref, xif_ref, xib_ref, zeta_ref, g_ref,
                    ng_ref, o_ref, s_ref):
    @pl.when(pl.program_id(1) == 0)
    def _():
        s_ref[...] = jnp.zeros_like(s_ref)

    q = q_ref[0]
    k = k_ref[0]
    v = v_ref[0]
    qf = q.astype(f32)
    q_xf = (qf * xif_ref[...]).astype(bf16)
    q_xb = (qf * xib_ref[...]).astype(bf16)
    outs = []
    for h in range(RET_HEADS):
        cols = slice(h * HD, (h + 1) * HD)
        s = lax.dot_general(q[:, cols], k[:, cols], (((1,), (1,)), ((), ())), preferred_element_type=f32)
        att = (s * dmask_ref[h]).astype(bf16)
        o = (_dot(att, v[:, cols])
             + _dot(q_xf[:, cols], s_ref[cols, :].astype(bf16))
             + _dot(q_xb[:, cols], sb_ref[0, 0, cols, :].astype(bf16)))
        ms = jnp.mean(o * o, axis=-1, keepdims=True)
        outs.append(o * lax.rsqrt(ms + EPS))
    y = jnp.concatenate(outs, axis=-1) * ng_ref[...]
    o_ref[0] = (y * _silu(gate_ref[0].astype(f32))).astype(bf16)
    _ret_state_update(s_ref, g_ref, k, v, zeta_ref[...])


def _retention(rq, rk, rv, rg, sb, dmask, xi_f, xi_b, zeta_f, g_f, ng):
    chunk = lambda b, t: (b, (t + N_LAT_CHUNK) % N_CHUNK, 0)
    blk = pl.BlockSpec((1, RET_CHUNK, 256), chunk)
    return pl.pallas_call(
        _ret_fwd_kernel,
        grid=(B, N_CHUNK),
        in_specs=[blk, blk, blk, blk,
                  pl.BlockSpec((1, 1, 256, HD), lambda b, t: (b, (t + N_LAT_CHUNK) % N_CHUNK, 0, 0)),
                  _const_spec((RET_HEADS, RET_CHUNK, RET_CHUNK)),
                  _const_spec((RET_CHUNK, 256)), _const_spec((RET_CHUNK, 256)), _const_spec((RET_CHUNK, 256)),
                  _const_spec((256, HD)), _const_spec((1, 256))],
        out_specs=blk,
        out_shape=jax.ShapeDtypeStruct((B, T, 256), bf16),
        scratch_shapes=[pltpu.VMEM((256, HD), f32)],
        compiler_params=_cparams("parallel", "arbitrary"),
        name="retention",
    )(rq, rk, rv, rg, sb, dmask, xi_f, xi_b, zeta_f, g_f, ng)


def _ret_tables(dec_f, dec_b):
    lg_f = jax.nn.log_sigmoid(dec_f.astype(f32))
    lg_b = jax.nn.log_sigmoid(dec_b.astype(f32))
    pos = jnp.arange(RET_CHUNK, dtype=f32)
    diff = pos[:, None] - pos[None, :]
    d_f = jnp.where(diff[None] >= 0.0, jnp.exp(jnp.maximum(diff, 0.0)[None] * lg_f[:, None, None]), 0.0)
    d_b = jnp.where(diff[None] <= 0.0, jnp.exp(jnp.maximum(-diff, 0.0)[None] * lg_b[:, None, None]), 0.0)
    lanes = lambda t: jnp.repeat(t, HD, axis=1)
    xi_f = lanes(jnp.exp((pos[:, None] + 1.0) * lg_f[None, :]))
    xi_b = lanes(jnp.exp((RET_CHUNK - pos[:, None]) * lg_b[None, :]))
    zeta_f = lanes(jnp.exp((RET_CHUNK - 1.0 - pos[:, None]) * lg_f[None, :]))
    zeta_b = lanes(jnp.exp(pos[:, None] * lg_b[None, :]))
    rows = lambda t: jnp.broadcast_to(jnp.repeat(t, HD)[:, None], (RET_HEADS * HD, HD))
    g_f = rows(jnp.exp(RET_CHUNK * lg_f))
    g_b = rows(jnp.exp(RET_CHUNK * lg_b))
    return d_f + d_b, xi_f, xi_b, zeta_f, zeta_b, g_f, g_b


def _conv_kernel(prev_ref, cur_ref, next_ref, w_ref, b_ref, lg_ref, lb_ref, o_ref, win_ref):
    i = pl.program_id(1)

    def glu(u):
        u = u.astype(f32)
        return u[:, :CONV_W] * jax.nn.sigmoid(u[:, CONV_W:])

    has_prev = jnp.logical_and(i >= 1, i < NL).astype(f32)
    has_next = (i < NL - 1).astype(f32)
    win_ref[0:HALO, :] = glu(prev_ref[0, TM - HALO:TM, :]) * has_prev
    win_ref[HALO:HALO + TM, :] = glu(cur_ref[0])
    win_ref[HALO + TM:2 * HALO + TM, :] = glu(next_ref[0, 0:HALO, :]) * has_next
    acc = jnp.zeros((TM, CONV_W), f32)
    off = HALO - CONV_K // 2
    for k in range(CONV_K):
        acc = acc + w_ref[k:k + 1, :] * win_ref[off + k:off + k + TM, :]
    y = acc + b_ref[...]
    mu = jnp.mean(y, axis=-1, keepdims=True)
    var = jnp.mean(jnp.square(y - mu), axis=-1, keepdims=True)
    y = (y - mu) * lax.rsqrt(var + EPS) * lg_ref[...] + lb_ref[...]
    o_ref[0] = _silu(y).astype(bf16)


def _conv(cv, dw_w, dw_b, ln_g, ln_b, n_tiles):
    return pl.pallas_call(
        _conv_kernel,
        grid=(B, n_tiles),
        in_specs=[pl.BlockSpec((1, TM, 512), lambda b, i: (b, jnp.maximum(i - 1, 0), 0)),
                  pl.BlockSpec((1, TM, 512), lambda b, i: (b, i, 0)),
                  pl.BlockSpec((1, TM, 512), lambda b, i: (b, jnp.minimum(i + 1, NT - 1), 0)),
                  _const_spec((CONV_K, CONV_W)), _const_spec((1, CONV_W)),
                  _const_spec((1, CONV_W)), _const_spec((1, CONV_W))],
        out_specs=pl.BlockSpec((1, TM, CONV_W), lambda b, i: (b, i, 0)),
        out_shape=jax.ShapeDtypeStruct((B, n_tiles * TM, CONV_W), bf16),
        scratch_shapes=[pltpu.VMEM((TM + 2 * HALO, CONV_W), f32)],
        compiler_params=_cparams("parallel", "parallel"),
        name="conv",
    )(cv, cv, cv, dw_w, dw_b, ln_g, ln_b)


def _fn_stage1_kernel(x_ref, g_ref, a_ref):
    for s in range(FN_STEP):
        res = _dot(g_ref[s], x_ref[0, :, s * 256:(s + 1) * 256])
        a_ref[0, 0, s] = res[:FN2].astype(bf16)
        a_ref[0, 1, s] = res[FN2:].astype(bf16)


def _fn_stage2_kernel(a_ref, m_ref, p_ref):
    res = _dot(m_ref[...], a_ref[0])
    p_ref[0, 0, :FN1] = res[:FN1].astype(bf16)
    p_ref[0, 1, :FN1] = res[FN1:].astype(bf16)
    if p_ref.shape[2] > FN1:
        p_ref[0, :, FN1:] = jnp.zeros((2, p_ref.shape[2] - FN1, p_ref.shape[3]), bf16)


def _fn_ctx_kernel(u_ref, f_ref, _p_in, p_ref):
    res = _dot(f_ref[...], u_ref[0])
    p_ref[0, 0] = res[:C].astype(bf16)
    p_ref[0, 1] = res[C:].astype(bf16)


def _fnet_dft(fn, g_tab, m2, fc, with_ctx):
    p_rows = T if with_ctx else S
    x = fn.reshape(B, T // FN1, FN1 * 256)
    a = pl.pallas_call(
        _fn_stage1_kernel,
        grid=(B, FN1 // FN_STEP),
        in_specs=[pl.BlockSpec((1, FN2, FN_STEP * 256), lambda b, j: (b, 0, j)),
                  pl.BlockSpec((FN_STEP, 2 * FN2, FN2), lambda b, j: (j, 0, 0))],
        out_specs=pl.BlockSpec((1, 2, FN_STEP, FN2, 256), lambda b, j: (b, 0, j, 0, 0)),
        out_shape=jax.ShapeDtypeStruct((B, 2, FN1, FN2, 256), bf16),
        compiler_params=_cparams("parallel", "parallel"),
        name="fnet_stage1",
    )(x, g_tab)
    a = a.reshape(B, 2 * FN1, FN2 * 256)
    p = pl.pallas_call(
        _fn_stage2_kernel,
        grid=(B, FN2 * 256 // FN_COLS),
        in_specs=[pl.BlockSpec((1, 2 * FN1, FN_COLS), lambda b, j: (b, 0, j)),
                  _const_spec((2 * FN1, 2 * FN1))],
        out_specs=pl.BlockSpec((1, 2, p_rows // FN2, FN_COLS), lambda b, j: (b, 0, 0, j)),
        out_shape=jax.ShapeDtypeStruct((B, 2, p_rows // FN2, FN2 * 256), bf16),
        compiler_params=_cparams("parallel", "parallel"),
        name="fnet_stage2",
    )(a, m2)
    p = p.reshape(B, 2, p_rows, 256)
    if with_ctx:
        p = pl.pallas_call(
            _fn_ctx_kernel,
            grid=(B,),
            in_specs=[pl.BlockSpec((1, C, 256), lambda b: (b, NL, 0)),
                      _const_spec((2 * C, C)),
                      pl.BlockSpec(memory_space=pl.ANY)],
            out_specs=pl.BlockSpec((1, 2, C, 256), lambda b: (b, 0, NL, 0)),
            out_shape=jax.ShapeDtypeStruct((B, 2, T, 256), bf16),
            input_output_aliases={2: 0},
            compiler_params=_cparams("parallel"),
            name="fnet_ctx",
        )(fn, fc, p)
    return p


def _merge_kernel(h_ref, gate_ref, p_ref, r_ref, o_ref, cv_ref, gt_ref, cbd_ref, sbd_ref,
                  fw_ref, rw_ref, aw_ref, cw_ref, ow_ref, out_ref):
    i = pl.program_id(1)
    scale = jnp.where(i == NL, (HD * C) ** -0.5, (HD * S) ** -0.5)
    yfn = (_dot(p_ref[0, 0], cbd_ref[...]) + _dot(p_ref[0, 1], sbd_ref[...])) * scale
    branches = (
        (yfn.astype(bf16), fw_ref),
        (r_ref[0], rw_ref),
        (o_ref[0], aw_ref),
        (cv_ref[0], cw_ref),
    )
    merged = jnp.zeros((TM, D), f32)
    for j, (xin, w_ref) in enumerate(branches):
        g = jax.nn.sigmoid(gt_ref[0, :, j * D:(j + 1) * D].astype(f32))
        merged = merged + g * _dot(xin, w_ref[...])
    out_ref[0] = h_ref[0] + gate_ref[0] * _dot(merged.astype(bf16), ow_ref[...])


def _merge(h, mod, p, r, o, cvo, gt, cbd, sbd, fw, rw, aw, cw, ow, n_tiles):
    tile = lambda w: pl.BlockSpec((1, TM, w), lambda b, i: (b, i, 0))
    return pl.pallas_call(
        _merge_kernel,
        grid=(B, n_tiles),
        in_specs=[tile(D), _mod_spec(2),
                  pl.BlockSpec((1, 2, TM, 256), lambda b, i: (b, 0, i, 0)),
                  tile(256), tile(512), tile(256), tile(4 * D),
                  _const_spec((256, 256)), _const_spec((256, 256)),
                  _const_spec((256, D)), _const_spec((256, D)), _const_spec((512, D)),
                  _const_spec((256, D)), _const_spec((D, D))],
        out_specs=tile(D),
        out_shape=jax.ShapeDtypeStruct((B, n_tiles * TM, D), f32),
        compiler_params=_cparams("parallel", "parallel"),
        name="merge",
    )(h, mod, p, r, o, cvo, gt, cbd, sbd, fw, rw, aw, cw, ow)


def _swiglu_rows(a, wg_ref, wu_ref, wd_ref, h_ref):
    for c in range(FFN // FFN_CHUNK):
        cols = slice(c * FFN_CHUNK, (c + 1) * FFN_CHUNK)
        h_ref[:, cols] = (_silu(_dot(a, wg_ref[:, cols])) * _dot(a, wu_ref[:, cols])).astype(bf16)
    return _dot(h_ref[...], wd_ref[...])


def _ffn_kernel(h_ref, sh_ref, sc_ref, gate_ref, g_ref, wg_ref, wu_ref, wd_ref, out_ref, hid_ref):
    x = h_ref[0]
    a = _norm_mod(x, g_ref[...], sh_ref[0], sc_ref[0]).astype(bf16)
    out_ref[0] = x + gate_ref[0] * _swiglu_rows(a, wg_ref, wu_ref, wd_ref, hid_ref)


def _ffn(h, mod, g2, wg, wu, wd):
    tile = pl.BlockSpec((1, TM, D), lambda b, i: (b, i, 0))
    return pl.pallas_call(
        _ffn_kernel,
        grid=(B, NT),
        in_specs=[tile, _mod_spec(3), _mod_spec(4), _mod_spec(5), _const_spec((1, D)),
                  _const_spec((D, FFN)), _const_spec((D, FFN)), _const_spec((FFN, D))],
        out_specs=tile,
        out_shape=jax.ShapeDtypeStruct((B, T, D), f32),
        scratch_shapes=[pltpu.VMEM((TM, FFN), bf16)],
        compiler_params=_cparams("parallel", "parallel"),
        name="ffn",
    )(h, mod, mod, mod, g2, wg, wu, wd)


def _moe_prep_kernel(h_ref, sh_ref, sc_ref, g_ref, rw_ref, x_ref, e_ref, w_ref):
    f = _norm_mod(h_ref[0], g_ref[...], sh_ref[0], sc_ref[0])
    x_ref[...] = f
    logits = jnp.dot(f, rw_ref[...], preferred_element_type=f32, precision=lax.Precision.HIGHEST)
    lane = lax.broadcasted_iota(jnp.int32, logits.shape, 1)
    neg = jnp.float32(-jnp.inf)
    logits = jnp.where(lane < N_EXPERTS, logits, neg)
    m1 = jnp.max(logits, axis=-1, keepdims=True)
    i1 = jnp.min(jnp.where(logits == m1, lane, 128), axis=-1, keepdims=True)
    rest = jnp.where(lane == i1, neg, logits)
    m2 = jnp.max(rest, axis=-1, keepdims=True)
    i2 = jnp.min(jnp.where(rest == m2, lane, 128), axis=-1, keepdims=True)
    z = jnp.exp(m2 - m1)
    w1 = 1.0 / (1.0 + z)
    w2 = z / (1.0 + z)
    e_ref[...] = jnp.where(lane == 0, i1, jnp.where(lane == 1, i2, 0))
    w_ref[...] = jnp.where(lane == 0, w1, jnp.where(lane == 1, w2, 0.0))


def _moe_prep(h, mod, g2, router_pad):
    n_steps = B * NL

    def mod_spec(k):
        return pl.BlockSpec((1, 1, D), lambda t: ((t // NL) * 6 + k, 0, 0))

    return pl.pallas_call(
        _moe_prep_kernel,
        grid=(n_steps,),
        in_specs=[pl.BlockSpec((1, TM, D), lambda t: (t // NL, t % NL, 0)),
                  mod_spec(3), mod_spec(4), _const_spec((1, D)), _const_spec((D, 128))],
        out_specs=[pl.BlockSpec((TM, D), lambda t: (t, 0)),
                   pl.BlockSpec((TM, 128), lambda t: (t, 0)),
                   pl.BlockSpec((TM, 128), lambda t: (t, 0))],
        out_shape=[jax.ShapeDtypeStruct((N_TOK, D), f32),
                   jax.ShapeDtypeStruct((N_TOK, 128), jnp.int32),
                   jax.ShapeDtypeStruct((N_TOK, 128), f32)],
        compiler_params=_cparams("parallel"),
        name="moe_prep",
    )(h, mod, mod, g2, router_pad)


def _moe_scatter_kernel(nvalid_ref, dest_ref, x_ref, xb_hbm, zero_ref, sem):
    def zero_copy(slot):
        return pltpu.make_async_copy(zero_ref.at[pl.ds(0, 1), :], xb_hbm.at[pl.ds(slot, 1), :], sem.at[2])

    @pl.when(pl.program_id(0) == 0)
    def _():
        zero_ref[...] = jnp.zeros_like(zero_ref)

        def pad_rows(start):
            def each_block(blk, carry):
                def each_row(r, c):
                    copy = zero_copy(blk * MOE_BLOCK + r)
                    copy.start() if start else copy.wait()
                    return c

                return lax.fori_loop(nvalid_ref[blk], MOE_BLOCK, each_row, carry)

            lax.fori_loop(0, N_BLOCKS, each_block, 0)

        pad_rows(True)
        pad_rows(False)

    def issue(r, carry):
        src = x_ref.at[pl.ds(r, 1), :]
        pltpu.make_async_copy(src, xb_hbm.at[pl.ds(dest_ref[0, 0, 2 * r], 1), :], sem.at[0]).start()
        pltpu.make_async_copy(src, xb_hbm.at[pl.ds(dest_ref[0, 0, 2 * r + 1], 1), :], sem.at[1]).start()
        return carry

    lax.fori_loop(0, TM, issue, 0)
    for k in range(2):
        pltpu.make_async_copy(x_ref, xb_hbm.at[pl.ds(0, TM), :], sem.at[k]).wait()


def _moe_scatter(nvalid, dest, xt):
    grid_spec = pltpu.PrefetchScalarGridSpec(
        num_scalar_prefetch=1,
        grid=(N_TOK // TM,),
        in_specs=[pl.BlockSpec((1, 1, 2 * TM), lambda i, nv: (i, 0, 0), memory_space=pltpu.SMEM),
                  pl.BlockSpec((TM, D), lambda i, nv: (i, 0))],
        out_specs=pl.BlockSpec(memory_space=pl.ANY),
        scratch_shapes=[pltpu.VMEM((8, D), f32), pltpu.SemaphoreType.DMA((3,))],
    )
    return pl.pallas_call(
        _moe_scatter_kernel,
        grid_spec=grid_spec,
        out_shape=jax.ShapeDtypeStruct((N_ROWS, D), f32),
        compiler_params=_cparams("arbitrary"),
        name="moe_scatter",
    )(nvalid, dest.reshape(N_TOK // TM, 1, 2 * TM), xt)


def _moe_expert_kernel(be_ref, x_ref, wg_ref, wu_ref, wd_ref, y_ref, hid_ref):
    a = x_ref[...].astype(bf16)
    y_ref[...] = _swiglu_rows(a, wg_ref.at[0], wu_ref.at[0], wd_ref.at[0], hid_ref)


def _moe_experts(blk_e, xb, wg, wu, wd):
    grid_spec = pltpu.PrefetchScalarGridSpec(
        num_scalar_prefetch=1,
        grid=(N_BLOCKS,),
        in_specs=[pl.BlockSpec((MOE_BLOCK, D), lambda i, be: (i, 0)),
                  pl.BlockSpec((1, D, FFN), lambda i, be: (be[i], 0, 0)),
                  pl.BlockSpec((1, D, FFN), lambda i, be: (be[i], 0, 0)),
                  pl.BlockSpec((1, FFN, D), lambda i, be: (be[i], 0, 0))],
        out_specs=pl.BlockSpec((MOE_BLOCK, D), lambda i, be: (i, 0)),
        scratch_shapes=[pltpu.VMEM((MOE_BLOCK, FFN), bf16)],
    )
    return pl.pallas_call(
        _moe_expert_kernel,
        grid_spec=grid_spec,
        out_shape=jax.ShapeDtypeStruct((N_ROWS, D), f32),
        compiler_params=_cparams("arbitrary"),
        name="moe_experts",
    )(blk_e, xb, wg, wu, wd)


def _moe_combine_kernel(dest_ref, h_ref, gate_ref, w_ref, y_hbm, out_ref, y0_ref, y1_ref, sem):
    def issue(r, carry):
        pltpu.make_async_copy(y_hbm.at[pl.ds(dest_ref[0, 0, 2 * r], 1), :], y0_ref.at[pl.ds(r, 1), :],
                              sem.at[0]).start()
        pltpu.make_async_copy(y_hbm.at[pl.ds(dest_ref[0, 0, 2 * r + 1], 1), :], y1_ref.at[pl.ds(r, 1), :],
                              sem.at[1]).start()
        return carry

    lax.fori_loop(0, CMB, issue, 0)
    pltpu.make_async_copy(y_hbm.at[pl.ds(0, CMB), :], y0_ref, sem.at[0]).wait()
    pltpu.make_async_copy(y_hbm.at[pl.ds(0, CMB), :], y1_ref, sem.at[1]).wait()
    w = w_ref[...]
    y = w[:, 0:1] * y0_ref[...] + w[:, 1:2] * y1_ref[...]
    out_ref[0] = h_ref[0] + gate_ref[0] * y


def _moe_combine(dest, h, mod, w_pad, yb):
    n_per_b = S // CMB
    return pl.pallas_call(
        _moe_combine_kernel,
        grid=(B, n_per_b),
        in_specs=[pl.BlockSpec((1, 1, 2 * CMB), lambda b, i: (b * n_per_b + i, 0, 0), memory_space=pltpu.SMEM),
                  pl.BlockSpec((1, CMB, D), lambda b, i: (b, i, 0)),
                  pl.BlockSpec((1, 1, D), lambda b, i: (b * 6 + 5, 0, 0)),
                  pl.BlockSpec((CMB, 128), lambda b, i: (b * n_per_b + i, 0)),
                  pl.BlockSpec(memory_space=pl.ANY)],
        out_specs=pl.BlockSpec((1, CMB, D), lambda b, i: (b, i, 0)),
        out_shape=jax.ShapeDtypeStruct((B, S, D), f32),
        scratch_shapes=[pltpu.VMEM((CMB, D), f32), pltpu.VMEM((CMB, D), f32), pltpu.SemaphoreType.DMA((2,))],
        compiler_params=_cparams("arbitrary", "arbitrary"),
        name="moe_combine",
    )(dest.reshape(N_TOK // CMB, 1, 2 * CMB), h, mod, w_pad, yb)


def _moe_routing(top_e):
    e = top_e.reshape(-1)
    onehot = (e[:, None] == jnp.arange(N_EXPERTS, dtype=jnp.int32)[None, :]).astype(jnp.int32)
    counts = jnp.sum(onehot, axis=0)
    rank = jnp.sum((jnp.cumsum(onehot, axis=0) - 1) * onehot, axis=1)
    padded = (counts + MOE_BLOCK - 1) // MOE_BLOCK * MOE_BLOCK
    pad_end = jnp.cumsum(padded)
    pad_start = pad_end - padded
    dest = (pad_start[e] + rank).astype(jnp.int32)
    blk_start = jnp.arange(N_BLOCKS, dtype=jnp.int32) * MOE_BLOCK
    blk_e = jnp.minimum(jnp.sum((pad_end[None, :] <= blk_start[:, None]).astype(jnp.int32), axis=1),
                        N_EXPERTS - 1).astype(jnp.int32)
    nvalid = jnp.clip(pad_start[blk_e] + counts[blk_e] - blk_start, 0, MOE_BLOCK).astype(jnp.int32)
    nvalid = jnp.where(blk_start < pad_end[N_EXPERTS - 1], nvalid, 0)
    return dest, blk_e, nvalid


def _moe(h, mod, g2, router_w, wg, wu, wd):
    router_pad = jnp.pad(router_w, ((0, 0), (0, 128 - N_EXPERTS)))
    xt, e_pad, w_pad = _moe_prep(h, mod, g2, router_pad)
    dest, blk_e, nvalid = _moe_routing(e_pad[:, :2])
    xb = _moe_scatter(nvalid, dest, xt)
    yb = _moe_experts(blk_e, xb, wg, wu, wd)
    return _moe_combine(dest, h, mod, w_pad, yb)


def _token_mixers(h, mod, i, with_ctx, tabs, norm1_g, w_in, fnet_w, ret_decay_fwd, ret_decay_bwd, ret_norm_g,
                  ret_w, attn_qn_g, attn_kn_g, attn_w, conv_dw_w, conv_dw_b, conv_ln_g, conv_ln_b, conv_w_out,
                  w_out):
    cos128, sin128, g_tab, m2, fc, cbd, sbd, bd = tabs
    n_tiles = NT if with_ctx else NL
    qg = jnp.tile(attn_qn_g[i].astype(f32), ATT_Q_HEADS)[None, :]
    kg = jnp.tile(attn_kn_g[i].astype(f32), ATT_KV_HEADS)[None, :]
    fn, rq, rk, rv, rg, aq, ak, avt, cv, gt = _proj_in(
        h, mod, norm1_g[i][None, :], w_in[i].astype(bf16), cos128, sin128, qg, kg, bd)

    o = _attention(aq, ak, avt, n_tiles)

    dmask, xi_f, xi_b, zeta_f, zeta_b, g_f, g_b = _ret_tables(ret_decay_fwd[i], ret_decay_bwd[i])
    sb = _ret_bwd_states(rk, rv, zeta_b, g_b)
    r = _retention(rq, rk, rv, rg, sb, dmask, xi_f, xi_b, zeta_f, g_f, ret_norm_g[i][None, :].astype(f32))

    cvo = _conv(cv, conv_dw_w[i], conv_dw_b[i][None, :], conv_ln_g[i][None, :], conv_ln_b[i][None, :], n_tiles)
    p = _fnet_dft(fn, g_tab, m2, fc, with_ctx)
    return _merge(h, mod, p, r, o, cvo, gt, cbd, sbd, fnet_w[i].astype(bf16), ret_w[i].astype(bf16),
                  attn_w[i].astype(bf16), conv_w_out[i].astype(bf16), w_out[i].astype(bf16), n_tiles)


def kernel(x, c, ctx, c_ctx, ada_w, ada_b, norm1_g, norm2_g, w_in, fnet_w, ret_decay_fwd, ret_decay_bwd, ret_norm_g, ret_w, attn_qn_g, attn_kn_g, attn_w, conv_dw_w, conv_dw_b, conv_ln_g, conv_ln_b, conv_w_out, w_out, ffn_w_gate, ffn_w_up, ffn_w_down, router_w, moe_w_gate, moe_w_up, moe_w_down):
    cos128, sin128 = _rope_tables()
    as_bf16 = lambda t: jnp.asarray(t, f32).astype(bf16)
    tabs = (cos128, sin128, as_bf16(_G_NP), as_bf16(_M2_NP), as_bf16(_FC_NP), as_bf16(_CBD_NP),
            as_bf16(_SBD_NP), as_bf16(_BD_NP))
    c_rows = jnp.concatenate([c, c_ctx[None, :], jnp.zeros((8 - B - 1, D), f32)], axis=0)
    mods = _ada(c_rows, ada_w, ada_b).reshape(DEPTH, 8 * 6, 1, D)
    h = jnp.concatenate([x, ctx], axis=1)
    for i in range(DEPTH):
        with_ctx = i < DEPTH - 1
        mod = mods[i]
        h = _token_mixers(h, mod, i, with_ctx, tabs, norm1_g, w_in, fnet_w, ret_decay_fwd, ret_decay_bwd,
                          ret_norm_g, ret_w, attn_qn_g, attn_kn_g, attn_w, conv_dw_w, conv_dw_b, conv_ln_g,
                          conv_ln_b, conv_w_out, w_out)
        j = i // 2
        g2 = norm2_g[i][None, :]
        if i % 2 == 0:
            h = _ffn(h, mod, g2, ffn_w_gate[j].astype(bf16), ffn_w_up[j].astype(bf16),
                     ffn_w_down[j].astype(bf16))
        else:
            h = _moe(h, mod, g2, router_w[j], moe_w_gate[j].astype(bf16), moe_w_up[j].astype(bf16),
                     moe_w_down[j].astype(bf16))
    return h
```

```python
import functools
import math

import numpy as np
import jax
import jax.numpy as jnp
from jax import lax
from jax.experimental import pallas as pl
from jax.experimental.pallas import tpu as pltpu

f32 = jnp.float32
bf16 = jnp.bfloat16

D = 1024
B = 2
S = 8192
C = 256
T = S + C
DEPTH = 2
GRID_W = 64
HD = 64
EPS = 1e-6
RET_HEADS = 4
RET_CHUNK = 128
N_CHUNK = T // RET_CHUNK
ATT_Q_HEADS = 8
ATT_KV_HEADS = 2
ATT_GROUP = 4
CONV_K = 31
CONV_W = 256
HALO = 16
IN_COLS = 6656
FFN = 2816
FFN_CHUNK = 256
N_EXPERTS = 8
MOE_BLOCK = 128
ROPE_THETA = 10000.0

TM = 256
NT = T // TM
NL = S // TM
RET_STEP = TM // RET_CHUNK
TK = 256
VROWS = HD + 16
LOG2E = math.log2(math.e)

N_TOK = B * S
N_ASSIGN = N_TOK * 2
N_BLOCKS = N_ASSIGN // MOE_BLOCK + N_EXPERTS
N_ROWS = N_BLOCKS * MOE_BLOCK
CMB = 256
DMA_UNROLL = 8

FN1 = 64
FN2 = 128
FN_STEP = 8
FN_COLS = 4096

VMEM_LIMIT = 56 * 1024 * 1024


def _cparams(*sem):
    return pltpu.CompilerParams(dimension_semantics=sem, vmem_limit_bytes=VMEM_LIMIT)


def _const_spec(shape):
    nd = len(shape)
    return pl.BlockSpec(shape, lambda *_: (0,) * nd, pipeline_mode=pl.Buffered(1))


def _mod_spec(k):
    return pl.BlockSpec((1, 1, D), lambda b, i: (jnp.where(i == NL, 2, b) * 6 + k, 0, 0))


def _dot(a, b):
    return jnp.dot(a, b, preferred_element_type=f32)


def _silu(x):
    return x * jax.nn.sigmoid(x)


def _dft_tables():
    n1 = np.arange(FN1)[:, None, None]
    k2 = np.arange(FN2)[None, :, None]
    n2 = np.arange(FN2)[None, None, :]
    ph = 2.0 * np.pi * (((n1 + FN1 * n2) * k2) % S) / S
    g = np.concatenate([np.cos(ph), -np.sin(ph)], axis=1)
    k1 = np.arange(FN1)[:, None]
    m1 = np.arange(FN1)[None, :]
    ph = 2.0 * np.pi * ((k1 * m1) % FN1) / FN1
    c64, s64 = np.cos(ph), np.sin(ph)
    m2 = np.block([[c64, s64], [-s64, c64]])
    kc = np.arange(C)[:, None]
    nc = np.arange(C)[None, :]
    ph = 2.0 * np.pi * ((kc * nc) % C) / C
    fc = np.concatenate([np.cos(ph), -np.sin(ph)], axis=0)
    eye4 = np.eye(4)
    cbd = np.kron(eye4, c64)
    sbd = np.kron(eye4, s64)
    bd = np.kron(np.eye(8), np.ones((HD, HD)))
    return g, m2, fc, cbd, sbd, bd


_G_NP, _M2_NP, _FC_NP, _CBD_NP, _SBD_NP, _BD_NP = _dft_tables()


def _rope_tables():
    rows = S // GRID_W
    row = jnp.broadcast_to(jnp.arange(rows)[:, None], (rows, GRID_W)).reshape(-1).astype(f32)
    col = jnp.broadcast_to(jnp.arange(GRID_W)[None, :], (rows, GRID_W)).reshape(-1).astype(f32)
    n_axis = HD // 4
    inv = ROPE_THETA ** (-jnp.arange(n_axis, dtype=f32) / n_axis)
    ang = jnp.concatenate([row[:, None] * inv, col[:, None] * inv], axis=-1)
    cos, sin = jnp.cos(ang), jnp.sin(ang)
    cos = jnp.concatenate([cos, jnp.ones((C, HD // 2), f32)], axis=0)
    sin = jnp.concatenate([sin, jnp.zeros((C, HD // 2), f32)], axis=0)
    cos128 = jnp.concatenate([cos, cos, cos, cos], axis=-1)
    sin128 = jnp.concatenate([-sin, sin, -sin, sin], axis=-1)
    return cos128, sin128


def _ada_kernel(c_ref, w_ref, b_ref, o_ref):
    s = _silu(c_ref[...])
    o_ref[0] = jnp.dot(s, w_ref[0], preferred_element_type=f32,
                       precision=lax.Precision.HIGHEST) + b_ref[0]


def _ada(c_rows, ada_w, ada_b):
    tn = 1536
    return pl.pallas_call(
        _ada_kernel,
        grid=(DEPTH, 6 * D // tn),
        in_specs=[pl.BlockSpec((8, D), lambda l, j: (0, 0)),
                  pl.BlockSpec((1, D, tn), lambda l, j: (l, 0, j)),
                  pl.BlockSpec((1, 1, tn), lambda l, j: (l, 0, j))],
        out_specs=pl.BlockSpec((1, 8, tn), lambda l, j: (l, 0, j)),
        out_shape=jax.ShapeDtypeStruct((DEPTH, 8, 6 * D), f32),
        compiler_params=_cparams("parallel", "parallel"),
        name="ada",
    )(c_rows, ada_w, ada_b.reshape(DEPTH, 1, 6 * D))


def _norm_mod(x, g, shift, scale):
    ms = jnp.mean(x * x, axis=-1, keepdims=True)
    return (x * lax.rsqrt(ms + EPS) * g) * (1.0 + scale) + shift


def _swap_halves(x):
    w = x.shape[-1]
    lane = lax.broadcasted_iota(jnp.int32, x.shape, 1)
    fwd = pltpu.roll(x, w - HD // 2, 1)
    bwd = pltpu.roll(x, HD // 2, 1)
    return jnp.where((lane % HD) < HD // 2, fwd, bwd)


def _rope(x, cos, sin):
    return x * cos + _swap_halves(x) * sin


def _head_rms(x, gain, bd):
    x2 = x * x
    hi = x2.astype(bf16)
    lo = (x2 - hi.astype(f32)).astype(bf16)
    ss = _dot(hi, bd) + _dot(lo, bd)
    return x * lax.rsqrt(ss * (1.0 / HD) + EPS) * gain


def _stream_specs(ctx_tile):
    return [pl.BlockSpec((1, TM, D), lambda b, i: (b, jnp.minimum(i, NL - 1), 0)),
            pl.BlockSpec((1, TM, D), lambda b, i: (b, ctx_tile, 0))]


def _stream_tile(lat_ref, ctx_ref):
    return jnp.where(pl.program_id(1) == NL, ctx_ref[0], lat_ref[0])


def _proj_in_kernel(lat_ref, ctx_ref, sh_ref, sc_ref, g_ref, w_ref, cos_ref, sin_ref, qg_ref, kg_ref, bd_ref,
                    fn_ref, rq_ref, rk_ref, rv_ref, rg_ref, aq_ref, ak_ref, avt_ref, cv_ref, gt_ref):
    a = _norm_mod(_stream_tile(lat_ref, ctx_ref), g_ref[...], sh_ref[0], sc_ref[0]).astype(bf16)

    def proj(c0, c1):
        return _dot(a, w_ref[:, c0:c1])

    cos = cos_ref[...]
    sin = sin_ref[...]
    cos2 = jnp.concatenate([cos, cos], axis=-1)
    sin2 = jnp.concatenate([sin, sin], axis=-1)
    cos4 = jnp.concatenate([cos2, cos2], axis=-1)
    sin4 = jnp.concatenate([sin2, sin2], axis=-1)

    fn_ref[0] = proj(0, 256).astype(bf16)
    rq_ref[0] = _rope(proj(256, 512), cos2, sin2).astype(bf16)
    rk_ref[0] = (_rope(proj(512, 768), cos2, sin2) * (HD ** -0.5)).astype(bf16)
    rv_ref[0] = proj(768, 1024).astype(bf16)
    rg_ref[0] = proj(1024, 1280).astype(bf16)

    q = _head_rms(proj(1280, 1792), qg_ref[...], bd_ref[...])
    qt = (_rope(q, cos4, sin4) * (HD ** -0.5 * LOG2E)).T
    for h in range(ATT_Q_HEADS):
        aq_ref[0, h] = qt[h * HD:(h + 1) * HD, :].astype(bf16)
    k = _head_rms(proj(1792, 1920), kg_ref[...], bd_ref[:2 * HD, :2 * HD])
    k = _rope(k, cos, sin).astype(bf16)
    vt = proj(1920, 2048).T
    row = lax.broadcasted_iota(jnp.int32, (VROWS - HD, TM), 0)
    ones_row = jnp.where(row == 0, 1.0, 0.0).astype(bf16)
    for g in range(ATT_KV_HEADS):
        ak_ref[0, g] = k[:, g * HD:(g + 1) * HD]
        avt_ref[0, g, 0, :HD] = vt[g * HD:(g + 1) * HD, :].astype(bf16)
        avt_ref[0, g, 0, HD:] = ones_row
    cv_ref[0] = proj(2048, 2560).astype(bf16)
    for j in range(4):
        gt_ref[0, :, j * D:(j + 1) * D] = proj(2560 + j * D, 2560 + (j + 1) * D).astype(bf16)


def _proj_in(h_lat, h_ctx, ctx_tile, mod, g1, w_in, cos128, sin128, qg, kg, bd):
    tile = lambda w: pl.BlockSpec((1, TM, w), lambda b, i: (b, i, 0))
    out_shapes = [
        jax.ShapeDtypeStruct((B, T, 256), bf16),
        jax.ShapeDtypeStruct((B, T, 256), bf16),
        jax.ShapeDtypeStruct((B, T, 256), bf16),
        jax.ShapeDtypeStruct((B, T, 256), bf16),
        jax.ShapeDtypeStruct((B, T, 256), bf16),
        jax.ShapeDtypeStruct((B, ATT_Q_HEADS, HD, T), bf16),
        jax.ShapeDtypeStruct((B, ATT_KV_HEADS, T, HD), bf16),
        jax.ShapeDtypeStruct((B, ATT_KV_HEADS, NT, VROWS, TM), bf16),
        jax.ShapeDtypeStruct((B, T, 512), bf16),
        jax.ShapeDtypeStruct((B, T, 4 * D), bf16),
    ]
    out_specs = [
        tile(256), tile(256), tile(256), tile(256), tile(256),
        pl.BlockSpec((1, ATT_Q_HEADS, HD, TM), lambda b, i: (b, 0, 0, i)),
        pl.BlockSpec((1, ATT_KV_HEADS, TM, HD), lambda b, i: (b, 0, i, 0)),
        pl.BlockSpec((1, ATT_KV_HEADS, 1, VROWS, TM), lambda b, i: (b, 0, i, 0, 0)),
        tile(512), tile(4 * D),
    ]
    return pl.pallas_call(
        _proj_in_kernel,
        grid=(B, NT),
        in_specs=_stream_specs(ctx_tile) + [
            _mod_spec(0), _mod_spec(1), _const_spec((1, D)), _const_spec((D, IN_COLS)),
            pl.BlockSpec((TM, 128), lambda b, i: (i, 0)),
            pl.BlockSpec((TM, 128), lambda b, i: (i, 0)),
            _const_spec((1, 512)), _const_spec((1, 128)), _const_spec((512, 512))],
        out_specs=out_specs,
        out_shape=out_shapes,
        compiler_params=_cparams("parallel", "parallel"),
        name="proj_in",
    )(h_lat, h_ctx, mod, mod, g1, w_in, cos128, sin128, qg, kg, bd)


def _attn_kernel(q_ref, k_ref, v_ref, o_ref, qt_ref, s_ref, p_ref):
    nq = ATT_GROUP * TM
    groups = range(ATT_KV_HEADS)
    is_ctx = pl.program_id(1) == NL
    j0 = jnp.where(is_ctx, NL, 0)
    n_pairs = jnp.where(is_ctx, 0, (NT - 1) // 2)
    for g in groups:
        for h in range(ATT_GROUP):
            qt_ref[g, :, h * TM:(h + 1) * TM] = q_ref[0, g * ATT_GROUP + h]

    def scores(t, slot):
        for g in groups:
            k = k_ref[0, g, pl.ds(pl.multiple_of(t * TK, TK), TK), :]
            s_ref[g, slot] = _dot(k, qt_ref[g])

    def weighted_values(t, slot, alphas, accs):
        return [alphas[g] * accs[g] + _dot(v_ref[0, g, t], p_ref[g, slot]) for g in groups]

    def softmax(slot, ms):
        new_ms, alphas = [], []
        for g in groups:
            m_new = jnp.maximum(ms[g], jnp.max(s_ref[g, slot], axis=0, keepdims=True))
            p_ref[g, slot] = jnp.exp2(s_ref[g, slot] - m_new).astype(bf16)
            new_ms.append(m_new)
            alphas.append(jnp.exp2(ms[g] - m_new))
        return new_ms, alphas

    def pair(i, carry):
        ms, alphas, accs = carry
        t = j0 + 2 * i
        scores(t + 1, 1)
        accs = weighted_values(jnp.maximum(t - 1, j0), 1, alphas, accs)
        ms, alphas = softmax(0, ms)
        scores(t + 2, 0)
        accs = weighted_values(t, 0, alphas, accs)
        ms, alphas = softmax(1, ms)
        return ms, alphas, accs

    scores(j0, 0)
    for g in groups:
        p_ref[g, 1] = jnp.zeros((TK, nq), bf16)
    init = ([jnp.full((1, nq), -1e30, f32) for _ in groups], [jnp.ones((1, nq), f32) for _ in groups],
            [jnp.zeros((VROWS, nq), f32) for _ in groups])
    ms, alphas, accs = lax.fori_loop(0, n_pairs, pair, init)
    last = NT - 1
    accs = weighted_values(jnp.maximum(last - 1, j0), 1, alphas, accs)
    ms, alphas = softmax(0, ms)
    accs = weighted_values(last, 0, alphas, accs)
    outs = []
    for g in groups:
        ot = accs[g][:HD] / accs[g][HD:HD + 1]
        outs += [ot[:, h * TM:(h + 1) * TM].T for h in range(ATT_GROUP)]
    o_ref[0] = jnp.concatenate(outs, axis=-1).astype(bf16)


def _attention(aq, ak, avt, n_tiles):
    nq = ATT_GROUP * TM
    return pl.pallas_call(
        _attn_kernel,
        grid=(B, n_tiles),
        in_specs=[pl.BlockSpec((1, ATT_Q_HEADS, HD, TM), lambda b, i: (b, 0, 0, i)),
                  pl.BlockSpec((1, ATT_KV_HEADS, T, HD), lambda b, i: (b, 0, 0, 0)),
                  pl.BlockSpec((1, ATT_KV_HEADS, NT, VROWS, TK), lambda b, i: (b, 0, 0, 0, 0))],
        out_specs=pl.BlockSpec((1, TM, ATT_Q_HEADS * HD), lambda b, i: (b, i, 0)),
        out_shape=jax.ShapeDtypeStruct((B, n_tiles * TM, ATT_Q_HEADS * HD), bf16),
        scratch_shapes=[pltpu.VMEM((ATT_KV_HEADS, HD, nq), bf16),
                        pltpu.VMEM((ATT_KV_HEADS, 2, TK, nq), f32), pltpu.VMEM((ATT_KV_HEADS, 2, TK, nq), bf16)],
        compiler_params=_cparams("parallel", "arbitrary"),
        name="attention",
    )(aq, ak, avt)


def _ret_state_update(s_ref, g_ref, k, v, zeta):
    kzt = (k.astype(f32) * zeta).T
    for h in range(RET_HEADS):
        rows = slice(h * HD, (h + 1) * HD)
        upd = _dot(kzt[rows, :].astype(bf16), v[:, rows])
        s_ref[rows, :] = g_ref[rows, :] * s_ref[rows, :] + upd


def _ret_bwd_kernel(k_ref, v_ref, zeta_ref, g_ref, sb_ref, s_ref):
    @pl.when(pl.program_id(1) == 0)
    def _():
        s_ref[...] = jnp.zeros_like(s_ref)

    for c in reversed(range(RET_STEP)):
        rows = slice(c * RET_CHUNK, (c + 1) * RET_CHUNK)
        sb_ref[0, c] = s_ref[...]
        _ret_state_update(s_ref, g_ref, k_ref[0, rows, :], v_ref[0, rows, :], zeta_ref[...])


def _ret_bwd_states(rk, rv, zeta_b, g_b):
    tile = lambda b, t: (b, NT - 1 - t, 0)
    return pl.pallas_call(
        _ret_bwd_kernel,
        grid=(B, NT),
        in_specs=[pl.BlockSpec((1, TM, 256), tile),
                  pl.BlockSpec((1, TM, 256), tile),
                  _const_spec((RET_CHUNK, 256)), _const_spec((256, HD))],
        out_specs=pl.BlockSpec((1, RET_STEP, 256, HD), lambda b, t: (b, NT - 1 - t, 0, 0)),
        out_shape=jax.ShapeDtypeStruct((B, N_CHUNK, 256, HD), f32),
        scratch_shapes=[pltpu.VMEM((256, HD), f32)],
        compiler_params=_cparams("parallel", "arbitrary"),
        name="ret_bwd_states",
    )(rk, rv, zeta_b, g_b)


def _ret_fwd_kernel(q_ref, k_ref, v_ref, gate_ref, sb_ref, dmask_ref, xif_ref, xib_ref, zeta_ref, g_ref,
                    ng_ref, o_ref, s_ref):
    @pl.when(pl.program_id(1) == 0)
    def _():
        s_ref[...] = jnp.zeros_like(s_ref)

    for c in range(RET_STEP):
        rows = slice(c * RET_CHUNK, (c + 1) * RET_CHUNK)
        q = q_ref[0, rows, :]
        k = k_ref[0, rows, :]
        v = v_ref[0, rows, :]
        qf = q.astype(f32)
        q_xf = (qf * xif_ref[...]).astype(bf16)
        q_xb = (qf * xib_ref[...]).astype(bf16)
        outs = []
        for h in range(RET_HEADS):
            cols = slice(h * HD, (h + 1) * HD)
            s = lax.dot_general(q[:, cols], k[:, cols], (((1,), (1,)), ((), ())), preferred_element_type=f32)
            att = (s * dmask_ref[h]).astype(bf16)
            o = (_dot(att, v[:, cols])
                 + _dot(q_xf[:, cols], s_ref[cols, :].astype(bf16))
                 + _dot(q_xb[:, cols], sb_ref[0, c, cols, :].astype(bf16)))
            ms = jnp.mean(o * o, axis=-1, keepdims=True)
            outs.append(o * lax.rsqrt(ms + EPS))
        y = jnp.concatenate(outs, axis=-1) * ng_ref[...]
        o_ref[0, rows, :] = (y * _silu(gate_ref[0, rows, :].astype(f32))).astype(bf16)
        _ret_state_update(s_ref, g_ref, k, v, zeta_ref[...])


def _retention(rq, rk, rv, rg, sb, dmask, xi_f, xi_b, zeta_f, g_f, ng):
    tile = lambda b, t: (b, (t + NL) % NT, 0)
    blk = pl.BlockSpec((1, TM, 256), tile)
    return pl.pallas_call(
        _ret_fwd_kernel,
        grid=(B, NT),
        in_specs=[blk, blk, blk, blk,
                  pl.BlockSpec((1, RET_STEP, 256, HD), lambda b, t: (b, (t + NL) % NT, 0, 0)),
                  _const_spec((RET_HEADS, RET_CHUNK, RET_CHUNK)),
                  _const_spec((RET_CHUNK, 256)), _const_spec((RET_CHUNK, 256)), _const_spec((RET_CHUNK, 256)),
                  _const_spec((256, HD)), _const_spec((1, 256))],
        out_specs=blk,
        out_shape=jax.ShapeDtypeStruct((B, T, 256), bf16),
        scratch_shapes=[pltpu.VMEM((256, HD), f32)],
        compiler_params=_cparams("parallel", "arbitrary"),
        name="retention",
    )(rq, rk, rv, rg, sb, dmask, xi_f, xi_b, zeta_f, g_f, ng)


def _ret_tables(dec_f, dec_b):
    lg_f = jax.nn.log_sigmoid(dec_f.astype(f32))
    lg_b = jax.nn.log_sigmoid(dec_b.astype(f32))
    pos = jnp.arange(RET_CHUNK, dtype=f32)
    diff = pos[:, None] - pos[None, :]
    d_f = jnp.where(diff[None] >= 0.0, jnp.exp(jnp.maximum(diff, 0.0)[None] * lg_f[:, None, None]), 0.0)
    d_b = jnp.where(diff[None] <= 0.0, jnp.exp(jnp.maximum(-diff, 0.0)[None] * lg_b[:, None, None]), 0.0)
    lanes = lambda t: jnp.repeat(t, HD, axis=1)
    xi_f = lanes(jnp.exp((pos[:, None] + 1.0) * lg_f[None, :]))
    xi_b = lanes(jnp.exp((RET_CHUNK - pos[:, None]) * lg_b[None, :]))
    zeta_f = lanes(jnp.exp((RET_CHUNK - 1.0 - pos[:, None]) * lg_f[None, :]))
    zeta_b = lanes(jnp.exp(pos[:, None] * lg_b[None, :]))
    rows = lambda t: jnp.broadcast_to(jnp.repeat(t, HD)[:, None], (RET_HEADS * HD, HD))
    g_f = rows(jnp.exp(RET_CHUNK * lg_f))
    g_b = rows(jnp.exp(RET_CHUNK * lg_b))
    return d_f + d_b, xi_f, xi_b, zeta_f, zeta_b, g_f, g_b


def _conv_kernel(prev_ref, cur_ref, next_ref, w_ref, b_ref, lg_ref, lb_ref, o_ref, win_ref, sh_ref):
    i = pl.program_id(1)

    def glu(u):
        u = u.astype(f32)
        return u[:, :CONV_W] * jax.nn.sigmoid(u[:, CONV_W:])

    has_prev = jnp.logical_and(i >= 1, i < NL).astype(f32)
    has_next = (i < NL - 1).astype(f32)
    win_ref[0:HALO, :] = glu(prev_ref[0, TM - HALO:TM, :]) * has_prev
    win_ref[HALO:HALO + TM, :] = glu(cur_ref[0])
    win_ref[HALO + TM:2 * HALO + TM, :] = glu(next_ref[0, 0:HALO, :]) * has_next
    sh_rows = sh_ref.shape[1]
    for j in range(8):
        sh_ref[j] = win_ref[j:j + sh_rows, :]
    acc = jnp.zeros((TM, CONV_W), f32)
    off = HALO - CONV_K // 2
    for k in range(CONV_K):
        a, j = divmod(off + k, 8)
        acc = acc + w_ref[k:k + 1, :] * sh_ref[j, 8 * a:8 * a + TM, :]
    y = acc + b_ref[...]
    mu = jnp.mean(y, axis=-1, keepdims=True)
    var = jnp.mean(jnp.square(y - mu), axis=-1, keepdims=True)
    y = (y - mu) * lax.rsqrt(var + EPS) * lg_ref[...] + lb_ref[...]
    o_ref[0] = _silu(y).astype(bf16)


def _conv(cv, dw_w, dw_b, ln_g, ln_b, n_tiles):
    return pl.pallas_call(
        _conv_kernel,
        grid=(B, n_tiles),
        in_specs=[pl.BlockSpec((1, TM, 512), lambda b, i: (b, jnp.maximum(i - 1, 0), 0)),
                  pl.BlockSpec((1, TM, 512), lambda b, i: (b, i, 0)),
                  pl.BlockSpec((1, TM, 512), lambda b, i: (b, jnp.minimum(i + 1, NT - 1), 0)),
                  _const_spec((CONV_K, CONV_W)), _const_spec((1, CONV_W)),
                  _const_spec((1, CONV_W)), _const_spec((1, CONV_W))],
        out_specs=pl.BlockSpec((1, TM, CONV_W), lambda b, i: (b, i, 0)),
        out_shape=jax.ShapeDtypeStruct((B, n_tiles * TM, CONV_W), bf16),
        scratch_shapes=[pltpu.VMEM((TM + 2 * HALO, CONV_W), f32),
                        pltpu.VMEM((8, TM + 2 * HALO - 8, CONV_W), f32)],
        compiler_params=_cparams("parallel", "parallel"),
        name="conv",
    )(cv, cv, cv, dw_w, dw_b, ln_g, ln_b)


def _fn_stage1_kernel(x_ref, g_ref, a_ref):
    for s in range(FN_STEP):
        res = _dot(g_ref[s], x_ref[0, :, s * 256:(s + 1) * 256])
        a_ref[0, 0, s] = res[:FN2].astype(bf16)
        a_ref[0, 1, s] = res[FN2:].astype(bf16)


def _fn_stage2_kernel(a_ref, m_ref, p_ref):
    res = _dot(m_ref[...], a_ref[0])
    p_ref[0, 0, :FN1] = res[:FN1].astype(bf16)
    p_ref[0, 1, :FN1] = res[FN1:].astype(bf16)
    if p_ref.shape[2] > FN1:
        p_ref[0, :, FN1:] = jnp.zeros((2, p_ref.shape[2] - FN1, p_ref.shape[3]), bf16)


def _fn_ctx_kernel(u_ref, f_ref, _p_in, p_ref):
    res = _dot(f_ref[...], u_ref[0])
    p_ref[0, 0] = res[:C].astype(bf16)
    p_ref[0, 1] = res[C:].astype(bf16)


def _fnet_dft(fn, g_tab, m2, fc, with_ctx):
    p_rows = T if with_ctx else S
    x = fn.reshape(B, T // FN1, FN1 * 256)
    a = pl.pallas_call(
        _fn_stage1_kernel,
        grid=(B, FN1 // FN_STEP),
        in_specs=[pl.BlockSpec((1, FN2, FN_STEP * 256), lambda b, j: (b, 0, j)),
                  pl.BlockSpec((FN_STEP, 2 * FN2, FN2), lambda b, j: (j, 0, 0))],
        out_specs=pl.BlockSpec((1, 2, FN_STEP, FN2, 256), lambda b, j: (b, 0, j, 0, 0)),
        out_shape=jax.ShapeDtypeStruct((B, 2, FN1, FN2, 256), bf16),
        compiler_params=_cparams("parallel", "parallel"),
        name="fnet_stage1",
    )(x, g_tab)
    a = a.reshape(B, 2 * FN1, FN2 * 256)
    p = pl.pallas_call(
        _fn_stage2_kernel,
        grid=(B, FN2 * 256 // FN_COLS),
        in_specs=[pl.BlockSpec((1, 2 * FN1, FN_COLS), lambda b, j: (b, 0, j)),
                  _const_spec((2 * FN1, 2 * FN1))],
        out_specs=pl.BlockSpec((1, 2, p_rows // FN2, FN_COLS), lambda b, j: (b, 0, 0, j)),
        out_shape=jax.ShapeDtypeStruct((B, 2, p_rows // FN2, FN2 * 256), bf16),
        compiler_params=_cparams("parallel", "parallel"),
        name="fnet_stage2",
    )(a, m2)
    p = p.reshape(B, 2, p_rows, 256)
    if with_ctx:
        p = pl.pallas_call(
            _fn_ctx_kernel,
            grid=(B,),
            in_specs=[pl.BlockSpec((1, C, 256), lambda b: (b, NL, 0)),
                      _const_spec((2 * C, C)),
                      pl.BlockSpec(memory_space=pl.ANY)],
            out_specs=pl.BlockSpec((1, 2, C, 256), lambda b: (b, 0, NL, 0)),
            out_shape=jax.ShapeDtypeStruct((B, 2, T, 256), bf16),
            input_output_aliases={2: 0},
            compiler_params=_cparams("parallel"),
            name="fnet_ctx",
        )(fn, fc, p)
    return p


def _merge_kernel(lat_ref, ctx_ref, gate_ref, p_ref, r_ref, o_ref, cv_ref, gt_ref, cbd_ref, sbd_ref,
                  fw_ref, rw_ref, aw_ref, cw_ref, ow_ref, out_ref):
    i = pl.program_id(1)
    scale = jnp.where(i == NL, (HD * C) ** -0.5, (HD * S) ** -0.5)
    yfn = (_dot(p_ref[0, 0], cbd_ref[...]) + _dot(p_ref[0, 1], sbd_ref[...])) * scale
    branches = (
        (yfn.astype(bf16), fw_ref),
        (r_ref[0], rw_ref),
        (o_ref[0], aw_ref),
        (cv_ref[0], cw_ref),
    )
    merged = jnp.zeros((TM, D), f32)
    for j, (xin, w_ref) in enumerate(branches):
        g = jax.nn.sigmoid(gt_ref[0, :, j * D:(j + 1) * D].astype(f32))
        merged = merged + g * _dot(xin, w_ref[...])
    out_ref[0] = (_stream_tile(lat_ref, ctx_ref)
                  + gate_ref[0] * _dot(merged.astype(bf16), ow_ref[...]))


def _merge(h_lat, h_ctx, ctx_tile, mod, p, r, o, cvo, gt, cbd, sbd, fw, rw, aw, cw, ow, n_tiles):
    tile = lambda w: pl.BlockSpec((1, TM, w), lambda b, i: (b, i, 0))
    return pl.pallas_call(
        _merge_kernel,
        grid=(B, n_tiles),
        in_specs=_stream_specs(ctx_tile) + [
            _mod_spec(2),
            pl.BlockSpec((1, 2, TM, 256), lambda b, i: (b, 0, i, 0)),
            tile(256), tile(512), tile(256), tile(4 * D),
            _const_spec((256, 256)), _const_spec((256, 256)),
            _const_spec((256, D)), _const_spec((256, D)), _const_spec((512, D)),
            _const_spec((256, D)), _const_spec((D, D))],
        out_specs=tile(D),
        out_shape=jax.ShapeDtypeStruct((B, n_tiles * TM, D), f32),
        compiler_params=_cparams("parallel", "parallel"),
        name="merge",
    )(h_lat, h_ctx, mod, p, r, o, cvo, gt, cbd, sbd, fw, rw, aw, cw, ow)


def _swiglu_rows(a, wg_ref, wu_ref, wd_ref, h_ref):
    for c in range(FFN // FFN_CHUNK):
        cols = slice(c * FFN_CHUNK, (c + 1) * FFN_CHUNK)
        h_ref[:, cols] = (_silu(_dot(a, wg_ref[:, cols])) * _dot(a, wu_ref[:, cols])).astype(bf16)
    return _dot(h_ref[...], wd_ref[...])


def _ffn_kernel(h_ref, sh_ref, sc_ref, gate_ref, g_ref, wg_ref, wu_ref, wd_ref, out_ref, hid_ref):
    x = h_ref[0]
    a = _norm_mod(x, g_ref[...], sh_ref[0], sc_ref[0]).astype(bf16)
    out_ref[0] = x + gate_ref[0] * _swiglu_rows(a, wg_ref, wu_ref, wd_ref, hid_ref)


def _ffn(h, mod, g2, wg, wu, wd):
    tile = pl.BlockSpec((1, TM, D), lambda b, i: (b, i, 0))
    return pl.pallas_call(
        _ffn_kernel,
        grid=(B, NT),
        in_specs=[tile, _mod_spec(3), _mod_spec(4), _mod_spec(5), _const_spec((1, D)),
                  _const_spec((D, FFN)), _const_spec((D, FFN)), _const_spec((FFN, D))],
        out_specs=tile,
        out_shape=jax.ShapeDtypeStruct((B, T, D), f32),
        scratch_shapes=[pltpu.VMEM((TM, FFN), bf16)],
        compiler_params=_cparams("parallel", "parallel"),
        name="ffn",
    )(h, mod, mod, mod, g2, wg, wu, wd)


def _moe_prep_kernel(h_ref, sh_ref, sc_ref, g_ref, rw_ref, x_ref, e_ref, w_ref):
    f = _norm_mod(h_ref[0], g_ref[...], sh_ref[0], sc_ref[0])
    x_ref[...] = f
    rw = rw_ref[...]
    f_hi = f.astype(bf16)
    f_lo = (f - f_hi.astype(f32)).astype(bf16)
    w_hi = rw.astype(bf16)
    w_lo = (rw - w_hi.astype(f32)).astype(bf16)
    logits = (_dot(f_lo, w_lo) + _dot(f_lo, w_hi)) + (_dot(f_hi, w_lo) + _dot(f_hi, w_hi))
    lane = lax.broadcasted_iota(jnp.int32, logits.shape, 1)
    neg = jnp.float32(-jnp.inf)
    logits = jnp.where(lane < N_EXPERTS, logits, neg)
    m1 = jnp.max(logits, axis=-1, keepdims=True)
    i1 = jnp.min(jnp.where(logits == m1, lane, 128), axis=-1, keepdims=True)
    rest = jnp.where(lane == i1, neg, logits)
    m2 = jnp.max(rest, axis=-1, keepdims=True)
    i2 = jnp.min(jnp.where(rest == m2, lane, 128), axis=-1, keepdims=True)
    z = jnp.exp(m2 - m1)
    w1 = 1.0 / (1.0 + z)
    w2 = z / (1.0 + z)
    e_ref[...] = jnp.where(lane == 0, i1, jnp.where(lane == 1, i2, 0))
    w_ref[...] = jnp.where(lane == 0, w1, jnp.where(lane == 1, w2, 0.0))


def _moe_prep(h, mod, g2, router_pad):
    n_steps = B * NL

    def mod_spec(k):
        return pl.BlockSpec((1, 1, D), lambda t: ((t // NL) * 6 + k, 0, 0))

    return pl.pallas_call(
        _moe_prep_kernel,
        grid=(n_steps,),
        in_specs=[pl.BlockSpec((1, TM, D), lambda t: (t // NL, t % NL, 0)),
                  mod_spec(3), mod_spec(4), _const_spec((1, D)), _const_spec((D, 128))],
        out_specs=[pl.BlockSpec((TM, D), lambda t: (t, 0)),
                   pl.BlockSpec((TM, 128), lambda t: (t, 0)),
                   pl.BlockSpec((TM, 128), lambda t: (t, 0))],
        out_shape=[jax.ShapeDtypeStruct((N_TOK, D), f32),
                   jax.ShapeDtypeStruct((N_TOK, 128), jnp.int32),
                   jax.ShapeDtypeStruct((N_TOK, 128), f32)],
        compiler_params=_cparams("parallel"),
        name="moe_prep",
    )(h, mod, mod, g2, router_pad)


def _moe_scatter_kernel(nvalid_ref, dest_ref, x_ref, xb_hbm, zero_ref, sem):
    def zero_copy(slot):
        return pltpu.make_async_copy(zero_ref.at[pl.ds(0, 1), :], xb_hbm.at[pl.ds(slot, 1), :], sem.at[2])

    @pl.when(pl.program_id(0) == 0)
    def _():
        zero_ref[...] = jnp.zeros_like(zero_ref)

        def pad_rows(start):
            def each_block(blk, carry):
                def each_row(r, c):
                    copy = zero_copy(blk * MOE_BLOCK + r)
                    copy.start() if start else copy.wait()
                    return c

                return lax.fori_loop(nvalid_ref[blk], MOE_BLOCK, each_row, carry)

            lax.fori_loop(0, N_BLOCKS, each_block, 0)

        pad_rows(True)
        pad_rows(False)

    def issue(r, carry):
        src = x_ref.at[pl.ds(r, 1), :]
        pltpu.make_async_copy(src, xb_hbm.at[pl.ds(dest_ref[0, 0, 2 * r], 1), :], sem.at[0]).start()
        pltpu.make_async_copy(src, xb_hbm.at[pl.ds(dest_ref[0, 0, 2 * r + 1], 1), :], sem.at[1]).start()
        return carry

    lax.fori_loop(0, TM, issue, 0, unroll=DMA_UNROLL)
    for k in range(2):
        pltpu.make_async_copy(x_ref, xb_hbm.at[pl.ds(0, TM), :], sem.at[k]).wait()


def _moe_scatter(nvalid, dest, xt):
    grid_spec = pltpu.PrefetchScalarGridSpec(
        num_scalar_prefetch=1,
        grid=(N_TOK // TM,),
        in_specs=[pl.BlockSpec((1, 1, 2 * TM), lambda i, nv: (i, 0, 0), memory_space=pltpu.SMEM),
                  pl.BlockSpec((TM, D), lambda i, nv: (i, 0))],
        out_specs=pl.BlockSpec(memory_space=pl.ANY),
        scratch_shapes=[pltpu.VMEM((8, D), f32), pltpu.SemaphoreType.DMA((3,))],
    )
    return pl.pallas_call(
        _moe_scatter_kernel,
        grid_spec=grid_spec,
        out_shape=jax.ShapeDtypeStruct((N_ROWS, D), f32),
        compiler_params=_cparams("arbitrary"),
        name="moe_scatter",
    )(nvalid, dest.reshape(N_TOK // TM, 1, 2 * TM), xt)


def _moe_expert_kernel(be_ref, x_ref, wg_ref, wu_ref, wd_ref, y_ref, hid_ref):
    a = x_ref[...].astype(bf16)
    y_ref[...] = _swiglu_rows(a, wg_ref.at[0], wu_ref.at[0], wd_ref.at[0], hid_ref)


def _moe_experts(blk_e, xb, wg, wu, wd):
    grid_spec = pltpu.PrefetchScalarGridSpec(
        num_scalar_prefetch=1,
        grid=(N_BLOCKS,),
        in_specs=[pl.BlockSpec((MOE_BLOCK, D), lambda i, be: (i, 0)),
                  pl.BlockSpec((1, D, FFN), lambda i, be: (be[i], 0, 0)),
                  pl.BlockSpec((1, D, FFN), lambda i, be: (be[i], 0, 0)),
                  pl.BlockSpec((1, FFN, D), lambda i, be: (be[i], 0, 0))],
        out_specs=pl.BlockSpec((MOE_BLOCK, D), lambda i, be: (i, 0)),
        scratch_shapes=[pltpu.VMEM((MOE_BLOCK, FFN), bf16)],
    )
    return pl.pallas_call(
        _moe_expert_kernel,
        grid_spec=grid_spec,
        out_shape=jax.ShapeDtypeStruct((N_ROWS, D), f32),
        compiler_params=_cparams("arbitrary"),
        name="moe_experts",
    )(blk_e, xb, wg, wu, wd)


def _moe_combine_kernel(dest_ref, h_ref, gate_ref, w_ref, y_hbm, out_ref, y0_ref, y1_ref, sem):
    def issue(r, carry):
        pltpu.make_async_copy(y_hbm.at[pl.ds(dest_ref[0, 0, 2 * r], 1), :], y0_ref.at[pl.ds(r, 1), :],
                              sem.at[0]).start()
        pltpu.make_async_copy(y_hbm.at[pl.ds(dest_ref[0, 0, 2 * r + 1], 1), :], y1_ref.at[pl.ds(r, 1), :],
                              sem.at[1]).start()
        return carry

    lax.fori_loop(0, CMB, issue, 0, unroll=DMA_UNROLL)
    pltpu.make_async_copy(y_hbm.at[pl.ds(0, CMB), :], y0_ref, sem.at[0]).wait()
    pltpu.make_async_copy(y_hbm.at[pl.ds(0, CMB), :], y1_ref, sem.at[1]).wait()
    w = w_ref[...]
    y = w[:, 0:1] * y0_ref[...] + w[:, 1:2] * y1_ref[...]
    out_ref[0] = h_ref[0] + gate_ref[0] * y


def _moe_combine(dest, h, mod, w_pad, yb):
    n_per_b = S // CMB
    return pl.pallas_call(
        _moe_combine_kernel,
        grid=(B, n_per_b),
        in_specs=[pl.BlockSpec((1, 1, 2 * CMB), lambda b, i: (b * n_per_b + i, 0, 0), memory_space=pltpu.SMEM),
                  pl.BlockSpec((1, CMB, D), lambda b, i: (b, i, 0)),
                  pl.BlockSpec((1, 1, D), lambda b, i: (b * 6 + 5, 0, 0)),
                  pl.BlockSpec((CMB, 128), lambda b, i: (b * n_per_b + i, 0)),
                  pl.BlockSpec(memory_space=pl.ANY)],
        out_specs=pl.BlockSpec((1, CMB, D), lambda b, i: (b, i, 0)),
        out_shape=jax.ShapeDtypeStruct((B, S, D), f32),
        scratch_shapes=[pltpu.VMEM((CMB, D), f32), pltpu.VMEM((CMB, D), f32), pltpu.SemaphoreType.DMA((2,))],
        compiler_params=_cparams("arbitrary", "arbitrary"),
        name="moe_combine",
    )(dest.reshape(N_TOK // CMB, 1, 2 * CMB), h, mod, w_pad, yb)


def _moe_routing(top_e):
    e = top_e.reshape(-1)
    onehot = (e[:, None] == jnp.arange(N_EXPERTS, dtype=jnp.int32)[None, :]).astype(jnp.int32)
    counts = jnp.sum(onehot, axis=0)
    rank = jnp.sum((jnp.cumsum(onehot, axis=0) - 1) * onehot, axis=1)
    padded = (counts + MOE_BLOCK - 1) // MOE_BLOCK * MOE_BLOCK
    pad_end = jnp.cumsum(padded)
    pad_start = pad_end - padded
    dest = (pad_start[e] + rank).astype(jnp.int32)
    blk_start = jnp.arange(N_BLOCKS, dtype=jnp.int32) * MOE_BLOCK
    blk_e = jnp.minimum(jnp.sum((pad_end[None, :] <= blk_start[:, None]).astype(jnp.int32), axis=1),
                        N_EXPERTS - 1).astype(jnp.int32)
    nvalid = jnp.clip(pad_start[blk_e] + counts[blk_e] - blk_start, 0, MOE_BLOCK).astype(jnp.int32)
    nvalid = jnp.where(blk_start < pad_end[N_EXPERTS - 1], nvalid, 0)
    return dest, blk_e, nvalid


def _moe(h, mod, g2, router_w, wg, wu, wd):
    router_pad = jnp.pad(router_w, ((0, 0), (0, 128 - N_EXPERTS)))
    xt, e_pad, w_pad = _moe_prep(h, mod, g2, router_pad)
    dest, blk_e, nvalid = _moe_routing(e_pad[:, :2])
    xb = _moe_scatter(nvalid, dest, xt)
    yb = _moe_experts(blk_e, xb, wg, wu, wd)
    return _moe_combine(dest, h, mod, w_pad, yb)


def _token_mixers(h_lat, h_ctx, ctx_tile, mod, i, with_ctx, tabs, norm1_g, w_in, fnet_w, ret_decay_fwd,
                  ret_decay_bwd, ret_norm_g, ret_w, attn_qn_g, attn_kn_g, attn_w, conv_dw_w, conv_dw_b, conv_ln_g,
                  conv_ln_b, conv_w_out, w_out):
    cos128, sin128, g_tab, m2, fc, cbd, sbd, bd = tabs
    n_tiles = NT if with_ctx else NL
    qg = jnp.tile(attn_qn_g[i].astype(f32), ATT_Q_HEADS)[None, :]
    kg = jnp.tile(attn_kn_g[i].astype(f32), ATT_KV_HEADS)[None, :]
    fn, rq, rk, rv, rg, aq, ak, avt, cv, gt = _proj_in(
        h_lat, h_ctx, ctx_tile, mod, norm1_g[i][None, :], w_in[i].astype(bf16), cos128, sin128, qg, kg, bd)

    o = _attention(aq, ak, avt, n_tiles)

    dmask, xi_f, xi_b, zeta_f, zeta_b, g_f, g_b = _ret_tables(ret_decay_fwd[i], ret_decay_bwd[i])
    sb = _ret_bwd_states(rk, rv, zeta_b, g_b)
    r = _retention(rq, rk, rv, rg, sb, dmask, xi_f, xi_b, zeta_f, g_f, ret_norm_g[i][None, :].astype(f32))

    cvo = _conv(cv, conv_dw_w[i], conv_dw_b[i][None, :], conv_ln_g[i][None, :], conv_ln_b[i][None, :], n_tiles)
    p = _fnet_dft(fn, g_tab, m2, fc, with_ctx)
    return _merge(h_lat, h_ctx, ctx_tile, mod, p, r, o, cvo, gt, cbd, sbd, fnet_w[i].astype(bf16),
                  ret_w[i].astype(bf16), attn_w[i].astype(bf16), conv_w_out[i].astype(bf16),
                  w_out[i].astype(bf16), n_tiles)


def kernel(x, c, ctx, c_ctx, ada_w, ada_b, norm1_g, norm2_g, w_in, fnet_w, ret_decay_fwd, ret_decay_bwd, ret_norm_g, ret_w, attn_qn_g, attn_kn_g, attn_w, conv_dw_w, conv_dw_b, conv_ln_g, conv_ln_b, conv_w_out, w_out, ffn_w_gate, ffn_w_up, ffn_w_down, router_w, moe_w_gate, moe_w_up, moe_w_down):
    cos128, sin128 = _rope_tables()
    as_bf16 = lambda t: jnp.asarray(t, f32).astype(bf16)
    tabs = (cos128, sin128, as_bf16(_G_NP), as_bf16(_M2_NP), as_bf16(_FC_NP), as_bf16(_CBD_NP),
            as_bf16(_SBD_NP), as_bf16(_BD_NP))
    c_rows = jnp.concatenate([c, c_ctx[None, :], jnp.zeros((8 - B - 1, D), f32)], axis=0)
    mods = _ada(c_rows, ada_w, ada_b).reshape(DEPTH, 8 * 6, 1, D)
    h = None
    for i in range(DEPTH):
        with_ctx = i < DEPTH - 1
        mod = mods[i]
        stream = (x, ctx, 0) if i == 0 else (h, h, NL)
        h = _token_mixers(*stream, mod, i, with_ctx, tabs, norm1_g, w_in, fnet_w, ret_decay_fwd, ret_decay_bwd,
                          ret_norm_g, ret_w, attn_qn_g, attn_kn_g, attn_w, conv_dw_w, conv_dw_b, conv_ln_g,
                          conv_ln_b, conv_w_out, w_out)
        j = i // 2
        g2 = norm2_g[i][None, :]
        if i % 2 == 0:
            h = _ffn(h, mod, g2, ffn_w_gate[j].astype(bf16), ffn_w_up[j].astype(bf16),
                     ffn_w_down[j].astype(bf16))
        else:
            h = _moe(h, mod, g2, router_w[j], moe_w_gate[j].astype(bf16), moe_w_up[j].astype(bf16),
                     moe_w_down[j].astype(bf16))
    return h
```

```python
import functools
import math

import numpy as np
import jax
import jax.numpy as jnp
from jax import lax
from jax.experimental import pallas as pl
from jax.experimental.pallas import tpu as pltpu

f32 = jnp.float32
bf16 = jnp.bfloat16

D = 1024
B = 2
S = 8192
C = 256
T = S + C
DEPTH = 2
GRID_W = 64
HD = 64
EPS = 1e-6
RET_HEADS = 4
RET_CHUNK = 128
N_CHUNK = T // RET_CHUNK
ATT_Q_HEADS = 8
ATT_KV_HEADS = 2
ATT_GROUP = 4
CONV_K = 31
CONV_W = 256
HALO = 16
IN_COLS = 6656
FFN = 2816
FFN_CHUNK = 256
N_EXPERTS = 8
MOE_BLOCK = 128
ROPE_THETA = 10000.0

TM = 256
NT = T // TM
NL = S // TM
RET_STEP = TM // RET_CHUNK
TK = 256
VROWS = HD + 16
LOG2E = math.log2(math.e)
ATT_SAFE_BOUND = 48.0

N_TOK = B * S
N_ASSIGN = N_TOK * 2
N_BLOCKS = N_ASSIGN // MOE_BLOCK + N_EXPERTS
N_ROWS = N_BLOCKS * MOE_BLOCK
CMB = 256
DMA_UNROLL = 8

FN1 = 64
FN2 = 128
FN_STEP = 8
FN_COLS = 4096

VMEM_LIMIT = 56 * 1024 * 1024


def _cparams(*sem):
    return pltpu.CompilerParams(dimension_semantics=sem, vmem_limit_bytes=VMEM_LIMIT)


def _const_spec(shape):
    nd = len(shape)
    return pl.BlockSpec(shape, lambda *_: (0,) * nd, pipeline_mode=pl.Buffered(1))


def _mod_spec(k):
    return pl.BlockSpec((1, 1, D), lambda b, i: (jnp.where(i == NL, 2, b) * 6 + k, 0, 0))


def _dot(a, b):
    return jnp.dot(a, b, preferred_element_type=f32)


def _silu(x):
    return x * jax.nn.sigmoid(x)


def _dft_tables():
    n1 = np.arange(FN1)[:, None, None]
    k2 = np.arange(FN2)[None, :, None]
    n2 = np.arange(FN2)[None, None, :]
    ph = 2.0 * np.pi * (((n1 + FN1 * n2) * k2) % S) / S
    g = np.concatenate([np.cos(ph), -np.sin(ph)], axis=1)
    k1 = np.arange(FN1)[:, None]
    m1 = np.arange(FN1)[None, :]
    ph = 2.0 * np.pi * ((k1 * m1) % FN1) / FN1
    c64, s64 = np.cos(ph), np.sin(ph)
    m2 = np.block([[c64, s64], [-s64, c64]])
    kc = np.arange(C)[:, None]
    nc = np.arange(C)[None, :]
    ph = 2.0 * np.pi * ((kc * nc) % C) / C
    fc = np.concatenate([np.cos(ph), -np.sin(ph)], axis=0)
    eye4 = np.eye(4)
    cbd = np.kron(eye4, c64)
    sbd = np.kron(eye4, s64)
    bd = np.kron(np.eye(8), np.ones((HD, HD)))
    return g, m2, fc, cbd, sbd, bd


_G_NP, _M2_NP, _FC_NP, _CBD_NP, _SBD_NP, _BD_NP = _dft_tables()


def _rope_tables():
    rows = S // GRID_W
    row = jnp.broadcast_to(jnp.arange(rows)[:, None], (rows, GRID_W)).reshape(-1).astype(f32)
    col = jnp.broadcast_to(jnp.arange(GRID_W)[None, :], (rows, GRID_W)).reshape(-1).astype(f32)
    n_axis = HD // 4
    inv = ROPE_THETA ** (-jnp.arange(n_axis, dtype=f32) / n_axis)
    ang = jnp.concatenate([row[:, None] * inv, col[:, None] * inv], axis=-1)
    cos, sin = jnp.cos(ang), jnp.sin(ang)
    cos = jnp.concatenate([cos, jnp.ones((C, HD // 2), f32)], axis=0)
    sin = jnp.concatenate([sin, jnp.zeros((C, HD // 2), f32)], axis=0)
    cos128 = jnp.concatenate([cos, cos, cos, cos], axis=-1)
    sin128 = jnp.concatenate([-sin, sin, -sin, sin], axis=-1)
    return cos128, sin128


def _ada_kernel(c_ref, w_ref, b_ref, o_ref):
    s = _silu(c_ref[...])
    o_ref[0] = jnp.dot(s, w_ref[0], preferred_element_type=f32,
                       precision=lax.Precision.HIGHEST) + b_ref[0]


def _ada(c_rows, ada_w, ada_b):
    tn = 1536
    return pl.pallas_call(
        _ada_kernel,
        grid=(DEPTH, 6 * D // tn),
        in_specs=[pl.BlockSpec((8, D), lambda l, j: (0, 0)),
                  pl.BlockSpec((1, D, tn), lambda l, j: (l, 0, j)),
                  pl.BlockSpec((1, 1, tn), lambda l, j: (l, 0, j))],
        out_specs=pl.BlockSpec((1, 8, tn), lambda l, j: (l, 0, j)),
        out_shape=jax.ShapeDtypeStruct((DEPTH, 8, 6 * D), f32),
        compiler_params=_cparams("parallel", "parallel"),
        name="ada",
    )(c_rows, ada_w, ada_b.reshape(DEPTH, 1, 6 * D))


def _norm_mod(x, g, shift, scale):
    ms = jnp.mean(x * x, axis=-1, keepdims=True)
    return (x * lax.rsqrt(ms + EPS) * g) * (1.0 + scale) + shift


def _swap_halves(x):
    w = x.shape[-1]
    lane = lax.broadcasted_iota(jnp.int32, x.shape, 1)
    fwd = pltpu.roll(x, w - HD // 2, 1)
    bwd = pltpu.roll(x, HD // 2, 1)
    return jnp.where((lane % HD) < HD // 2, fwd, bwd)


def _rope(x, cos, sin):
    return x * cos + _swap_halves(x) * sin


def _head_rms(x, gain, bd):
    x2 = x * x
    hi = x2.astype(bf16)
    lo = (x2 - hi.astype(f32)).astype(bf16)
    ss = _dot(hi, bd) + _dot(lo, bd)
    return x * lax.rsqrt(ss * (1.0 / HD) + EPS) * gain


def _stream_specs(ctx_tile):
    return [pl.BlockSpec((1, TM, D), lambda b, i: (b, jnp.minimum(i, NL - 1), 0)),
            pl.BlockSpec((1, TM, D), lambda b, i: (b, ctx_tile, 0))]


def _stream_tile(lat_ref, ctx_ref):
    return jnp.where(pl.program_id(1) == NL, ctx_ref[0], lat_ref[0])


def _proj_in_kernel(lat_ref, ctx_ref, sh_ref, sc_ref, g_ref, w_ref, cos_ref, sin_ref, qg_ref, kg_ref, bd_ref,
                    fn_ref, rq_ref, rk_ref, rv_ref, rg_ref, aq_ref, ak_ref, avt_ref, cv_ref, gt_ref):
    a = _norm_mod(_stream_tile(lat_ref, ctx_ref), g_ref[...], sh_ref[0], sc_ref[0]).astype(bf16)

    def proj(c0, c1):
        return _dot(a, w_ref[:, c0:c1])

    cos = cos_ref[...]
    sin = sin_ref[...]
    cos2 = jnp.concatenate([cos, cos], axis=-1)
    sin2 = jnp.concatenate([sin, sin], axis=-1)
    cos4 = jnp.concatenate([cos2, cos2], axis=-1)
    sin4 = jnp.concatenate([sin2, sin2], axis=-1)

    fn_ref[0] = proj(0, 256).astype(bf16)
    rq_ref[0] = _rope(proj(256, 512), cos2, sin2).astype(bf16)
    rk_ref[0] = (_rope(proj(512, 768), cos2, sin2) * (HD ** -0.5)).astype(bf16)
    rv_ref[0] = proj(768, 1024).astype(bf16)
    rg_ref[0] = proj(1024, 1280).astype(bf16)

    q = _head_rms(proj(1280, 1792), qg_ref[...], bd_ref[...])
    qt = (_rope(q, cos4, sin4) * (HD ** -0.5 * LOG2E)).T
    for h in range(ATT_Q_HEADS):
        aq_ref[0, h] = qt[h * HD:(h + 1) * HD, :].astype(bf16)
    k = _head_rms(proj(1792, 1920), kg_ref[...], bd_ref[:2 * HD, :2 * HD])
    k = _rope(k, cos, sin).astype(bf16)
    vt = proj(1920, 2048).T
    row = lax.broadcasted_iota(jnp.int32, (VROWS - HD, TM), 0)
    ones_row = jnp.where(row == 0, 1.0, 0.0).astype(bf16)
    for g in range(ATT_KV_HEADS):
        ak_ref[0, g] = k[:, g * HD:(g + 1) * HD]
        avt_ref[0, g, 0, :HD] = vt[g * HD:(g + 1) * HD, :].astype(bf16)
        avt_ref[0, g, 0, HD:] = ones_row
    cv_ref[0] = proj(2048, 2560).astype(bf16)
    for j in range(4):
        gt_ref[0, :, j * D:(j + 1) * D] = proj(2560 + j * D, 2560 + (j + 1) * D).astype(bf16)


def _proj_in(h_lat, h_ctx, ctx_tile, mod, g1, w_in, cos128, sin128, qg, kg, bd):
    tile = lambda w: pl.BlockSpec((1, TM, w), lambda b, i: (b, i, 0))
    out_shapes = [
        jax.ShapeDtypeStruct((B, T, 256), bf16),
        jax.ShapeDtypeStruct((B, T, 256), bf16),
        jax.ShapeDtypeStruct((B, T, 256), bf16),
        jax.ShapeDtypeStruct((B, T, 256), bf16),
        jax.ShapeDtypeStruct((B, T, 256), bf16),
        jax.ShapeDtypeStruct((B, ATT_Q_HEADS, HD, T), bf16),
        jax.ShapeDtypeStruct((B, ATT_KV_HEADS, T, HD), bf16),
        jax.ShapeDtypeStruct((B, ATT_KV_HEADS, NT, VROWS, TM), bf16),
        jax.ShapeDtypeStruct((B, T, 512), bf16),
        jax.ShapeDtypeStruct((B, T, 4 * D), bf16),
    ]
    out_specs = [
        tile(256), tile(256), tile(256), tile(256), tile(256),
        pl.BlockSpec((1, ATT_Q_HEADS, HD, TM), lambda b, i: (b, 0, 0, i)),
        pl.BlockSpec((1, ATT_KV_HEADS, TM, HD), lambda b, i: (b, 0, i, 0)),
        pl.BlockSpec((1, ATT_KV_HEADS, 1, VROWS, TM), lambda b, i: (b, 0, i, 0, 0)),
        tile(512), tile(4 * D),
    ]
    return pl.pallas_call(
        _proj_in_kernel,
        grid=(B, NT),
        in_specs=_stream_specs(ctx_tile) + [
            _mod_spec(0), _mod_spec(1), _const_spec((1, D)), _const_spec((D, IN_COLS)),
            pl.BlockSpec((TM, 128), lambda b, i: (i, 0)),
            pl.BlockSpec((TM, 128), lambda b, i: (i, 0)),
            _const_spec((1, 512)), _const_spec((1, 128)), _const_spec((512, 512))],
        out_specs=out_specs,
        out_shape=out_shapes,
        compiler_params=_cparams("parallel", "parallel"),
        name="proj_in",
    )(h_lat, h_ctx, mod, mod, g1, w_in, cos128, sin128, qg, kg, bd)


def _attn_kernel(q_ref, k_ref, v_ref, kn_ref, o_ref, qt_ref, s_ref, p_ref, acc_ref):
    nq = ATT_GROUP * TM
    groups = range(ATT_KV_HEADS)
    last = NT - 1
    is_ctx = pl.program_id(1) == NL
    j0 = jnp.where(is_ctx, NL, 0)
    n_pairs = jnp.where(is_ctx, 0, (NT - 1) // 2)
    for g in groups:
        for h in range(ATT_GROUP):
            qt_ref[g, :, h * TM:(h + 1) * TM] = q_ref[0, g * ATT_GROUP + h]

    bounds = []
    for g in groups:
        qf = qt_ref[g].astype(f32)
        qn = jnp.sqrt(jnp.sum(qf * qf, axis=0, keepdims=True))
        bounds.append(qn * jnp.concatenate([kn_ref[0, g, 0:1, :]] * (nq // 128), axis=-1))
    bound_max = jnp.max(jnp.maximum(bounds[0], bounds[1]))

    def key_block(g, t):
        return k_ref[0, g, pl.ds(pl.multiple_of(t * TK, TK), TK), :]

    @pl.when(bound_max <= ATT_SAFE_BOUND)
    def _():
        def probs(t, slot):
            for g in groups:
                p_ref[g, slot] = jnp.exp2(_dot(key_block(g, t), qt_ref[g]) - bounds[g]).astype(bf16)

        def add_values(t, slot, accs):
            return [accs[g] + _dot(v_ref[0, g, t], p_ref[g, slot]) for g in groups]

        def pair(i, accs):
            t = j0 + 2 * i
            probs(t + 1, 1)
            accs = add_values(t, 0, accs)
            probs(t + 2, 0)
            return add_values(t + 1, 1, accs)

        probs(j0, 0)
        accs = lax.fori_loop(0, n_pairs, pair, [jnp.zeros((VROWS, nq), f32) for _ in groups])
        accs = add_values(last, 0, accs)
        for g in groups:
            acc_ref[g] = accs[g]

    @pl.when(bound_max > ATT_SAFE_BOUND)
    def _():
        _attn_online_softmax(k_ref, v_ref, qt_ref, s_ref, p_ref, acc_ref, j0, n_pairs)

    outs = []
    for g in groups:
        ot = acc_ref[g, :HD, :] / acc_ref[g, HD:HD + 1, :]
        outs += [ot[:, h * TM:(h + 1) * TM].T for h in range(ATT_GROUP)]
    o_ref[0] = jnp.concatenate(outs, axis=-1).astype(bf16)


def _attn_online_softmax(k_ref, v_ref, qt_ref, s_ref, p_ref, acc_ref, j0, n_pairs):
    nq = ATT_GROUP * TM
    groups = range(ATT_KV_HEADS)

    def scores(t, slot):
        for g in groups:
            k = k_ref[0, g, pl.ds(pl.multiple_of(t * TK, TK), TK), :]
            s_ref[g, slot] = _dot(k, qt_ref[g])

    def weighted_values(t, slot, alphas, accs):
        return [alphas[g] * accs[g] + _dot(v_ref[0, g, t], p_ref[g, slot]) for g in groups]

    def softmax(slot, ms):
        new_ms, alphas = [], []
        for g in groups:
            s = s_ref[g, slot]
            m_new = jnp.maximum(ms[g], jnp.max(s, axis=0, keepdims=True))
            p_ref[g, slot] = jnp.exp2(s - m_new).astype(bf16)
            new_ms.append(m_new)
            alphas.append(jnp.exp2(ms[g] - m_new))
        return new_ms, alphas

    def pair(i, carry):
        ms, alphas, accs = carry
        t = j0 + 2 * i
        scores(t + 1, 1)
        accs = weighted_values(jnp.maximum(t - 1, j0), 1, alphas, accs)
        ms, alphas = softmax(0, ms)
        scores(t + 2, 0)
        accs = weighted_values(t, 0, alphas, accs)
        ms, alphas = softmax(1, ms)
        return ms, alphas, accs

    scores(j0, 0)
    for g in groups:
        p_ref[g, 1] = jnp.zeros((TK, nq), bf16)
    init = ([jnp.full((1, nq), -1e30, f32) for _ in groups], [jnp.ones((1, nq), f32) for _ in groups],
            [jnp.zeros((VROWS, nq), f32) for _ in groups])
    ms, alphas, accs = lax.fori_loop(0, n_pairs, pair, init)
    last = NT - 1
    accs = weighted_values(jnp.maximum(last - 1, j0), 1, alphas, accs)
    ms, alphas = softmax(0, ms)
    accs = weighted_values(last, 0, alphas, accs)
    for g in groups:
        acc_ref[g] = accs[g]


def _attention(aq, ak, avt, n_tiles):
    nq = ATT_GROUP * TM
    akf = ak.astype(f32)
    k_norm = jnp.sqrt(jnp.max(jnp.sum(akf * akf, axis=-1), axis=-1))
    k_norm = jnp.broadcast_to(k_norm[:, :, None, None], (B, ATT_KV_HEADS, 8, 128))
    return pl.pallas_call(
        _attn_kernel,
        grid=(B, n_tiles),
        in_specs=[pl.BlockSpec((1, ATT_Q_HEADS, HD, TM), lambda b, i: (b, 0, 0, i)),
                  pl.BlockSpec((1, ATT_KV_HEADS, T, HD), lambda b, i: (b, 0, 0, 0)),
                  pl.BlockSpec((1, ATT_KV_HEADS, NT, VROWS, TK), lambda b, i: (b, 0, 0, 0, 0)),
                  pl.BlockSpec((1, ATT_KV_HEADS, 8, 128), lambda b, i: (b, 0, 0, 0))],
        out_specs=pl.BlockSpec((1, TM, ATT_Q_HEADS * HD), lambda b, i: (b, i, 0)),
        out_shape=jax.ShapeDtypeStruct((B, n_tiles * TM, ATT_Q_HEADS * HD), bf16),
        scratch_shapes=[pltpu.VMEM((ATT_KV_HEADS, HD, nq), bf16),
                        pltpu.VMEM((ATT_KV_HEADS, 2, TK, nq), f32), pltpu.VMEM((ATT_KV_HEADS, 2, TK, nq), bf16),
                        pltpu.VMEM((ATT_KV_HEADS, VROWS, nq), f32)],
        compiler_params=_cparams("parallel", "arbitrary"),
        name="attention",
    )(aq, ak, avt, k_norm)


def _ret_state_update(s_ref, g_ref, k, v, zeta):
    kzt = (k.astype(f32) * zeta).T
    for h in range(RET_HEADS):
        rows = slice(h * HD, (h + 1) * HD)
        upd = _dot(kzt[rows, :].astype(bf16), v[:, rows])
        s_ref[rows, :] = g_ref[rows, :] * s_ref[rows, :] + upd


def _ret_bwd_kernel(k_ref, v_ref, zeta_ref, g_ref, sb_ref, s_ref):
    @pl.when(pl.program_id(1) == 0)
    def _():
        s_ref[...] = jnp.zeros_like(s_ref)

    for c in reversed(range(RET_STEP)):
        rows = slice(c * RET_CHUNK, (c + 1) * RET_CHUNK)
        sb_ref[0, c] = s_ref[...]
        _ret_state_update(s_ref, g_ref, k_ref[0, rows, :], v_ref[0, rows, :], zeta_ref[...])


def _ret_bwd_states(rk, rv, zeta_b, g_b):
    tile = lambda b, t: (b, NT - 1 - t, 0)
    return pl.pallas_call(
        _ret_bwd_kernel,
        grid=(B, NT),
        in_specs=[pl.BlockSpec((1, TM, 256), tile),
                  pl.BlockSpec((1, TM, 256), tile),
                  _const_spec((RET_CHUNK, 256)), _const_spec((256, HD))],
        out_specs=pl.BlockSpec((1, RET_STEP, 256, HD), lambda b, t: (b, NT - 1 - t, 0, 0)),
        out_shape=jax.ShapeDtypeStruct((B, N_CHUNK, 256, HD), f32),
        scratch_shapes=[pltpu.VMEM((256, HD), f32)],
        compiler_params=_cparams("parallel", "arbitrary"),
        name="ret_bwd_states",
    )(rk, rv, zeta_b, g_b)


def _ret_fwd_kernel(q_ref, k_ref, v_ref, gate_ref, sb_ref, dmask_ref, xif_ref, xib_ref, zeta_ref, g_ref,
                    ng_ref, o_ref, s_ref):
    @pl.when(pl.program_id(1) == 0)
    def _():
        s_ref[...] = jnp.zeros_like(s_ref)

    for c in range(RET_STEP):
        rows = slice(c * RET_CHUNK, (c + 1) * RET_CHUNK)
        q = q_ref[0, rows, :]
        k = k_ref[0, rows, :]
        v = v_ref[0, rows, :]
        qf = q.astype(f32)
        q_xf = (qf * xif_ref[...]).astype(bf16)
        q_xb = (qf * xib_ref[...]).astype(bf16)
        outs = []
        for h in range(RET_HEADS):
            cols = slice(h * HD, (h + 1) * HD)
            s = lax.dot_general(q[:, cols], k[:, cols], (((1,), (1,)), ((), ())), preferred_element_type=f32)
            att = (s * dmask_ref[h]).astype(bf16)
            o = (_dot(att, v[:, cols])
                 + _dot(q_xf[:, cols], s_ref[cols, :].astype(bf16))
                 + _dot(q_xb[:, cols], sb_ref[0, c, cols, :].astype(bf16)))
            ms = jnp.mean(o * o, axis=-1, keepdims=True)
            outs.append(o * lax.rsqrt(ms + EPS))
        y = jnp.concatenate(outs, axis=-1) * ng_ref[...]
        o_ref[0, rows, :] = (y * _silu(gate_ref[0, rows, :].astype(f32))).astype(bf16)
        _ret_state_update(s_ref, g_ref, k, v, zeta_ref[...])


def _retention(rq, rk, rv, rg, sb, dmask, xi_f, xi_b, zeta_f, g_f, ng):
    tile = lambda b, t: (b, (t + NL) % NT, 0)
    blk = pl.BlockSpec((1, TM, 256), tile)
    return pl.pallas_call(
        _ret_fwd_kernel,
        grid=(B, NT),
        in_specs=[blk, blk, blk, blk,
                  pl.BlockSpec((1, RET_STEP, 256, HD), lambda b, t: (b, (t + NL) % NT, 0, 0)),
                  _const_spec((RET_HEADS, RET_CHUNK, RET_CHUNK)),
                  _const_spec((RET_CHUNK, 256)), _const_spec((RET_CHUNK, 256)), _const_spec((RET_CHUNK, 256)),
                  _const_spec((256, HD)), _const_spec((1, 256))],
        out_specs=blk,
        out_shape=jax.ShapeDtypeStruct((B, T, 256), bf16),
        scratch_shapes=[pltpu.VMEM((256, HD), f32)],
        compiler_params=_cparams("parallel", "arbitrary"),
        name="retention",
    )(rq, rk, rv, rg, sb, dmask, xi_f, xi_b, zeta_f, g_f, ng)


def _ret_tables(dec_f, dec_b):
    lg_f = jax.nn.log_sigmoid(dec_f.astype(f32))
    lg_b = jax.nn.log_sigmoid(dec_b.astype(f32))
    pos = jnp.arange(RET_CHUNK, dtype=f32)
    diff = pos[:, None] - pos[None, :]
    d_f = jnp.where(diff[None] >= 0.0, jnp.exp(jnp.maximum(diff, 0.0)[None] * lg_f[:, None, None]), 0.0)
    d_b = jnp.where(diff[None] <= 0.0, jnp.exp(jnp.maximum(-diff, 0.0)[None] * lg_b[:, None, None]), 0.0)
    lanes = lambda t: jnp.repeat(t, HD, axis=1)
    xi_f = lanes(jnp.exp((pos[:, None] + 1.0) * lg_f[None, :]))
    xi_b = lanes(jnp.exp((RET_CHUNK - pos[:, None]) * lg_b[None, :]))
    zeta_f = lanes(jnp.exp((RET_CHUNK - 1.0 - pos[:, None]) * lg_f[None, :]))
    zeta_b = lanes(jnp.exp(pos[:, None] * lg_b[None, :]))
    rows = lambda t: jnp.broadcast_to(jnp.repeat(t, HD)[:, None], (RET_HEADS * HD, HD))
    g_f = rows(jnp.exp(RET_CHUNK * lg_f))
    g_b = rows(jnp.exp(RET_CHUNK * lg_b))
    return d_f + d_b, xi_f, xi_b, zeta_f, zeta_b, g_f, g_b


def _conv_kernel(prev_ref, cur_ref, next_ref, w_ref, b_ref, lg_ref, lb_ref, o_ref, win_ref, sh_ref):
    i = pl.program_id(1)

    def glu(u):
        u = u.astype(f32)
        return u[:, :CONV_W] * jax.nn.sigmoid(u[:, CONV_W:])

    has_prev = jnp.logical_and(i >= 1, i < NL).astype(f32)
    has_next = (i < NL - 1).astype(f32)
    win_ref[0:HALO, :] = glu(prev_ref[0, TM - HALO:TM, :]) * has_prev
    win_ref[HALO:HALO + TM, :] = glu(cur_ref[0])
    win_ref[HALO + TM:2 * HALO + TM, :] = glu(next_ref[0, 0:HALO, :]) * has_next
    sh_rows = sh_ref.shape[1]
    for j in range(8):
        sh_ref[j] = win_ref[j:j + sh_rows, :]
    acc = jnp.zeros((TM, CONV_W), f32)
    off = HALO - CONV_K // 2
    for k in range(CONV_K):
        a, j = divmod(off + k, 8)
        acc = acc + w_ref[k:k + 1, :] * sh_ref[j, 8 * a:8 * a + TM, :]
    y = acc + b_ref[...]
    mu = jnp.mean(y, axis=-1, keepdims=True)
    var = jnp.mean(jnp.square(y - mu), axis=-1, keepdims=True)
    y = (y - mu) * lax.rsqrt(var + EPS) * lg_ref[...] + lb_ref[...]
    o_ref[0] = _silu(y).astype(bf16)


def _conv(cv, dw_w, dw_b, ln_g, ln_b, n_tiles):
    return pl.pallas_call(
        _conv_kernel,
        grid=(B, n_tiles),
        in_specs=[pl.BlockSpec((1, TM, 512), lambda b, i: (b, jnp.maximum(i - 1, 0), 0)),
                  pl.BlockSpec((1, TM, 512), lambda b, i: (b, i, 0)),
                  pl.BlockSpec((1, TM, 512), lambda b, i: (b, jnp.minimum(i + 1, NT - 1), 0)),
                  _const_spec((CONV_K, CONV_W)), _const_spec((1, CONV_W)),
                  _const_spec((1, CONV_W)), _const_spec((1, CONV_W))],
        out_specs=pl.BlockSpec((1, TM, CONV_W), lambda b, i: (b, i, 0)),
        out_shape=jax.ShapeDtypeStruct((B, n_tiles * TM, CONV_W), bf16),
        scratch_shapes=[pltpu.VMEM((TM + 2 * HALO, CONV_W), f32),
                        pltpu.VMEM((8, TM + 2 * HALO - 8, CONV_W), f32)],
        compiler_params=_cparams("parallel", "parallel"),
        name="conv",
    )(cv, cv, cv, dw_w, dw_b, ln_g, ln_b)


def _fn_stage1_kernel(x_ref, g_ref, a_ref):
    for s in range(FN_STEP):
        res = _dot(g_ref[s], x_ref[0, :, s * 256:(s + 1) * 256])
        a_ref[0, 0, s] = res[:FN2].astype(bf16)
        a_ref[0, 1, s] = res[FN2:].astype(bf16)


def _fn_stage2_kernel(a_ref, m_ref, p_ref):
    res = _dot(m_ref[...], a_ref[0])
    p_ref[0, 0, :FN1] = res[:FN1].astype(bf16)
    p_ref[0, 1, :FN1] = res[FN1:].astype(bf16)
    if p_ref.shape[2] > FN1:
        p_ref[0, :, FN1:] = jnp.zeros((2, p_ref.shape[2] - FN1, p_ref.shape[3]), bf16)


def _fn_ctx_kernel(u_ref, f_ref, _p_in, p_ref):
    res = _dot(f_ref[...], u_ref[0])
    p_ref[0, 0] = res[:C].astype(bf16)
    p_ref[0, 1] = res[C:].astype(bf16)


def _fnet_dft(fn, g_tab, m2, fc, with_ctx):
    p_rows = T if with_ctx else S
    x = fn.reshape(B, T // FN1, FN1 * 256)
    a = pl.pallas_call(
        _fn_stage1_kernel,
        grid=(B, FN1 // FN_STEP),
        in_specs=[pl.BlockSpec((1, FN2, FN_STEP * 256), lambda b, j: (b, 0, j)),
                  pl.BlockSpec((FN_STEP, 2 * FN2, FN2), lambda b, j: (j, 0, 0))],
        out_specs=pl.BlockSpec((1, 2, FN_STEP, FN2, 256), lambda b, j: (b, 0, j, 0, 0)),
        out_shape=jax.ShapeDtypeStruct((B, 2, FN1, FN2, 256), bf16),
        compiler_params=_cparams("parallel", "parallel"),
        name="fnet_stage1",
    )(x, g_tab)
    a = a.reshape(B, 2 * FN1, FN2 * 256)
    p = pl.pallas_call(
        _fn_stage2_kernel,
        grid=(B, FN2 * 256 // FN_COLS),
        in_specs=[pl.BlockSpec((1, 2 * FN1, FN_COLS), lambda b, j: (b, 0, j)),
                  _const_spec((2 * FN1, 2 * FN1))],
        out_specs=pl.BlockSpec((1, 2, p_rows // FN2, FN_COLS), lambda b, j: (b, 0, 0, j)),
        out_shape=jax.ShapeDtypeStruct((B, 2, p_rows // FN2, FN2 * 256), bf16),
        compiler_params=_cparams("parallel", "parallel"),
        name="fnet_stage2",
    )(a, m2)
    p = p.reshape(B, 2, p_rows, 256)
    if with_ctx:
        p = pl.pallas_call(
            _fn_ctx_kernel,
            grid=(B,),
            in_specs=[pl.BlockSpec((1, C, 256), lambda b: (b, NL, 0)),
                      _const_spec((2 * C, C)),
                      pl.BlockSpec(memory_space=pl.ANY)],
            out_specs=pl.BlockSpec((1, 2, C, 256), lambda b: (b, 0, NL, 0)),
            out_shape=jax.ShapeDtypeStruct((B, 2, T, 256), bf16),
            input_output_aliases={2: 0},
            compiler_params=_cparams("parallel"),
            name="fnet_ctx",
        )(fn, fc, p)
    return p


def _merge_kernel(lat_ref, ctx_ref, gate_ref, p_ref, r_ref, o_ref, cv_ref, gt_ref, cbd_ref, sbd_ref,
                  fw_ref, rw_ref, aw_ref, cw_ref, ow_ref, out_ref):
    i = pl.program_id(1)
    scale = jnp.where(i == NL, (HD * C) ** -0.5, (HD * S) ** -0.5)
    yfn = (_dot(p_ref[0, 0], cbd_ref[...]) + _dot(p_ref[0, 1], sbd_ref[...])) * scale
    branches = (
        (yfn.astype(bf16), fw_ref),
        (r_ref[0], rw_ref),
        (o_ref[0], aw_ref),
        (cv_ref[0], cw_ref),
    )
    merged = jnp.zeros((TM, D), f32)
    for j, (xin, w_ref) in enumerate(branches):
        g = jax.nn.sigmoid(gt_ref[0, :, j * D:(j + 1) * D].astype(f32))
        merged = merged + g * _dot(xin, w_ref[...])
    out_ref[0] = (_stream_tile(lat_ref, ctx_ref)
                  + gate_ref[0] * _dot(merged.astype(bf16), ow_ref[...]))


def _merge(h_lat, h_ctx, ctx_tile, mod, p, r, o, cvo, gt, cbd, sbd, fw, rw, aw, cw, ow, n_tiles):
    tile = lambda w: pl.BlockSpec((1, TM, w), lambda b, i: (b, i, 0))
    return pl.pallas_call(
        _merge_kernel,
        grid=(B, n_tiles),
        in_specs=_stream_specs(ctx_tile) + [
            _mod_spec(2),
            pl.BlockSpec((1, 2, TM, 256), lambda b, i: (b, 0, i, 0)),
            tile(256), tile(512), tile(256), tile(4 * D),
            _const_spec((256, 256)), _const_spec((256, 256)),
            _const_spec((256, D)), _const_spec((256, D)), _const_spec((512, D)),
            _const_spec((256, D)), _const_spec((D, D))],
        out_specs=tile(D),
        out_shape=jax.ShapeDtypeStruct((B, n_tiles * TM, D), f32),
        compiler_params=_cparams("parallel", "parallel"),
        name="merge",
    )(h_lat, h_ctx, mod, p, r, o, cvo, gt, cbd, sbd, fw, rw, aw, cw, ow)


def _swiglu_rows(a, wg_ref, wu_ref, wd_ref, h_ref):
    for c in range(FFN // FFN_CHUNK):
        cols = slice(c * FFN_CHUNK, (c + 1) * FFN_CHUNK)
        h_ref[:, cols] = (_silu(_dot(a, wg_ref[:, cols])) * _dot(a, wu_ref[:, cols])).astype(bf16)
    return _dot(h_ref[...], wd_ref[...])


def _ffn_kernel(h_ref, sh_ref, sc_ref, gate_ref, g_ref, wg_ref, wu_ref, wd_ref, out_ref, hid_ref):
    x = h_ref[0]
    a = _norm_mod(x, g_ref[...], sh_ref[0], sc_ref[0]).astype(bf16)
    out_ref[0] = x + gate_ref[0] * _swiglu_rows(a, wg_ref, wu_ref, wd_ref, hid_ref)


def _ffn(h, mod, g2, wg, wu, wd):
    tile = pl.BlockSpec((1, TM, D), lambda b, i: (b, i, 0))
    return pl.pallas_call(
        _ffn_kernel,
        grid=(B, NT),
        in_specs=[tile, _mod_spec(3), _mod_spec(4), _mod_spec(5), _const_spec((1, D)),
                  _const_spec((D, FFN)), _const_spec((D, FFN)), _const_spec((FFN, D))],
        out_specs=tile,
        out_shape=jax.ShapeDtypeStruct((B, T, D), f32),
        scratch_shapes=[pltpu.VMEM((TM, FFN), bf16)],
        compiler_params=_cparams("parallel", "parallel"),
        name="ffn",
    )(h, mod, mod, mod, g2, wg, wu, wd)


def _moe_prep_kernel(h_ref, sh_ref, sc_ref, g_ref, rw_ref, x_ref, e_ref, w_ref):
    f = _norm_mod(h_ref[0], g_ref[...], sh_ref[0], sc_ref[0])
    x_ref[...] = f
    rw = rw_ref[...]
    f_hi = f.astype(bf16)
    f_lo = (f - f_hi.astype(f32)).astype(bf16)
    w_hi = rw.astype(bf16)
    w_lo = (rw - w_hi.astype(f32)).astype(bf16)
    logits = (_dot(f_lo, w_lo) + _dot(f_lo, w_hi)) + (_dot(f_hi, w_lo) + _dot(f_hi, w_hi))
    lane = lax.broadcasted_iota(jnp.int32, logits.shape, 1)
    neg = jnp.float32(-jnp.inf)
    logits = jnp.where(lane < N_EXPERTS, logits, neg)
    m1 = jnp.max(logits, axis=-1, keepdims=True)
    i1 = jnp.min(jnp.where(logits == m1, lane, 128), axis=-1, keepdims=True)
    rest = jnp.where(lane == i1, neg, logits)
    m2 = jnp.max(rest, axis=-1, keepdims=True)
    i2 = jnp.min(jnp.where(rest == m2, lane, 128), axis=-1, keepdims=True)
    z = jnp.exp(m2 - m1)
    w1 = 1.0 / (1.0 + z)
    w2 = z / (1.0 + z)
    e_ref[...] = jnp.where(lane == 0, i1, jnp.where(lane == 1, i2, 0))
    w_ref[...] = jnp.where(lane == 0, w1, jnp.where(lane == 1, w2, 0.0))


def _moe_prep(h, mod, g2, router_pad):
    n_steps = B * NL

    def mod_spec(k):
        return pl.BlockSpec((1, 1, D), lambda t: ((t // NL) * 6 + k, 0, 0))

    return pl.pallas_call(
        _moe_prep_kernel,
        grid=(n_steps,),
        in_specs=[pl.BlockSpec((1, TM, D), lambda t: (t // NL, t % NL, 0)),
                  mod_spec(3), mod_spec(4), _const_spec((1, D)), _const_spec((D, 128))],
        out_specs=[pl.BlockSpec((TM, D), lambda t: (t, 0)),
                   pl.BlockSpec((TM, 128), lambda t: (t, 0)),
                   pl.BlockSpec((TM, 128), lambda t: (t, 0))],
        out_shape=[jax.ShapeDtypeStruct((N_TOK, D), f32),
                   jax.ShapeDtypeStruct((N_TOK, 128), jnp.int32),
                   jax.ShapeDtypeStruct((N_TOK, 128), f32)],
        compiler_params=_cparams("parallel"),
        name="moe_prep",
    )(h, mod, mod, g2, router_pad)


def _moe_scatter_kernel(nvalid_ref, dest_ref, x_ref, xb_hbm, zero_ref, sem):
    def zero_copy(slot):
        return pltpu.make_async_copy(zero_ref.at[pl.ds(0, 1), :], xb_hbm.at[pl.ds(slot, 1), :], sem.at[2])

    @pl.when(pl.program_id(0) == 0)
    def _():
        zero_ref[...] = jnp.zeros_like(zero_ref)

        def pad_rows(start):
            def each_block(blk, carry):
                def each_row(r, c):
                    copy = zero_copy(blk * MOE_BLOCK + r)
                    copy.start() if start else copy.wait()
                    return c

                return lax.fori_loop(nvalid_ref[blk], MOE_BLOCK, each_row, carry)

            lax.fori_loop(0, N_BLOCKS, each_block, 0)

        pad_rows(True)
        pad_rows(False)

    def issue(r, carry):
        src = x_ref.at[pl.ds(r, 1), :]
        pltpu.make_async_copy(src, xb_hbm.at[pl.ds(dest_ref[0, 0, 2 * r], 1), :], sem.at[0]).start()
        pltpu.make_async_copy(src, xb_hbm.at[pl.ds(dest_ref[0, 0, 2 * r + 1], 1), :], sem.at[1]).start()
        return carry

    lax.fori_loop(0, TM, issue, 0, unroll=DMA_UNROLL)
    for k in range(2):
        pltpu.make_async_copy(x_ref, xb_hbm.at[pl.ds(0, TM), :], sem.at[k]).wait()


def _moe_scatter(nvalid, dest, xt):
    grid_spec = pltpu.PrefetchScalarGridSpec(
        num_scalar_prefetch=1,
        grid=(N_TOK // TM,),
        in_specs=[pl.BlockSpec((1, 1, 2 * TM), lambda i, nv: (i, 0, 0), memory_space=pltpu.SMEM),
                  pl.BlockSpec((TM, D), lambda i, nv: (i, 0))],
        out_specs=pl.BlockSpec(memory_space=pl.ANY),
        scratch_shapes=[pltpu.VMEM((8, D), f32), pltpu.SemaphoreType.DMA((3,))],
    )
    return pl.pallas_call(
        _moe_scatter_kernel,
        grid_spec=grid_spec,
        out_shape=jax.ShapeDtypeStruct((N_ROWS, D), f32),
        compiler_params=_cparams("arbitrary"),
        name="moe_scatter",
    )(nvalid, dest.reshape(N_TOK // TM, 1, 2 * TM), xt)


def _moe_expert_kernel(be_ref, x_ref, wg_ref, wu_ref, wd_ref, y_ref, hid_ref):
    a = x_ref[...].astype(bf16)
    y_ref[...] = _swiglu_rows(a, wg_ref.at[0], wu_ref.at[0], wd_ref.at[0], hid_ref)


def _moe_experts(blk_e, xb, wg, wu, wd):
    grid_spec = pltpu.PrefetchScalarGridSpec(
        num_scalar_prefetch=1,
        grid=(N_BLOCKS,),
        in_specs=[pl.BlockSpec((MOE_BLOCK, D), lambda i, be: (i, 0)),
                  pl.BlockSpec((1, D, FFN), lambda i, be: (be[i], 0, 0)),
                  pl.BlockSpec((1, D, FFN), lambda i, be: (be[i], 0, 0)),
                  pl.BlockSpec((1, FFN, D), lambda i, be: (be[i], 0, 0))],
        out_specs=pl.BlockSpec((MOE_BLOCK, D), lambda i, be: (i, 0)),
        scratch_shapes=[pltpu.VMEM((MOE_BLOCK, FFN), bf16)],
    )
    return pl.pallas_call(
        _moe_expert_kernel,
        grid_spec=grid_spec,
        out_shape=jax.ShapeDtypeStruct((N_ROWS, D), f32),
        compiler_params=_cparams("arbitrary"),
        name="moe_experts",
    )(blk_e, xb, wg, wu, wd)


def _moe_combine_kernel(dest_ref, h_ref, gate_ref, w_ref, y_hbm, out_ref, y0_ref, y1_ref, sem):
    def issue(r, carry):
        pltpu.make_async_copy(y_hbm.at[pl.ds(dest_ref[0, 0, 2 * r], 1), :], y0_ref.at[pl.ds(r, 1), :],
                              sem.at[0]).start()
        pltpu.make_async_copy(y_hbm.at[pl.ds(dest_ref[0, 0, 2 * r + 1], 1), :], y1_ref.at[pl.ds(r, 1), :],
                              sem.at[1]).start()
        return carry

    lax.fori_loop(0, CMB, issue, 0, unroll=DMA_UNROLL)
    pltpu.make_async_copy(y_hbm.at[pl.ds(0, CMB), :], y0_ref, sem.at[0]).wait()
    pltpu.make_async_copy(y_hbm.at[pl.ds(0, CMB), :], y1_ref, sem.at[1]).wait()
    w = w_ref[...]
    y = w[:, 0:1] * y0_ref[...] + w[:, 1:2] * y1_ref[...]
    out_ref[0] = h_ref[0] + gate_ref[0] * y


def _moe_combine(dest, h, mod, w_pad, yb):
    n_per_b = S // CMB
    return pl.pallas_call(
        _moe_combine_kernel,
        grid=(B, n_per_b),
        in_specs=[pl.BlockSpec((1, 1, 2 * CMB), lambda b, i: (b * n_per_b + i, 0, 0), memory_space=pltpu.SMEM),
                  pl.BlockSpec((1, CMB, D), lambda b, i: (b, i, 0)),
                  pl.BlockSpec((1, 1, D), lambda b, i: (b * 6 + 5, 0, 0)),
                  pl.BlockSpec((CMB, 128), lambda b, i: (b * n_per_b + i, 0)),
                  pl.BlockSpec(memory_space=pl.ANY)],
        out_specs=pl.BlockSpec((1, CMB, D), lambda b, i: (b, i, 0)),
        out_shape=jax.ShapeDtypeStruct((B, S, D), f32),
        scratch_shapes=[pltpu.VMEM((CMB, D), f32), pltpu.VMEM((CMB, D), f32), pltpu.SemaphoreType.DMA((2,))],
        compiler_params=_cparams("arbitrary", "arbitrary"),
        name="moe_combine",
    )(dest.reshape(N_TOK // CMB, 1, 2 * CMB), h, mod, w_pad, yb)


def _moe_routing(top_e):
    e = top_e.reshape(-1)
    onehot = (e[:, None] == jnp.arange(N_EXPERTS, dtype=jnp.int32)[None, :]).astype(jnp.int32)
    counts = jnp.sum(onehot, axis=0)
    rank = jnp.sum((jnp.cumsum(onehot, axis=0) - 1) * onehot, axis=1)
    padded = (counts + MOE_BLOCK - 1) // MOE_BLOCK * MOE_BLOCK
    pad_end = jnp.cumsum(padded)
    pad_start = pad_end - padded
    dest = (pad_start[e] + rank).astype(jnp.int32)
    blk_start = jnp.arange(N_BLOCKS, dtype=jnp.int32) * MOE_BLOCK
    blk_e = jnp.minimum(jnp.sum((pad_end[None, :] <= blk_start[:, None]).astype(jnp.int32), axis=1),
                        N_EXPERTS - 1).astype(jnp.int32)
    nvalid = jnp.clip(pad_start[blk_e] + counts[blk_e] - blk_start, 0, MOE_BLOCK).astype(jnp.int32)
    nvalid = jnp.where(blk_start < pad_end[N_EXPERTS - 1], nvalid, 0)
    return dest, blk_e, nvalid


def _moe(h, mod, g2, router_w, wg, wu, wd):
    router_pad = jnp.pad(router_w, ((0, 0), (0, 128 - N_EXPERTS)))
    xt, e_pad, w_pad = _moe_prep(h, mod, g2, router_pad)
    dest, blk_e, nvalid = _moe_routing(e_pad[:, :2])
    xb = _moe_scatter(nvalid, dest, xt)
    yb = _moe_experts(blk_e, xb, wg, wu, wd)
    return _moe_combine(dest, h, mod, w_pad, yb)


def _token_mixers(h_lat, h_ctx, ctx_tile, mod, i, with_ctx, tabs, norm1_g, w_in, fnet_w, ret_decay_fwd,
                  ret_decay_bwd, ret_norm_g, ret_w, attn_qn_g, attn_kn_g, attn_w, conv_dw_w, conv_dw_b, conv_ln_g,
                  conv_ln_b, conv_w_out, w_out):
    cos128, sin128, g_tab, m2, fc, cbd, sbd, bd = tabs
    n_tiles = NT if with_ctx else NL
    qg = jnp.tile(attn_qn_g[i].astype(f32), ATT_Q_HEADS)[None, :]
    kg = jnp.tile(attn_kn_g[i].astype(f32), ATT_KV_HEADS)[None, :]
    fn, rq, rk, rv, rg, aq, ak, avt, cv, gt = _proj_in(
        h_lat, h_ctx, ctx_tile, mod, norm1_g[i][None, :], w_in[i].astype(bf16), cos128, sin128, qg, kg, bd)

    o = _attention(aq, ak, avt, n_tiles)

    dmask, xi_f, xi_b, zeta_f, zeta_b, g_f, g_b = _ret_tables(ret_decay_fwd[i], ret_decay_bwd[i])
    sb = _ret_bwd_states(rk, rv, zeta_b, g_b)
    r = _retention(rq, rk, rv, rg, sb, dmask, xi_f, xi_b, zeta_f, g_f, ret_norm_g[i][None, :].astype(f32))

    cvo = _conv(cv, conv_dw_w[i], conv_dw_b[i][None, :], conv_ln_g[i][None, :], conv_ln_b[i][None, :], n_tiles)
    p = _fnet_dft(fn, g_tab, m2, fc, with_ctx)
    return _merge(h_lat, h_ctx, ctx_tile, mod, p, r, o, cvo, gt, cbd, sbd, fnet_w[i].astype(bf16),
                  ret_w[i].astype(bf16), attn_w[i].astype(bf16), conv_w_out[i].astype(bf16),
                  w_out[i].astype(bf16), n_tiles)


def kernel(x, c, ctx, c_ctx, ada_w, ada_b, norm1_g, norm2_g, w_in, fnet_w, ret_decay_fwd, ret_decay_bwd, ret_norm_g, ret_w, attn_qn_g, attn_kn_g, attn_w, conv_dw_w, conv_dw_b, conv_ln_g, conv_ln_b, conv_w_out, w_out, ffn_w_gate, ffn_w_up, ffn_w_down, router_w, moe_w_gate, moe_w_up, moe_w_down):
    cos128, sin128 = _rope_tables()
    as_bf16 = lambda t: jnp.asarray(t, f32).astype(bf16)
    tabs = (cos128, sin128, as_bf16(_G_NP), as_bf16(_M2_NP), as_bf16(_FC_NP), as_bf16(_CBD_NP),
            as_bf16(_SBD_NP), as_bf16(_BD_NP))
    c_rows = jnp.concatenate([c, c_ctx[None, :], jnp.zeros((8 - B - 1, D), f32)], axis=0)
    mods = _ada(c_rows, ada_w, ada_b).reshape(DEPTH, 8 * 6, 1, D)
    h = None
    for i in range(DEPTH):
        with_ctx = i < DEPTH - 1
        mod = mods[i]
        stream = (x, ctx, 0) if i == 0 else (h, h, NL)
        h = _token_mixers(*stream, mod, i, with_ctx, tabs, norm1_g, w_in, fnet_w, ret_decay_fwd, ret_decay_bwd,
                          ret_norm_g, ret_w, attn_qn_g, attn_kn_g, attn_w, conv_dw_w, conv_dw_b, conv_ln_g,
                          conv_ln_b, conv_w_out, w_out)
        j = i // 2
        g2 = norm2_g[i][None, :]
        if i % 2 == 0:
            h = _ffn(h, mod, g2, ffn_w_gate[j].astype(bf16), ffn_w_up[j].astype(bf16),
                     ffn_w_down[j].astype(bf16))
        else:
            h = _moe(h, mod, g2, router_w[j], moe_w_gate[j].astype(bf16), moe_w_up[j].astype(bf16),
                     moe_w_down[j].astype(bf16))
    return h
```

```python
import functools
import math

import numpy as np
import jax
import jax.numpy as jnp
from jax import lax
from jax.experimental import pallas as pl
from jax.experimental.pallas import tpu as pltpu

f32 = jnp.float32
bf16 = jnp.bfloat16

D = 1024
B = 2
S = 8192
C = 256
T = S + C
DEPTH = 2
GRID_W = 64
HD = 64
EPS = 1e-6
RET_HEADS = 4
RET_CHUNK = 128
N_CHUNK = T // RET_CHUNK
ATT_Q_HEADS = 8
ATT_KV_HEADS = 2
ATT_GROUP = 4
CONV_K = 31
CONV_W = 256
HALO = 16
IN_COLS = 6656
FFN = 2816
FFN_CHUNK = 256
N_EXPERTS = 8
MOE_BLOCK = 128
ROPE_THETA = 10000.0

TM = 256
NT = T // TM
NL = S // TM
RET_STEP = TM // RET_CHUNK
TK = 256
VROWS = HD + 16
LOG2E = math.log2(math.e)
ATT_SAFE_BOUND = 48.0
ATT_UNROLL = 4

N_TOK = B * S
N_ASSIGN = N_TOK * 2
N_BLOCKS = N_ASSIGN // MOE_BLOCK + N_EXPERTS
N_ROWS = N_BLOCKS * MOE_BLOCK
CMB = 256
DMA_UNROLL = 8

FN1 = 64
FN2 = 128
FN_STEP = 8
FN_COLS = 4096

VMEM_LIMIT = 56 * 1024 * 1024


def _cparams(*sem):
    return pltpu.CompilerParams(dimension_semantics=sem, vmem_limit_bytes=VMEM_LIMIT)


def _const_spec(shape):
    nd = len(shape)
    return pl.BlockSpec(shape, lambda *_: (0,) * nd, pipeline_mode=pl.Buffered(1))


def _mod_spec(k):
    return pl.BlockSpec((1, 1, D), lambda b, i: (jnp.where(i == NL, 2, b) * 6 + k, 0, 0))


def _dot(a, b):
    return jnp.dot(a, b, preferred_element_type=f32)


def _silu(x):
    return x * _sigmoid(x)


def _sigmoid(x):
    return 0.5 * jnp.tanh(0.5 * x) + 0.5


def _dft_tables():
    n1 = np.arange(FN1)[:, None, None]
    k2 = np.arange(FN2)[None, :, None]
    n2 = np.arange(FN2)[None, None, :]
    ph = 2.0 * np.pi * (((n1 + FN1 * n2) * k2) % S) / S
    g = np.concatenate([np.cos(ph), -np.sin(ph)], axis=1)
    k1 = np.arange(FN1)[:, None]
    m1 = np.arange(FN1)[None, :]
    ph = 2.0 * np.pi * ((k1 * m1) % FN1) / FN1
    c64, s64 = np.cos(ph), np.sin(ph)
    m2 = np.block([[c64, s64], [-s64, c64]])
    kc = np.arange(C)[:, None]
    nc = np.arange(C)[None, :]
    ph = 2.0 * np.pi * ((kc * nc) % C) / C
    fc = np.concatenate([np.cos(ph), -np.sin(ph)], axis=0)
    eye4 = np.eye(4)
    cbd = np.kron(eye4, c64)
    sbd = np.kron(eye4, s64)
    bd = np.kron(np.eye(8), np.ones((HD, HD)))
    return g, m2, fc, cbd, sbd, bd


_G_NP, _M2_NP, _FC_NP, _CBD_NP, _SBD_NP, _BD_NP = _dft_tables()


def _rope_tables():
    rows = S // GRID_W
    row = jnp.broadcast_to(jnp.arange(rows)[:, None], (rows, GRID_W)).reshape(-1).astype(f32)
    col = jnp.broadcast_to(jnp.arange(GRID_W)[None, :], (rows, GRID_W)).reshape(-1).astype(f32)
    n_axis = HD // 4
    inv = ROPE_THETA ** (-jnp.arange(n_axis, dtype=f32) / n_axis)
    ang = jnp.concatenate([row[:, None] * inv, col[:, None] * inv], axis=-1)
    cos, sin = jnp.cos(ang), jnp.sin(ang)
    cos = jnp.concatenate([cos, jnp.ones((C, HD // 2), f32)], axis=0)
    sin = jnp.concatenate([sin, jnp.zeros((C, HD // 2), f32)], axis=0)
    cos128 = jnp.concatenate([cos, cos, cos, cos], axis=-1)
    sin128 = jnp.concatenate([-sin, sin, -sin, sin], axis=-1)
    return cos128, sin128


def _ada_kernel(c_ref, w_ref, b_ref, o_ref):
    s = _silu(c_ref[...])
    o_ref[0] = jnp.dot(s, w_ref[0], preferred_element_type=f32,
                       precision=lax.Precision.HIGHEST) + b_ref[0]


def _ada(c_rows, ada_w, ada_b):
    tn = 1536
    return pl.pallas_call(
        _ada_kernel,
        grid=(DEPTH, 6 * D // tn),
        in_specs=[pl.BlockSpec((8, D), lambda l, j: (0, 0)),
                  pl.BlockSpec((1, D, tn), lambda l, j: (l, 0, j)),
                  pl.BlockSpec((1, 1, tn), lambda l, j: (l, 0, j))],
        out_specs=pl.BlockSpec((1, 8, tn), lambda l, j: (l, 0, j)),
        out_shape=jax.ShapeDtypeStruct((DEPTH, 8, 6 * D), f32),
        compiler_params=_cparams("parallel", "parallel"),
        name="ada",
    )(c_rows, ada_w, ada_b.reshape(DEPTH, 1, 6 * D))


def _norm_mod(x, g, shift, scale):
    ms = jnp.mean(x * x, axis=-1, keepdims=True)
    return (x * lax.rsqrt(ms + EPS) * g) * (1.0 + scale) + shift


def _swap_halves(x):
    w = x.shape[-1]
    lane = lax.broadcasted_iota(jnp.int32, x.shape, 1)
    fwd = pltpu.roll(x, w - HD // 2, 1)
    bwd = pltpu.roll(x, HD // 2, 1)
    return jnp.where((lane % HD) < HD // 2, fwd, bwd)


def _rope(x, cos, sin):
    return x * cos + _swap_halves(x) * sin


def _head_rms(x, gain, bd):
    x2 = x * x
    hi = x2.astype(bf16)
    lo = (x2 - hi.astype(f32)).astype(bf16)
    ss = _dot(hi, bd) + _dot(lo, bd)
    return x * lax.rsqrt(ss * (1.0 / HD) + EPS) * gain


def _stream_specs(ctx_tile):
    return [pl.BlockSpec((1, TM, D), lambda b, i: (b, jnp.minimum(i, NL - 1), 0)),
            pl.BlockSpec((1, TM, D), lambda b, i: (b, ctx_tile, 0))]


def _stream_tile(lat_ref, ctx_ref):
    return jnp.where(pl.program_id(1) == NL, ctx_ref[0], lat_ref[0])


def _proj_in_kernel(lat_ref, ctx_ref, sh_ref, sc_ref, g_ref, w_ref, cos_ref, sin_ref, qg_ref, kg_ref, bd_ref,
                    fn_ref, rq_ref, rk_ref, rv_ref, rg_ref, aq_ref, ak_ref, avt_ref, cv_ref, gt_ref):
    a = _norm_mod(_stream_tile(lat_ref, ctx_ref), g_ref[...], sh_ref[0], sc_ref[0]).astype(bf16)

    def proj(c0, c1):
        return _dot(a, w_ref[:, c0:c1])

    cos = cos_ref[...]
    sin = sin_ref[...]
    cos2 = jnp.concatenate([cos, cos], axis=-1)
    sin2 = jnp.concatenate([sin, sin], axis=-1)
    cos4 = jnp.concatenate([cos2, cos2], axis=-1)
    sin4 = jnp.concatenate([sin2, sin2], axis=-1)

    fn_ref[0] = proj(0, 256).astype(bf16)
    rq_ref[0] = _rope(proj(256, 512), cos2, sin2).astype(bf16)
    rk_ref[0] = (_rope(proj(512, 768), cos2, sin2) * (HD ** -0.5)).astype(bf16)
    rv_ref[0] = proj(768, 1024).astype(bf16)
    rg_ref[0] = proj(1024, 1280).astype(bf16)

    q = _head_rms(proj(1280, 1792), qg_ref[...], bd_ref[...])
    qt = (_rope(q, cos4, sin4) * (HD ** -0.5 * LOG2E)).T
    for h in range(ATT_Q_HEADS):
        aq_ref[0, h] = qt[h * HD:(h + 1) * HD, :].astype(bf16)
    k = _head_rms(proj(1792, 1920), kg_ref[...], bd_ref[:2 * HD, :2 * HD])
    k = _rope(k, cos, sin).astype(bf16)
    vt = proj(1920, 2048).T
    row = lax.broadcasted_iota(jnp.int32, (VROWS - HD, TM), 0)
    ones_row = jnp.where(row == 0, 1.0, 0.0).astype(bf16)
    for g in range(ATT_KV_HEADS):
        ak_ref[0, g] = k[:, g * HD:(g + 1) * HD]
        avt_ref[0, g, 0, :HD] = vt[g * HD:(g + 1) * HD, :].astype(bf16)
        avt_ref[0, g, 0, HD:] = ones_row
    cv_ref[0] = proj(2048, 2560).astype(bf16)
    for j in range(4):
        gt_ref[0, :, j * D:(j + 1) * D] = proj(2560 + j * D, 2560 + (j + 1) * D).astype(bf16)


def _proj_in(h_lat, h_ctx, ctx_tile, mod, g1, w_in, cos128, sin128, qg, kg, bd):
    tile = lambda w: pl.BlockSpec((1, TM, w), lambda b, i: (b, i, 0))
    out_shapes = [
        jax.ShapeDtypeStruct((B, T, 256), bf16),
        jax.ShapeDtypeStruct((B, T, 256), bf16),
        jax.ShapeDtypeStruct((B, T, 256), bf16),
        jax.ShapeDtypeStruct((B, T, 256), bf16),
        jax.ShapeDtypeStruct((B, T, 256), bf16),
        jax.ShapeDtypeStruct((B, ATT_Q_HEADS, HD, T), bf16),
        jax.ShapeDtypeStruct((B, ATT_KV_HEADS, T, HD), bf16),
        jax.ShapeDtypeStruct((B, ATT_KV_HEADS, NT, VROWS, TM), bf16),
        jax.ShapeDtypeStruct((B, T, 512), bf16),
        jax.ShapeDtypeStruct((B, T, 4 * D), bf16),
    ]
    out_specs = [
        tile(256), tile(256), tile(256), tile(256), tile(256),
        pl.BlockSpec((1, ATT_Q_HEADS, HD, TM), lambda b, i: (b, 0, 0, i)),
        pl.BlockSpec((1, ATT_KV_HEADS, TM, HD), lambda b, i: (b, 0, i, 0)),
        pl.BlockSpec((1, ATT_KV_HEADS, 1, VROWS, TM), lambda b, i: (b, 0, i, 0, 0)),
        tile(512), tile(4 * D),
    ]
    return pl.pallas_call(
        _proj_in_kernel,
        grid=(B, NT),
        in_specs=_stream_specs(ctx_tile) + [
            _mod_spec(0), _mod_spec(1), _const_spec((1, D)), _const_spec((D, IN_COLS)),
            pl.BlockSpec((TM, 128), lambda b, i: (i, 0)),
            pl.BlockSpec((TM, 128), lambda b, i: (i, 0)),
            _const_spec((1, 512)), _const_spec((1, 128)), _const_spec((512, 512))],
        out_specs=out_specs,
        out_shape=out_shapes,
        compiler_params=_cparams("parallel", "parallel"),
        name="proj_in",
    )(h_lat, h_ctx, mod, mod, g1, w_in, cos128, sin128, qg, kg, bd)


def _attn_kernel(q_ref, k_ref, v_ref, kn_ref, o_ref, qt_ref, s_ref, p_ref, acc_ref):
    nq = ATT_GROUP * TM
    groups = range(ATT_KV_HEADS)
    last = NT - 1
    is_ctx = pl.program_id(1) == NL
    j0 = jnp.where(is_ctx, NL, 0)
    n_pairs = jnp.where(is_ctx, 0, (NT - 1) // 2)
    for g in groups:
        for h in range(ATT_GROUP):
            qt_ref[g, :, h * TM:(h + 1) * TM] = q_ref[0, g * ATT_GROUP + h]

    bounds = []
    for g in groups:
        qf = qt_ref[g].astype(f32)
        qn = jnp.sqrt(jnp.sum(qf * qf, axis=0, keepdims=True))
        bounds.append(qn * jnp.concatenate([kn_ref[0, g, 0:1, :]] * (nq // 128), axis=-1))
    bound_max = jnp.max(jnp.maximum(bounds[0], bounds[1]))

    def key_block(g, t):
        return k_ref[0, g, pl.ds(pl.multiple_of(t * TK, TK), TK), :]

    @pl.when(bound_max <= ATT_SAFE_BOUND)
    def _():
        def probs(t, slot):
            for g in groups:
                p_ref[g, slot] = jnp.exp2(_dot(key_block(g, t), qt_ref[g]) - bounds[g]).astype(bf16)

        def add_values(t, slot, accs):
            return [accs[g] + _dot(v_ref[0, g, t], p_ref[g, slot]) for g in groups]

        def pair(t, accs):
            probs(t + 1, 1)
            accs = add_values(t, 0, accs)
            probs(t + 2, 0)
            return add_values(t + 1, 1, accs)

        def pairs(i, accs):
            t = j0 + 2 * ATT_UNROLL * i
            for u in range(ATT_UNROLL):
                accs = pair(t + 2 * u, accs)
            return accs

        probs(j0, 0)
        accs = lax.fori_loop(0, n_pairs // ATT_UNROLL, pairs, [jnp.zeros((VROWS, nq), f32) for _ in groups])
        accs = add_values(last, 0, accs)
        for g in groups:
            acc_ref[g] = accs[g]

    @pl.when(bound_max > ATT_SAFE_BOUND)
    def _():
        _attn_online_softmax(k_ref, v_ref, qt_ref, s_ref, p_ref, acc_ref, j0, n_pairs)

    outs = []
    for g in groups:
        ot = acc_ref[g, :HD, :] / acc_ref[g, HD:HD + 1, :]
        outs += [ot[:, h * TM:(h + 1) * TM].T for h in range(ATT_GROUP)]
    o_ref[0] = jnp.concatenate(outs, axis=-1).astype(bf16)


def _attn_online_softmax(k_ref, v_ref, qt_ref, s_ref, p_ref, acc_ref, j0, n_pairs):
    nq = ATT_GROUP * TM
    groups = range(ATT_KV_HEADS)

    def scores(t, slot):
        for g in groups:
            k = k_ref[0, g, pl.ds(pl.multiple_of(t * TK, TK), TK), :]
            s_ref[g, slot] = _dot(k, qt_ref[g])

    def weighted_values(t, slot, alphas, accs):
        return [alphas[g] * accs[g] + _dot(v_ref[0, g, t], p_ref[g, slot]) for g in groups]

    def softmax(slot, ms):
        new_ms, alphas = [], []
        for g in groups:
            s = s_ref[g, slot]
            m_new = jnp.maximum(ms[g], jnp.max(s, axis=0, keepdims=True))
            p_ref[g, slot] = jnp.exp2(s - m_new).astype(bf16)
            new_ms.append(m_new)
            alphas.append(jnp.exp2(ms[g] - m_new))
        return new_ms, alphas

    def pair(i, carry):
        ms, alphas, accs = carry
        t = j0 + 2 * i
        scores(t + 1, 1)
        accs = weighted_values(jnp.maximum(t - 1, j0), 1, alphas, accs)
        ms, alphas = softmax(0, ms)
        scores(t + 2, 0)
        accs = weighted_values(t, 0, alphas, accs)
        ms, alphas = softmax(1, ms)
        return ms, alphas, accs

    scores(j0, 0)
    for g in groups:
        p_ref[g, 1] = jnp.zeros((TK, nq), bf16)
    init = ([jnp.full((1, nq), -1e30, f32) for _ in groups], [jnp.ones((1, nq), f32) for _ in groups],
            [jnp.zeros((VROWS, nq), f32) for _ in groups])
    ms, alphas, accs = lax.fori_loop(0, n_pairs, pair, init)
    last = NT - 1
    accs = weighted_values(jnp.maximum(last - 1, j0), 1, alphas, accs)
    ms, alphas = softmax(0, ms)
    accs = weighted_values(last, 0, alphas, accs)
    for g in groups:
        acc_ref[g] = accs[g]


def _attention(aq, ak, avt, n_tiles):
    nq = ATT_GROUP * TM
    akf = ak.astype(f32)
    k_norm = jnp.sqrt(jnp.max(jnp.sum(akf * akf, axis=-1), axis=-1))
    k_norm = jnp.broadcast_to(k_norm[:, :, None, None], (B, ATT_KV_HEADS, 8, 128))
    return pl.pallas_call(
        _attn_kernel,
        grid=(B, n_tiles),
        in_specs=[pl.BlockSpec((1, ATT_Q_HEADS, HD, TM), lambda b, i: (b, 0, 0, i)),
                  pl.BlockSpec((1, ATT_KV_HEADS, T, HD), lambda b, i: (b, 0, 0, 0)),
                  pl.BlockSpec((1, ATT_KV_HEADS, NT, VROWS, TK), lambda b, i: (b, 0, 0, 0, 0)),
                  pl.BlockSpec((1, ATT_KV_HEADS, 8, 128), lambda b, i: (b, 0, 0, 0))],
        out_specs=pl.BlockSpec((1, TM, ATT_Q_HEADS * HD), lambda b, i: (b, i, 0)),
        out_shape=jax.ShapeDtypeStruct((B, n_tiles * TM, ATT_Q_HEADS * HD), bf16),
        scratch_shapes=[pltpu.VMEM((ATT_KV_HEADS, HD, nq), bf16),
                        pltpu.VMEM((ATT_KV_HEADS, 2, TK, nq), f32), pltpu.VMEM((ATT_KV_HEADS, 2, TK, nq), bf16),
                        pltpu.VMEM((ATT_KV_HEADS, VROWS, nq), f32)],
        compiler_params=_cparams("parallel", "arbitrary"),
        name="attention",
    )(aq, ak, avt, k_norm)


def _ret_state_update(s_ref, g_ref, k, v, zeta):
    kzt = (k.astype(f32) * zeta).T
    for h in range(RET_HEADS):
        rows = slice(h * HD, (h + 1) * HD)
        upd = _dot(kzt[rows, :].astype(bf16), v[:, rows])
        s_ref[rows, :] = g_ref[rows, :] * s_ref[rows, :] + upd


def _ret_bwd_kernel(k_ref, v_ref, zeta_ref, g_ref, sb_ref, s_ref):
    @pl.when(pl.program_id(1) == 0)
    def _():
        s_ref[...] = jnp.zeros_like(s_ref)

    for c in reversed(range(RET_STEP)):
        rows = slice(c * RET_CHUNK, (c + 1) * RET_CHUNK)
        sb_ref[0, c] = s_ref[...]
        _ret_state_update(s_ref, g_ref, k_ref[0, rows, :], v_ref[0, rows, :], zeta_ref[...])


def _ret_bwd_states(rk, rv, zeta_b, g_b):
    tile = lambda b, t: (b, NT - 1 - t, 0)
    return pl.pallas_call(
        _ret_bwd_kernel,
        grid=(B, NT),
        in_specs=[pl.BlockSpec((1, TM, 256), tile),
                  pl.BlockSpec((1, TM, 256), tile),
                  _const_spec((RET_CHUNK, 256)), _const_spec((256, HD))],
        out_specs=pl.BlockSpec((1, RET_STEP, 256, HD), lambda b, t: (b, NT - 1 - t, 0, 0)),
        out_shape=jax.ShapeDtypeStruct((B, N_CHUNK, 256, HD), f32),
        scratch_shapes=[pltpu.VMEM((256, HD), f32)],
        compiler_params=_cparams("parallel", "arbitrary"),
        name="ret_bwd_states",
    )(rk, rv, zeta_b, g_b)


def _ret_fwd_kernel(q_ref, k_ref, v_ref, gate_ref, sb_ref, dmask_ref, xif_ref, xib_ref, zeta_ref, g_ref,
                    ng_ref, o_ref, s_ref):
    @pl.when(pl.program_id(1) == 0)
    def _():
        s_ref[...] = jnp.zeros_like(s_ref)

    for c in range(RET_STEP):
        rows = slice(c * RET_CHUNK, (c + 1) * RET_CHUNK)
        q = q_ref[0, rows, :]
        k = k_ref[0, rows, :]
        v = v_ref[0, rows, :]
        qf = q.astype(f32)
        q_xf = (qf * xif_ref[...]).astype(bf16)
        q_xb = (qf * xib_ref[...]).astype(bf16)
        outs = []
        for h in range(RET_HEADS):
            cols = slice(h * HD, (h + 1) * HD)
            s = lax.dot_general(q[:, cols], k[:, cols], (((1,), (1,)), ((), ())), preferred_element_type=f32)
            att = (s * dmask_ref[h]).astype(bf16)
            o = (_dot(att, v[:, cols])
                 + _dot(q_xf[:, cols], s_ref[cols, :].astype(bf16))
                 + _dot(q_xb[:, cols], sb_ref[0, c, cols, :].astype(bf16)))
            ms = jnp.mean(o * o, axis=-1, keepdims=True)
            outs.append(o * lax.rsqrt(ms + EPS))
        y = jnp.concatenate(outs, axis=-1) * ng_ref[...]
        o_ref[0, rows, :] = (y * _silu(gate_ref[0, rows, :].astype(f32))).astype(bf16)
        _ret_state_update(s_ref, g_ref, k, v, zeta_ref[...])


def _retention(rq, rk, rv, rg, sb, dmask, xi_f, xi_b, zeta_f, g_f, ng):
    tile = lambda b, t: (b, (t + NL) % NT, 0)
    blk = pl.BlockSpec((1, TM, 256), tile)
    return pl.pallas_call(
        _ret_fwd_kernel,
        grid=(B, NT),
        in_specs=[blk, blk, blk, blk,
                  pl.BlockSpec((1, RET_STEP, 256, HD), lambda b, t: (b, (t + NL) % NT, 0, 0)),
                  _const_spec((RET_HEADS, RET_CHUNK, RET_CHUNK)),
                  _const_spec((RET_CHUNK, 256)), _const_spec((RET_CHUNK, 256)), _const_spec((RET_CHUNK, 256)),
                  _const_spec((256, HD)), _const_spec((1, 256))],
        out_specs=blk,
        out_shape=jax.ShapeDtypeStruct((B, T, 256), bf16),
        scratch_shapes=[pltpu.VMEM((256, HD), f32)],
        compiler_params=_cparams("parallel", "arbitrary"),
        name="retention",
    )(rq, rk, rv, rg, sb, dmask, xi_f, xi_b, zeta_f, g_f, ng)


def _ret_tables(dec_f, dec_b):
    lg_f = jax.nn.log_sigmoid(dec_f.astype(f32))
    lg_b = jax.nn.log_sigmoid(dec_b.astype(f32))
    pos = jnp.arange(RET_CHUNK, dtype=f32)
    diff = pos[:, None] - pos[None, :]
    d_f = jnp.where(diff[None] >= 0.0, jnp.exp(jnp.maximum(diff, 0.0)[None] * lg_f[:, None, None]), 0.0)
    d_b = jnp.where(diff[None] <= 0.0, jnp.exp(jnp.maximum(-diff, 0.0)[None] * lg_b[:, None, None]), 0.0)
    lanes = lambda t: jnp.repeat(t, HD, axis=1)
    xi_f = lanes(jnp.exp((pos[:, None] + 1.0) * lg_f[None, :]))
    xi_b = lanes(jnp.exp((RET_CHUNK - pos[:, None]) * lg_b[None, :]))
    zeta_f = lanes(jnp.exp((RET_CHUNK - 1.0 - pos[:, None]) * lg_f[None, :]))
    zeta_b = lanes(jnp.exp(pos[:, None] * lg_b[None, :]))
    rows = lambda t: jnp.broadcast_to(jnp.repeat(t, HD)[:, None], (RET_HEADS * HD, HD))
    g_f = rows(jnp.exp(RET_CHUNK * lg_f))
    g_b = rows(jnp.exp(RET_CHUNK * lg_b))
    return d_f + d_b, xi_f, xi_b, zeta_f, zeta_b, g_f, g_b


def _conv_kernel(prev_ref, cur_ref, next_ref, w_ref, b_ref, lg_ref, lb_ref, o_ref, win_ref, sh_ref):
    i = pl.program_id(1)

    def glu(u):
        u = u.astype(f32)
        return u[:, :CONV_W] * _sigmoid(u[:, CONV_W:])

    has_prev = jnp.logical_and(i >= 1, i < NL).astype(f32)
    has_next = (i < NL - 1).astype(f32)
    win_ref[0:HALO, :] = glu(prev_ref[0, TM - HALO:TM, :]) * has_prev
    win_ref[HALO:HALO + TM, :] = glu(cur_ref[0])
    win_ref[HALO + TM:2 * HALO + TM, :] = glu(next_ref[0, 0:HALO, :]) * has_next
    sh_rows = sh_ref.shape[1]
    for j in range(8):
        sh_ref[j] = win_ref[j:j + sh_rows, :]
    acc = jnp.zeros((TM, CONV_W), f32)
    off = HALO - CONV_K // 2
    for k in range(CONV_K):
        a, j = divmod(off + k, 8)
        acc = acc + w_ref[k:k + 1, :] * sh_ref[j, 8 * a:8 * a + TM, :]
    y = acc + b_ref[...]
    mu = jnp.mean(y, axis=-1, keepdims=True)
    var = jnp.mean(jnp.square(y - mu), axis=-1, keepdims=True)
    y = (y - mu) * lax.rsqrt(var + EPS) * lg_ref[...] + lb_ref[...]
    o_ref[0] = _silu(y).astype(bf16)


def _conv(cv, dw_w, dw_b, ln_g, ln_b, n_tiles):
    return pl.pallas_call(
        _conv_kernel,
        grid=(B, n_tiles),
        in_specs=[pl.BlockSpec((1, TM, 512), lambda b, i: (b, jnp.maximum(i - 1, 0), 0)),
                  pl.BlockSpec((1, TM, 512), lambda b, i: (b, i, 0)),
                  pl.BlockSpec((1, TM, 512), lambda b, i: (b, jnp.minimum(i + 1, NT - 1), 0)),
                  _const_spec((CONV_K, CONV_W)), _const_spec((1, CONV_W)),
                  _const_spec((1, CONV_W)), _const_spec((1, CONV_W))],
        out_specs=pl.BlockSpec((1, TM, CONV_W), lambda b, i: (b, i, 0)),
        out_shape=jax.ShapeDtypeStruct((B, n_tiles * TM, CONV_W), bf16),
        scratch_shapes=[pltpu.VMEM((TM + 2 * HALO, CONV_W), f32),
                        pltpu.VMEM((8, TM + 2 * HALO - 8, CONV_W), f32)],
        compiler_params=_cparams("parallel", "parallel"),
        name="conv",
    )(cv, cv, cv, dw_w, dw_b, ln_g, ln_b)


def _fn_stage1_kernel(x_ref, g_ref, a_ref):
    for s in range(FN_STEP):
        res = _dot(g_ref[s], x_ref[0, :, s * 256:(s + 1) * 256])
        a_ref[0, 0, s] = res[:FN2].astype(bf16)
        a_ref[0, 1, s] = res[FN2:].astype(bf16)


def _fn_stage2_kernel(a_ref, m_ref, p_ref):
    res = _dot(m_ref[...], a_ref[0])
    p_ref[0, 0, :FN1] = res[:FN1].astype(bf16)
    p_ref[0, 1, :FN1] = res[FN1:].astype(bf16)
    if p_ref.shape[2] > FN1:
        p_ref[0, :, FN1:] = jnp.zeros((2, p_ref.shape[2] - FN1, p_ref.shape[3]), bf16)


def _fn_ctx_kernel(u_ref, f_ref, _p_in, p_ref):
    res = _dot(f_ref[...], u_ref[0])
    p_ref[0, 0] = res[:C].astype(bf16)
    p_ref[0, 1] = res[C:].astype(bf16)


def _fnet_dft(fn, g_tab, m2, fc, with_ctx):
    p_rows = T if with_ctx else S
    x = fn.reshape(B, T // FN1, FN1 * 256)
    a = pl.pallas_call(
        _fn_stage1_kernel,
        grid=(B, FN1 // FN_STEP),
        in_specs=[pl.BlockSpec((1, FN2, FN_STEP * 256), lambda b, j: (b, 0, j)),
                  pl.BlockSpec((FN_STEP, 2 * FN2, FN2), lambda b, j: (j, 0, 0))],
        out_specs=pl.BlockSpec((1, 2, FN_STEP, FN2, 256), lambda b, j: (b, 0, j, 0, 0)),
        out_shape=jax.ShapeDtypeStruct((B, 2, FN1, FN2, 256), bf16),
        compiler_params=_cparams("parallel", "parallel"),
        name="fnet_stage1",
    )(x, g_tab)
    a = a.reshape(B, 2 * FN1, FN2 * 256)
    p = pl.pallas_call(
        _fn_stage2_kernel,
        grid=(B, FN2 * 256 // FN_COLS),
        in_specs=[pl.BlockSpec((1, 2 * FN1, FN_COLS), lambda b, j: (b, 0, j)),
                  _const_spec((2 * FN1, 2 * FN1))],
        out_specs=pl.BlockSpec((1, 2, p_rows // FN2, FN_COLS), lambda b, j: (b, 0, 0, j)),
        out_shape=jax.ShapeDtypeStruct((B, 2, p_rows // FN2, FN2 * 256), bf16),
        compiler_params=_cparams("parallel", "parallel"),
        name="fnet_stage2",
    )(a, m2)
    p = p.reshape(B, 2, p_rows, 256)
    if with_ctx:
        p = pl.pallas_call(
            _fn_ctx_kernel,
            grid=(B,),
            in_specs=[pl.BlockSpec((1, C, 256), lambda b: (b, NL, 0)),
                      _const_spec((2 * C, C)),
                      pl.BlockSpec(memory_space=pl.ANY)],
            out_specs=pl.BlockSpec((1, 2, C, 256), lambda b: (b, 0, NL, 0)),
            out_shape=jax.ShapeDtypeStruct((B, 2, T, 256), bf16),
            input_output_aliases={2: 0},
            compiler_params=_cparams("parallel"),
            name="fnet_ctx",
        )(fn, fc, p)
    return p


def _merge_kernel(lat_ref, ctx_ref, gate_ref, p_ref, r_ref, o_ref, cv_ref, gt_ref, cbd_ref, sbd_ref,
                  fw_ref, rw_ref, aw_ref, cw_ref, ow_ref, out_ref):
    i = pl.program_id(1)
    scale = jnp.where(i == NL, (HD * C) ** -0.5, (HD * S) ** -0.5)
    yfn = (_dot(p_ref[0, 0], cbd_ref[...]) + _dot(p_ref[0, 1], sbd_ref[...])) * scale
    branches = (
        (yfn.astype(bf16), fw_ref),
        (r_ref[0], rw_ref),
        (o_ref[0], aw_ref),
        (cv_ref[0], cw_ref),
    )
    merged = jnp.zeros((TM, D), f32)
    for j, (xin, w_ref) in enumerate(branches):
        g = _sigmoid(gt_ref[0, :, j * D:(j + 1) * D].astype(f32))
        merged = merged + g * _dot(xin, w_ref[...])
    out_ref[0] = (_stream_tile(lat_ref, ctx_ref)
                  + gate_ref[0] * _dot(merged.astype(bf16), ow_ref[...]))


def _merge(h_lat, h_ctx, ctx_tile, mod, p, r, o, cvo, gt, cbd, sbd, fw, rw, aw, cw, ow, n_tiles):
    tile = lambda w: pl.BlockSpec((1, TM, w), lambda b, i: (b, i, 0))
    return pl.pallas_call(
        _merge_kernel,
        grid=(B, n_tiles),
        in_specs=_stream_specs(ctx_tile) + [
            _mod_spec(2),
            pl.BlockSpec((1, 2, TM, 256), lambda b, i: (b, 0, i, 0)),
            tile(256), tile(512), tile(256), tile(4 * D),
            _const_spec((256, 256)), _const_spec((256, 256)),
            _const_spec((256, D)), _const_spec((256, D)), _const_spec((512, D)),
            _const_spec((256, D)), _const_spec((D, D))],
        out_specs=tile(D),
        out_shape=jax.ShapeDtypeStruct((B, n_tiles * TM, D), f32),
        compiler_params=_cparams("parallel", "parallel"),
        name="merge",
    )(h_lat, h_ctx, mod, p, r, o, cvo, gt, cbd, sbd, fw, rw, aw, cw, ow)


def _swiglu_rows(a, wg_ref, wu_ref, wd_ref, h_ref):
    for c in range(FFN // FFN_CHUNK):
        cols = slice(c * FFN_CHUNK, (c + 1) * FFN_CHUNK)
        h_ref[:, cols] = (_silu(_dot(a, wg_ref[:, cols])) * _dot(a, wu_ref[:, cols])).astype(bf16)
    return _dot(h_ref[...], wd_ref[...])


def _ffn_kernel(h_ref, sh_ref, sc_ref, gate_ref, g_ref, wg_ref, wu_ref, wd_ref, out_ref, hid_ref):
    x = h_ref[0]
    a = _norm_mod(x, g_ref[...], sh_ref[0], sc_ref[0]).astype(bf16)
    out_ref[0] = x + gate_ref[0] * _swiglu_rows(a, wg_ref, wu_ref, wd_ref, hid_ref)


def _ffn(h, mod, g2, wg, wu, wd):
    tile = pl.BlockSpec((1, TM, D), lambda b, i: (b, i, 0))
    return pl.pallas_call(
        _ffn_kernel,
        grid=(B, NT),
        in_specs=[tile, _mod_spec(3), _mod_spec(4), _mod_spec(5), _const_spec((1, D)),
                  _const_spec((D, FFN)), _const_spec((D, FFN)), _const_spec((FFN, D))],
        out_specs=tile,
        out_shape=jax.ShapeDtypeStruct((B, T, D), f32),
        scratch_shapes=[pltpu.VMEM((TM, FFN), bf16)],
        compiler_params=_cparams("parallel", "parallel"),
        name="ffn",
    )(h, mod, mod, mod, g2, wg, wu, wd)


def _moe_prep_kernel(h_ref, sh_ref, sc_ref, g_ref, rw_ref, x_ref, e_ref, w_ref):
    f = _norm_mod(h_ref[0], g_ref[...], sh_ref[0], sc_ref[0])
    x_ref[...] = f
    rw = rw_ref[...]
    f_hi = f.astype(bf16)
    f_lo = (f - f_hi.astype(f32)).astype(bf16)
    w_hi = rw.astype(bf16)
    w_lo = (rw - w_hi.astype(f32)).astype(bf16)
    logits = (_dot(f_lo, w_lo) + _dot(f_lo, w_hi)) + (_dot(f_hi, w_lo) + _dot(f_hi, w_hi))
    lane = lax.broadcasted_iota(jnp.int32, logits.shape, 1)
    neg = jnp.float32(-jnp.inf)
    logits = jnp.where(lane < N_EXPERTS, logits, neg)
    m1 = jnp.max(logits, axis=-1, keepdims=True)
    i1 = jnp.min(jnp.where(logits == m1, lane, 128), axis=-1, keepdims=True)
    rest = jnp.where(lane == i1, neg, logits)
    m2 = jnp.max(rest, axis=-1, keepdims=True)
    i2 = jnp.min(jnp.where(rest == m2, lane, 128), axis=-1, keepdims=True)
    z = jnp.exp(m2 - m1)
    w1 = 1.0 / (1.0 + z)
    w2 = z / (1.0 + z)
    e_ref[...] = jnp.where(lane == 0, i1, jnp.where(lane == 1, i2, 0))
    w_ref[...] = jnp.where(lane == 0, w1, jnp.where(lane == 1, w2, 0.0))


def _moe_prep(h, mod, g2, router_pad):
    n_steps = B * NL

    def mod_spec(k):
        return pl.BlockSpec((1, 1, D), lambda t: ((t // NL) * 6 + k, 0, 0))

    return pl.pallas_call(
        _moe_prep_kernel,
        grid=(n_steps,),
        in_specs=[pl.BlockSpec((1, TM, D), lambda t: (t // NL, t % NL, 0)),
                  mod_spec(3), mod_spec(4), _const_spec((1, D)), _const_spec((D, 128))],
        out_specs=[pl.BlockSpec((TM, D), lambda t: (t, 0)),
                   pl.BlockSpec((TM, 128), lambda t: (t, 0)),
                   pl.BlockSpec((TM, 128), lambda t: (t, 0))],
        out_shape=[jax.ShapeDtypeStruct((N_TOK, D), f32),
                   jax.ShapeDtypeStruct((N_TOK, 128), jnp.int32),
                   jax.ShapeDtypeStruct((N_TOK, 128), f32)],
        compiler_params=_cparams("parallel"),
        name="moe_prep",
    )(h, mod, mod, g2, router_pad)


def _moe_scatter_kernel(nvalid_ref, dest_ref, x_ref, xb_hbm, zero_ref, sem):
    def zero_copy(slot):
        return pltpu.make_async_copy(zero_ref.at[pl.ds(0, 1), :], xb_hbm.at[pl.ds(slot, 1), :], sem.at[2])

    @pl.when(pl.program_id(0) == 0)
    def _():
        zero_ref[...] = jnp.zeros_like(zero_ref)

        def pad_rows(start):
            def each_block(blk, carry):
                def each_row(r, c):
                    copy = zero_copy(blk * MOE_BLOCK + r)
                    copy.start() if start else copy.wait()
                    return c

                return lax.fori_loop(nvalid_ref[blk], MOE_BLOCK, each_row, carry)

            lax.fori_loop(0, N_BLOCKS, each_block, 0)

        pad_rows(True)
        pad_rows(False)

    def issue(r, carry):
        src = x_ref.at[pl.ds(r, 1), :]
        pltpu.make_async_copy(src, xb_hbm.at[pl.ds(dest_ref[0, 0, 2 * r], 1), :], sem.at[0]).start()
        pltpu.make_async_copy(src, xb_hbm.at[pl.ds(dest_ref[0, 0, 2 * r + 1], 1), :], sem.at[1]).start()
        return carry

    lax.fori_loop(0, TM, issue, 0, unroll=DMA_UNROLL)
    for k in range(2):
        pltpu.make_async_copy(x_ref, xb_hbm.at[pl.ds(0, TM), :], sem.at[k]).wait()


def _moe_scatter(nvalid, dest, xt):
    grid_spec = pltpu.PrefetchScalarGridSpec(
        num_scalar_prefetch=1,
        grid=(N_TOK // TM,),
        in_specs=[pl.BlockSpec((1, 1, 2 * TM), lambda i, nv: (i, 0, 0), memory_space=pltpu.SMEM),
                  pl.BlockSpec((TM, D), lambda i, nv: (i, 0))],
        out_specs=pl.BlockSpec(memory_space=pl.ANY),
        scratch_shapes=[pltpu.VMEM((8, D), f32), pltpu.SemaphoreType.DMA((3,))],
    )
    return pl.pallas_call(
        _moe_scatter_kernel,
        grid_spec=grid_spec,
        out_shape=jax.ShapeDtypeStruct((N_ROWS, D), f32),
        compiler_params=_cparams("arbitrary"),
        name="moe_scatter",
    )(nvalid, dest.reshape(N_TOK // TM, 1, 2 * TM), xt)


def _moe_expert_kernel(be_ref, x_ref, wg_ref, wu_ref, wd_ref, y_ref, hid_ref):
    a = x_ref[...].astype(bf16)
    y_ref[...] = _swiglu_rows(a, wg_ref.at[0], wu_ref.at[0], wd_ref.at[0], hid_ref)


def _moe_experts(blk_e, xb, wg, wu, wd):
    grid_spec = pltpu.PrefetchScalarGridSpec(
        num_scalar_prefetch=1,
        grid=(N_BLOCKS,),
        in_specs=[pl.BlockSpec((MOE_BLOCK, D), lambda i, be: (i, 0)),
                  pl.BlockSpec((1, D, FFN), lambda i, be: (be[i], 0, 0)),
                  pl.BlockSpec((1, D, FFN), lambda i, be: (be[i], 0, 0)),
                  pl.BlockSpec((1, FFN, D), lambda i, be: (be[i], 0, 0))],
        out_specs=pl.BlockSpec((MOE_BLOCK, D), lambda i, be: (i, 0)),
        scratch_shapes=[pltpu.VMEM((MOE_BLOCK, FFN), bf16)],
    )
    return pl.pallas_call(
        _moe_expert_kernel,
        grid_spec=grid_spec,
        out_shape=jax.ShapeDtypeStruct((N_ROWS, D), f32),
        compiler_params=_cparams("arbitrary"),
        name="moe_experts",
    )(blk_e, xb, wg, wu, wd)


def _moe_combine_kernel(dest_ref, h_ref, gate_ref, w_ref, y_hbm, out_ref, y0_ref, y1_ref, sem):
    def issue(r, carry):
        pltpu.make_async_copy(y_hbm.at[pl.ds(dest_ref[0, 0, 2 * r], 1), :], y0_ref.at[pl.ds(r, 1), :],
                              sem.at[0]).start()
        pltpu.make_async_copy(y_hbm.at[pl.ds(dest_ref[0, 0, 2 * r + 1], 1), :], y1_ref.at[pl.ds(r, 1), :],
                              sem.at[1]).start()
        return carry

    lax.fori_loop(0, CMB, issue, 0, unroll=DMA_UNROLL)
    pltpu.make_async_copy(y_hbm.at[pl.ds(0, CMB), :], y0_ref, sem.at[0]).wait()
    pltpu.make_async_copy(y_hbm.at[pl.ds(0, CMB), :], y1_ref, sem.at[1]).wait()
    w = w_ref[...]
    y = w[:, 0:1] * y0_ref[...] + w[:, 1:2] * y1_ref[...]
    out_ref[0] = h_ref[0] + gate_ref[0] * y


def _moe_combine(dest, h, mod, w_pad, yb):
    n_per_b = S // CMB
    return pl.pallas_call(
        _moe_combine_kernel,
        grid=(B, n_per_b),
        in_specs=[pl.BlockSpec((1, 1, 2 * CMB), lambda b, i: (b * n_per_b + i, 0, 0), memory_space=pltpu.SMEM),
                  pl.BlockSpec((1, CMB, D), lambda b, i: (b, i, 0)),
                  pl.BlockSpec((1, 1, D), lambda b, i: (b * 6 + 5, 0, 0)),
                  pl.BlockSpec((CMB, 128), lambda b, i: (b * n_per_b + i, 0)),
                  pl.BlockSpec(memory_space=pl.ANY)],
        out_specs=pl.BlockSpec((1, CMB, D), lambda b, i: (b, i, 0)),
        out_shape=jax.ShapeDtypeStruct((B, S, D), f32),
        scratch_shapes=[pltpu.VMEM((CMB, D), f32), pltpu.VMEM((CMB, D), f32), pltpu.SemaphoreType.DMA((2,))],
        compiler_params=_cparams("arbitrary", "arbitrary"),
        name="moe_combine",
    )(dest.reshape(N_TOK // CMB, 1, 2 * CMB), h, mod, w_pad, yb)


def _moe_routing(top_e):
    e = top_e.reshape(-1)
    onehot = (e[:, None] == jnp.arange(N_EXPERTS, dtype=jnp.int32)[None, :]).astype(jnp.int32)
    counts = jnp.sum(onehot, axis=0)
    rank = jnp.sum((jnp.cumsum(onehot, axis=0) - 1) * onehot, axis=1)
    padded = (counts + MOE_BLOCK - 1) // MOE_BLOCK * MOE_BLOCK
    pad_end = jnp.cumsum(padded)
    pad_start = pad_end - padded
    dest = (pad_start[e] + rank).astype(jnp.int32)
    blk_start = jnp.arange(N_BLOCKS, dtype=jnp.int32) * MOE_BLOCK
    blk_e = jnp.minimum(jnp.sum((pad_end[None, :] <= blk_start[:, None]).astype(jnp.int32), axis=1),
                        N_EXPERTS - 1).astype(jnp.int32)
    nvalid = jnp.clip(pad_start[blk_e] + counts[blk_e] - blk_start, 0, MOE_BLOCK).astype(jnp.int32)
    nvalid = jnp.where(blk_start < pad_end[N_EXPERTS - 1], nvalid, 0)
    return dest, blk_e, nvalid


def _moe(h, mod, g2, router_w, wg, wu, wd):
    router_pad = jnp.pad(router_w, ((0, 0), (0, 128 - N_EXPERTS)))
    xt, e_pad, w_pad = _moe_prep(h, mod, g2, router_pad)
    dest, blk_e, nvalid = _moe_routing(e_pad[:, :2])
    xb = _moe_scatter(nvalid, dest, xt)
    yb = _moe_experts(blk_e, xb, wg, wu, wd)
    return _moe_combine(dest, h, mod, w_pad, yb)


def _token_mixers(h_lat, h_ctx, ctx_tile, mod, i, with_ctx, tabs, norm1_g, w_in, fnet_w, ret_decay_fwd,
                  ret_decay_bwd, ret_norm_g, ret_w, attn_qn_g, attn_kn_g, attn_w, conv_dw_w, conv_dw_b, conv_ln_g,
                  conv_ln_b, conv_w_out, w_out):
    cos128, sin128, g_tab, m2, fc, cbd, sbd, bd = tabs
    n_tiles = NT if with_ctx else NL
    qg = jnp.tile(attn_qn_g[i].astype(f32), ATT_Q_HEADS)[None, :]
    kg = jnp.tile(attn_kn_g[i].astype(f32), ATT_KV_HEADS)[None, :]
    fn, rq, rk, rv, rg, aq, ak, avt, cv, gt = _proj_in(
        h_lat, h_ctx, ctx_tile, mod, norm1_g[i][None, :], w_in[i].astype(bf16), cos128, sin128, qg, kg, bd)

    o = _attention(aq, ak, avt, n_tiles)

    dmask, xi_f, xi_b, zeta_f, zeta_b, g_f, g_b = _ret_tables(ret_decay_fwd[i], ret_decay_bwd[i])
    sb = _ret_bwd_states(rk, rv, zeta_b, g_b)
    r = _retention(rq, rk, rv, rg, sb, dmask, xi_f, xi_b, zeta_f, g_f, ret_norm_g[i][None, :].astype(f32))

    cvo = _conv(cv, conv_dw_w[i], conv_dw_b[i][None, :], conv_ln_g[i][None, :], conv_ln_b[i][None, :], n_tiles)
    p = _fnet_dft(fn, g_tab, m2, fc, with_ctx)
    return _merge(h_lat, h_ctx, ctx_tile, mod, p, r, o, cvo, gt, cbd, sbd, fnet_w[i].astype(bf16),
                  ret_w[i].astype(bf16), attn_w[i].astype(bf16), conv_w_out[i].astype(bf16),
                  w_out[i].astype(bf16), n_tiles)


def kernel(x, c, ctx, c_ctx, ada_w, ada_b, norm1_g, norm2_g, w_in, fnet_w, ret_decay_fwd, ret_decay_bwd, ret_norm_g, ret_w, attn_qn_g, attn_kn_g, attn_w, conv_dw_w, conv_dw_b, conv_ln_g, conv_ln_b, conv_w_out, w_out, ffn_w_gate, ffn_w_up, ffn_w_down, router_w, moe_w_gate, moe_w_up, moe_w_down):
    cos128, sin128 = _rope_tables()
    as_bf16 = lambda t: jnp.asarray(t, f32).astype(bf16)
    tabs = (cos128, sin128, as_bf16(_G_NP), as_bf16(_M2_NP), as_bf16(_FC_NP), as_bf16(_CBD_NP),
            as_bf16(_SBD_NP), as_bf16(_BD_NP))
    c_rows = jnp.concatenate([c, c_ctx[None, :], jnp.zeros((8 - B - 1, D), f32)], axis=0)
    mods = _ada(c_rows, ada_w, ada_b).reshape(DEPTH, 8 * 6, 1, D)
    h = None
    for i in range(DEPTH):
        with_ctx = i < DEPTH - 1
        mod = mods[i]
        stream = (x, ctx, 0) if i == 0 else (h, h, NL)
        h = _token_mixers(*stream, mod, i, with_ctx, tabs, norm1_g, w_in, fnet_w, ret_decay_fwd, ret_decay_bwd,
                          ret_norm_g, ret_w, attn_qn_g, attn_kn_g, attn_w, conv_dw_w, conv_dw_b, conv_ln_g,
                          conv_ln_b, conv_w_out, w_out)
        j = i // 2
        g2 = norm2_g[i][None, :]
        if i % 2 == 0:
            h = _ffn(h, mod, g2, ffn_w_gate[j].astype(bf16), ffn_w_up[j].astype(bf16),
                     ffn_w_down[j].astype(bf16))
        else:
            h = _moe(h, mod, g2, router_w[j], moe_w_gate[j].astype(bf16), moe_w_up[j].astype(bf16),
                     moe_w_down[j].astype(bf16))
    return h
```

```python
import functools
import math

import numpy as np
import jax
import jax.numpy as jnp
from jax import lax
from jax.experimental import pallas as pl
from jax.experimental.pallas import tpu as pltpu

f32 = jnp.float32
bf16 = jnp.bfloat16

D = 1024
B = 2
S = 8192
C = 256
T = S + C
DEPTH = 2
GRID_W = 64
HD = 64
EPS = 1e-6
RET_HEADS = 4
RET_CHUNK = 128
N_CHUNK = T // RET_CHUNK
ATT_Q_HEADS = 8
ATT_KV_HEADS = 2
ATT_GROUP = 4
CONV_K = 31
CONV_W = 256
HALO = 16
IN_COLS = 6656
FFN = 2816
FFN_CHUNK = 256
N_EXPERTS = 8
MOE_BLOCK = 128
ROPE_THETA = 10000.0

TM = 256
NT = T // TM
NL = S // TM
RET_STEP = TM // RET_CHUNK
TK = 256
VROWS = HD + 16
LOG2E = math.log2(math.e)
ATT_SAFE_BOUND = 48.0
ATT_UNROLL = 8

N_TOK = B * S
N_ASSIGN = N_TOK * 2
N_BLOCKS = N_ASSIGN // MOE_BLOCK + N_EXPERTS
N_ROWS = N_BLOCKS * MOE_BLOCK
CMB = 256
DMA_UNROLL = 8

FN1 = 64
FN2 = 128
FN_STEP = 8
FN_COLS = 4096

VMEM_LIMIT = 56 * 1024 * 1024


def _cparams(*sem):
    return pltpu.CompilerParams(dimension_semantics=sem, vmem_limit_bytes=VMEM_LIMIT)


def _const_spec(shape):
    nd = len(shape)
    return pl.BlockSpec(shape, lambda *_: (0,) * nd, pipeline_mode=pl.Buffered(1))


def _mod_spec(k):
    return pl.BlockSpec((1, 1, D), lambda b, i: (jnp.where(i == NL, 2, b) * 6 + k, 0, 0))


def _dot(a, b):
    return jnp.dot(a, b, preferred_element_type=f32)


def _silu(x):
    return x * _sigmoid(x)


def _sigmoid(x):
    return 0.5 * jnp.tanh(0.5 * x) + 0.5


def _dft_tables():
    n1 = np.arange(FN1)[:, None, None]
    k2 = np.arange(FN2)[None, :, None]
    n2 = np.arange(FN2)[None, None, :]
    ph = 2.0 * np.pi * (((n1 + FN1 * n2) * k2) % S) / S
    g = np.concatenate([np.cos(ph), -np.sin(ph)], axis=1)
    k1 = np.arange(FN1)[:, None]
    m1 = np.arange(FN1)[None, :]
    ph = 2.0 * np.pi * ((k1 * m1) % FN1) / FN1
    c64, s64 = np.cos(ph), np.sin(ph)
    m2 = np.block([[c64, s64], [-s64, c64]])
    kc = np.arange(C)[:, None]
    nc = np.arange(C)[None, :]
    ph = 2.0 * np.pi * ((kc * nc) % C) / C
    fc = np.concatenate([np.cos(ph), -np.sin(ph)], axis=0)
    eye4 = np.eye(4)
    cbd = np.kron(eye4, c64)
    sbd = np.kron(eye4, s64)
    bd = np.kron(np.eye(8), np.ones((HD, HD)))
    return g, m2, fc, cbd, sbd, bd


_G_NP, _M2_NP, _FC_NP, _CBD_NP, _SBD_NP, _BD_NP = _dft_tables()


def _rope_tables():
    rows = S // GRID_W
    row = jnp.broadcast_to(jnp.arange(rows)[:, None], (rows, GRID_W)).reshape(-1).astype(f32)
    col = jnp.broadcast_to(jnp.arange(GRID_W)[None, :], (rows, GRID_W)).reshape(-1).astype(f32)
    n_axis = HD // 4
    inv = ROPE_THETA ** (-jnp.arange(n_axis, dtype=f32) / n_axis)
    ang = jnp.concatenate([row[:, None] * inv, col[:, None] * inv], axis=-1)
    cos, sin = jnp.cos(ang), jnp.sin(ang)
    cos = jnp.concatenate([cos, jnp.ones((C, HD // 2), f32)], axis=0)
    sin = jnp.concatenate([sin, jnp.zeros((C, HD // 2), f32)], axis=0)
    cos128 = jnp.concatenate([cos, cos, cos, cos], axis=-1)
    sin128 = jnp.concatenate([-sin, sin, -sin, sin], axis=-1)
    return cos128, sin128


def _ada_kernel(c_ref, w_ref, b_ref, o_ref):
    s = _silu(c_ref[...])
    o_ref[0] = jnp.dot(s, w_ref[0], preferred_element_type=f32,
                       precision=lax.Precision.HIGHEST) + b_ref[0]


def _ada(c_rows, ada_w, ada_b):
    tn = 1536
    return pl.pallas_call(
        _ada_kernel,
        grid=(DEPTH, 6 * D // tn),
        in_specs=[pl.BlockSpec((8, D), lambda l, j: (0, 0)),
                  pl.BlockSpec((1, D, tn), lambda l, j: (l, 0, j)),
                  pl.BlockSpec((1, 1, tn), lambda l, j: (l, 0, j))],
        out_specs=pl.BlockSpec((1, 8, tn), lambda l, j: (l, 0, j)),
        out_shape=jax.ShapeDtypeStruct((DEPTH, 8, 6 * D), f32),
        compiler_params=_cparams("parallel", "parallel"),
        name="ada",
    )(c_rows, ada_w, ada_b.reshape(DEPTH, 1, 6 * D))


def _norm_mod(x, g, shift, scale):
    ms = jnp.mean(x * x, axis=-1, keepdims=True)
    return (x * lax.rsqrt(ms + EPS) * g) * (1.0 + scale) + shift


def _swap_halves(x):
    w = x.shape[-1]
    lane = lax.broadcasted_iota(jnp.int32, x.shape, 1)
    fwd = pltpu.roll(x, w - HD // 2, 1)
    bwd = pltpu.roll(x, HD // 2, 1)
    return jnp.where((lane % HD) < HD // 2, fwd, bwd)


def _rope(x, cos, sin):
    return x * cos + _swap_halves(x) * sin


def _head_rms(x, gain, bd):
    x2 = x * x
    hi = x2.astype(bf16)
    lo = (x2 - hi.astype(f32)).astype(bf16)
    ss = _dot(hi, bd) + _dot(lo, bd)
    return x * lax.rsqrt(ss * (1.0 / HD) + EPS) * gain


def _stream_specs(ctx_tile):
    return [pl.BlockSpec((1, TM, D), lambda b, i: (b, jnp.minimum(i, NL - 1), 0)),
            pl.BlockSpec((1, TM, D), lambda b, i: (b, ctx_tile, 0))]


def _stream_tile(lat_ref, ctx_ref):
    return jnp.where(pl.program_id(1) == NL, ctx_ref[0], lat_ref[0])


def _proj_in_kernel(lat_ref, ctx_ref, sh_ref, sc_ref, g_ref, w_ref, cos_ref, sin_ref, qg_ref, kg_ref, bd_ref,
                    fn_ref, rq_ref, rk_ref, rv_ref, rg_ref, aq_ref, ak_ref, avt_ref, cv_ref, gt_ref):
    a = _norm_mod(_stream_tile(lat_ref, ctx_ref), g_ref[...], sh_ref[0], sc_ref[0]).astype(bf16)

    def proj(c0, c1):
        return _dot(a, w_ref[:, c0:c1])

    cos = cos_ref[...]
    sin = sin_ref[...]
    cos2 = jnp.concatenate([cos, cos], axis=-1)
    sin2 = jnp.concatenate([sin, sin], axis=-1)
    cos4 = jnp.concatenate([cos2, cos2], axis=-1)
    sin4 = jnp.concatenate([sin2, sin2], axis=-1)

    fn_ref[0] = proj(0, 256).astype(bf16)
    rq_ref[0] = _rope(proj(256, 512), cos2, sin2).astype(bf16)
    rk_ref[0] = (_rope(proj(512, 768), cos2, sin2) * (HD ** -0.5)).astype(bf16)
    rv_ref[0] = proj(768, 1024).astype(bf16)
    rg_ref[0] = proj(1024, 1280).astype(bf16)

    q = _head_rms(proj(1280, 1792), qg_ref[...], bd_ref[...])
    qt = (_rope(q, cos4, sin4) * (HD ** -0.5 * LOG2E)).T
    for h in range(ATT_Q_HEADS):
        aq_ref[0, h] = qt[h * HD:(h + 1) * HD, :].astype(bf16)
    k = _head_rms(proj(1792, 1920), kg_ref[...], bd_ref[:2 * HD, :2 * HD])
    k = _rope(k, cos, sin).astype(bf16)
    vt = proj(1920, 2048).T
    row = lax.broadcasted_iota(jnp.int32, (VROWS - HD, TM), 0)
    ones_row = jnp.where(row == 0, 1.0, 0.0).astype(bf16)
    for g in range(ATT_KV_HEADS):
        ak_ref[0, g] = k[:, g * HD:(g + 1) * HD]
        avt_ref[0, g, 0, :HD] = vt[g * HD:(g + 1) * HD, :].astype(bf16)
        avt_ref[0, g, 0, HD:] = ones_row
    cv_ref[0] = proj(2048, 2560).astype(bf16)
    for j in range(4):
        gt_ref[0, :, j * D:(j + 1) * D] = proj(2560 + j * D, 2560 + (j + 1) * D).astype(bf16)


def _proj_in(h_lat, h_ctx, ctx_tile, mod, g1, w_in, cos128, sin128, qg, kg, bd):
    tile = lambda w: pl.BlockSpec((1, TM, w), lambda b, i: (b, i, 0))
    out_shapes = [
        jax.ShapeDtypeStruct((B, T, 256), bf16),
        jax.ShapeDtypeStruct((B, T, 256), bf16),
        jax.ShapeDtypeStruct((B, T, 256), bf16),
        jax.ShapeDtypeStruct((B, T, 256), bf16),
        jax.ShapeDtypeStruct((B, T, 256), bf16),
        jax.ShapeDtypeStruct((B, ATT_Q_HEADS, HD, T), bf16),
        jax.ShapeDtypeStruct((B, ATT_KV_HEADS, T, HD), bf16),
        jax.ShapeDtypeStruct((B, ATT_KV_HEADS, NT, VROWS, TM), bf16),
        jax.ShapeDtypeStruct((B, T, 512), bf16),
        jax.ShapeDtypeStruct((B, T, 4 * D), bf16),
    ]
    out_specs = [
        tile(256), tile(256), tile(256), tile(256), tile(256),
        pl.BlockSpec((1, ATT_Q_HEADS, HD, TM), lambda b, i: (b, 0, 0, i)),
        pl.BlockSpec((1, ATT_KV_HEADS, TM, HD), lambda b, i: (b, 0, i, 0)),
        pl.BlockSpec((1, ATT_KV_HEADS, 1, VROWS, TM), lambda b, i: (b, 0, i, 0, 0)),
        tile(512), tile(4 * D),
    ]
    return pl.pallas_call(
        _proj_in_kernel,
        grid=(B, NT),
        in_specs=_stream_specs(ctx_tile) + [
            _mod_spec(0), _mod_spec(1), _const_spec((1, D)), _const_spec((D, IN_COLS)),
            pl.BlockSpec((TM, 128), lambda b, i: (i, 0)),
            pl.BlockSpec((TM, 128), lambda b, i: (i, 0)),
            _const_spec((1, 512)), _const_spec((1, 128)), _const_spec((512, 512))],
        out_specs=out_specs,
        out_shape=out_shapes,
        compiler_params=_cparams("parallel", "parallel"),
        name="proj_in",
    )(h_lat, h_ctx, mod, mod, g1, w_in, cos128, sin128, qg, kg, bd)


def _attn_kernel(q_ref, k_ref, v_ref, kn_ref, o_ref, qt_ref, s_ref, p_ref, acc_ref):
    nq = ATT_GROUP * TM
    groups = range(ATT_KV_HEADS)
    last = NT - 1
    is_ctx = pl.program_id(1) == NL
    j0 = jnp.where(is_ctx, NL, 0)
    n_pairs = jnp.where(is_ctx, 0, (NT - 1) // 2)
    for g in groups:
        for h in range(ATT_GROUP):
            qt_ref[g, :, h * TM:(h + 1) * TM] = q_ref[0, g * ATT_GROUP + h]

    bounds = []
    for g in groups:
        qf = qt_ref[g].astype(f32)
        qn = jnp.sqrt(jnp.sum(qf * qf, axis=0, keepdims=True))
        bounds.append(qn * jnp.concatenate([kn_ref[0, g, 0:1, :]] * (nq // 128), axis=-1))
    bound_max = jnp.max(jnp.maximum(bounds[0], bounds[1]))

    def key_block(g, t):
        return k_ref[0, g, pl.ds(pl.multiple_of(t * TK, TK), TK), :]

    @pl.when(bound_max <= ATT_SAFE_BOUND)
    def _():
        def probs(t, slot):
            for g in groups:
                p_ref[g, slot] = jnp.exp2(_dot(key_block(g, t), qt_ref[g]) - bounds[g]).astype(bf16)

        def add_values(t, slot, accs):
            return [accs[g] + _dot(v_ref[0, g, t], p_ref[g, slot]) for g in groups]

        def pair(t, accs):
            probs(t + 1, 1)
            accs = add_values(t, 0, accs)
            probs(t + 2, 0)
            return add_values(t + 1, 1, accs)

        def pairs(i, accs):
            t = j0 + 2 * ATT_UNROLL * i
            for u in range(ATT_UNROLL):
                accs = pair(t + 2 * u, accs)
            return accs

        probs(j0, 0)
        accs = lax.fori_loop(0, n_pairs // ATT_UNROLL, pairs, [jnp.zeros((VROWS, nq), f32) for _ in groups])
        accs = add_values(last, 0, accs)
        for g in groups:
            acc_ref[g] = accs[g]

    @pl.when(bound_max > ATT_SAFE_BOUND)
    def _():
        _attn_online_softmax(k_ref, v_ref, qt_ref, s_ref, p_ref, acc_ref, j0, n_pairs)

    outs = []
    for g in groups:
        ot = acc_ref[g, :HD, :] / acc_ref[g, HD:HD + 1, :]
        outs += [ot[:, h * TM:(h + 1) * TM].T for h in range(ATT_GROUP)]
    o_ref[0] = jnp.concatenate(outs, axis=-1).astype(bf16)


def _attn_online_softmax(k_ref, v_ref, qt_ref, s_ref, p_ref, acc_ref, j0, n_pairs):
    nq = ATT_GROUP * TM
    groups = range(ATT_KV_HEADS)

    def scores(t, slot):
        for g in groups:
            k = k_ref[0, g, pl.ds(pl.multiple_of(t * TK, TK), TK), :]
            s_ref[g, slot] = _dot(k, qt_ref[g])

    def weighted_values(t, slot, alphas, accs):
        return [alphas[g] * accs[g] + _dot(v_ref[0, g, t], p_ref[g, slot]) for g in groups]

    def softmax(slot, ms):
        new_ms, alphas = [], []
        for g in groups:
            s = s_ref[g, slot]
            m_new = jnp.maximum(ms[g], jnp.max(s, axis=0, keepdims=True))
            p_ref[g, slot] = jnp.exp2(s - m_new).astype(bf16)
            new_ms.append(m_new)
            alphas.append(jnp.exp2(ms[g] - m_new))
        return new_ms, alphas

    def pair(i, carry):
        ms, alphas, accs = carry
        t = j0 + 2 * i
        scores(t + 1, 1)
        accs = weighted_values(jnp.maximum(t - 1, j0), 1, alphas, accs)
        ms, alphas = softmax(0, ms)
        scores(t + 2, 0)
        accs = weighted_values(t, 0, alphas, accs)
        ms, alphas = softmax(1, ms)
        return ms, alphas, accs

    scores(j0, 0)
    for g in groups:
        p_ref[g, 1] = jnp.zeros((TK, nq), bf16)
    init = ([jnp.full((1, nq), -1e30, f32) for _ in groups], [jnp.ones((1, nq), f32) for _ in groups],
            [jnp.zeros((VROWS, nq), f32) for _ in groups])
    ms, alphas, accs = lax.fori_loop(0, n_pairs, pair, init)
    last = NT - 1
    accs = weighted_values(jnp.maximum(last - 1, j0), 1, alphas, accs)
    ms, alphas = softmax(0, ms)
    accs = weighted_values(last, 0, alphas, accs)
    for g in groups:
        acc_ref[g] = accs[g]


def _attention(aq, ak, avt, n_tiles):
    nq = ATT_GROUP * TM
    akf = ak.astype(f32)
    k_norm = jnp.sqrt(jnp.max(jnp.sum(akf * akf, axis=-1), axis=-1))
    k_norm = jnp.broadcast_to(k_norm[:, :, None, None], (B, ATT_KV_HEADS, 8, 128))
    return pl.pallas_call(
        _attn_kernel,
        grid=(B, n_tiles),
        in_specs=[pl.BlockSpec((1, ATT_Q_HEADS, HD, TM), lambda b, i: (b, 0, 0, i)),
                  pl.BlockSpec((1, ATT_KV_HEADS, T, HD), lambda b, i: (b, 0, 0, 0)),
                  pl.BlockSpec((1, ATT_KV_HEADS, NT, VROWS, TK), lambda b, i: (b, 0, 0, 0, 0)),
                  pl.BlockSpec((1, ATT_KV_HEADS, 8, 128), lambda b, i: (b, 0, 0, 0))],
        out_specs=pl.BlockSpec((1, TM, ATT_Q_HEADS * HD), lambda b, i: (b, i, 0)),
        out_shape=jax.ShapeDtypeStruct((B, n_tiles * TM, ATT_Q_HEADS * HD), bf16),
        scratch_shapes=[pltpu.VMEM((ATT_KV_HEADS, HD, nq), bf16),
                        pltpu.VMEM((ATT_KV_HEADS, 2, TK, nq), f32), pltpu.VMEM((ATT_KV_HEADS, 2, TK, nq), bf16),
                        pltpu.VMEM((ATT_KV_HEADS, VROWS, nq), f32)],
        compiler_params=_cparams("parallel", "arbitrary"),
        name="attention",
    )(aq, ak, avt, k_norm)


def _ret_state_update(s_ref, g_ref, k, v, zeta):
    kzt = (k.astype(f32) * zeta).T
    for h in range(RET_HEADS):
        rows = slice(h * HD, (h + 1) * HD)
        upd = _dot(kzt[rows, :].astype(bf16), v[:, rows])
        s_ref[rows, :] = g_ref[rows, :] * s_ref[rows, :] + upd


def _ret_bwd_kernel(k_ref, v_ref, zeta_ref, g_ref, sb_ref, s_ref):
    @pl.when(pl.program_id(1) == 0)
    def _():
        s_ref[...] = jnp.zeros_like(s_ref)

    for c in reversed(range(RET_STEP)):
        rows = slice(c * RET_CHUNK, (c + 1) * RET_CHUNK)
        sb_ref[0, c] = s_ref[...]
        _ret_state_update(s_ref, g_ref, k_ref[0, rows, :], v_ref[0, rows, :], zeta_ref[...])


def _ret_bwd_states(rk, rv, zeta_b, g_b):
    tile = lambda b, t: (b, NT - 1 - t, 0)
    return pl.pallas_call(
        _ret_bwd_kernel,
        grid=(B, NT),
        in_specs=[pl.BlockSpec((1, TM, 256), tile),
                  pl.BlockSpec((1, TM, 256), tile),
                  _const_spec((RET_CHUNK, 256)), _const_spec((256, HD))],
        out_specs=pl.BlockSpec((1, RET_STEP, 256, HD), lambda b, t: (b, NT - 1 - t, 0, 0)),
        out_shape=jax.ShapeDtypeStruct((B, N_CHUNK, 256, HD), f32),
        scratch_shapes=[pltpu.VMEM((256, HD), f32)],
        compiler_params=_cparams("parallel", "arbitrary"),
        name="ret_bwd_states",
    )(rk, rv, zeta_b, g_b)


def _ret_fwd_kernel(q_ref, k_ref, v_ref, gate_ref, sb_ref, dmask_ref, xif_ref, xib_ref, zeta_ref, g_ref,
                    ng_ref, o_ref, s_ref):
    @pl.when(pl.program_id(1) == 0)
    def _():
        s_ref[...] = jnp.zeros_like(s_ref)

    for c in range(RET_STEP):
        rows = slice(c * RET_CHUNK, (c + 1) * RET_CHUNK)
        q = q_ref[0, rows, :]
        k = k_ref[0, rows, :]
        v = v_ref[0, rows, :]
        qf = q.astype(f32)
        q_xf = (qf * xif_ref[...]).astype(bf16)
        q_xb = (qf * xib_ref[...]).astype(bf16)
        outs = []
        for h in range(RET_HEADS):
            cols = slice(h * HD, (h + 1) * HD)
            s = lax.dot_general(q[:, cols], k[:, cols], (((1,), (1,)), ((), ())), preferred_element_type=f32)
            att = (s * dmask_ref[h]).astype(bf16)
            o = (_dot(att, v[:, cols])
                 + _dot(q_xf[:, cols], s_ref[cols, :].astype(bf16))
                 + _dot(q_xb[:, cols], sb_ref[0, c, cols, :].astype(bf16)))
            ms = jnp.mean(o * o, axis=-1, keepdims=True)
            outs.append(o * lax.rsqrt(ms + EPS))
        y = jnp.concatenate(outs, axis=-1) * ng_ref[...]
        o_ref[0, rows, :] = (y * _silu(gate_ref[0, rows, :].astype(f32))).astype(bf16)
        _ret_state_update(s_ref, g_ref, k, v, zeta_ref[...])


def _retention(rq, rk, rv, rg, sb, dmask, xi_f, xi_b, zeta_f, g_f, ng):
    tile = lambda b, t: (b, (t + NL) % NT, 0)
    blk = pl.BlockSpec((1, TM, 256), tile)
    return pl.pallas_call(
        _ret_fwd_kernel,
        grid=(B, NT),
        in_specs=[blk, blk, blk, blk,
                  pl.BlockSpec((1, RET_STEP, 256, HD), lambda b, t: (b, (t + NL) % NT, 0, 0)),
                  _const_spec((RET_HEADS, RET_CHUNK, RET_CHUNK)),
                  _const_spec((RET_CHUNK, 256)), _const_spec((RET_CHUNK, 256)), _const_spec((RET_CHUNK, 256)),
                  _const_spec((256, HD)), _const_spec((1, 256))],
        out_specs=blk,
        out_shape=jax.ShapeDtypeStruct((B, T, 256), bf16),
        scratch_shapes=[pltpu.VMEM((256, HD), f32)],
        compiler_params=_cparams("parallel", "arbitrary"),
        name="retention",
    )(rq, rk, rv, rg, sb, dmask, xi_f, xi_b, zeta_f, g_f, ng)


def _ret_tables(dec_f, dec_b):
    lg_f = jax.nn.log_sigmoid(dec_f.astype(f32))
    lg_b = jax.nn.log_sigmoid(dec_b.astype(f32))
    pos = jnp.arange(RET_CHUNK, dtype=f32)
    diff = pos[:, None] - pos[None, :]
    d_f = jnp.where(diff[None] >= 0.0, jnp.exp(jnp.maximum(diff, 0.0)[None] * lg_f[:, None, None]), 0.0)
    d_b = jnp.where(diff[None] <= 0.0, jnp.exp(jnp.maximum(-diff, 0.0)[None] * lg_b[:, None, None]), 0.0)
    lanes = lambda t: jnp.repeat(t, HD, axis=1)
    xi_f = lanes(jnp.exp((pos[:, None] + 1.0) * lg_f[None, :]))
    xi_b = lanes(jnp.exp((RET_CHUNK - pos[:, None]) * lg_b[None, :]))
    zeta_f = lanes(jnp.exp((RET_CHUNK - 1.0 - pos[:, None]) * lg_f[None, :]))
    zeta_b = lanes(jnp.exp(pos[:, None] * lg_b[None, :]))
    rows = lambda t: jnp.broadcast_to(jnp.repeat(t, HD)[:, None], (RET_HEADS * HD, HD))
    g_f = rows(jnp.exp(RET_CHUNK * lg_f))
    g_b = rows(jnp.exp(RET_CHUNK * lg_b))
    return d_f + d_b, xi_f, xi_b, zeta_f, zeta_b, g_f, g_b


def _conv_kernel(prev_ref, cur_ref, next_ref, w_ref, b_ref, lg_ref, lb_ref, o_ref, win_ref, sh_ref):
    i = pl.program_id(1)

    def glu(u):
        u = u.astype(f32)
        return u[:, :CONV_W] * _sigmoid(u[:, CONV_W:])

    has_prev = jnp.logical_and(i >= 1, i < NL).astype(f32)
    has_next = (i < NL - 1).astype(f32)
    win_ref[0:HALO, :] = glu(prev_ref[0, TM - HALO:TM, :]) * has_prev
    win_ref[HALO:HALO + TM, :] = glu(cur_ref[0])
    win_ref[HALO + TM:2 * HALO + TM, :] = glu(next_ref[0, 0:HALO, :]) * has_next
    sh_rows = sh_ref.shape[1]
    for j in range(8):
        sh_ref[j] = win_ref[j:j + sh_rows, :]
    acc = jnp.zeros((TM, CONV_W), f32)
    off = HALO - CONV_K // 2
    for k in range(CONV_K):
        a, j = divmod(off + k, 8)
        acc = acc + w_ref[k:k + 1, :] * sh_ref[j, 8 * a:8 * a + TM, :]
    y = acc + b_ref[...]
    mu = jnp.mean(y, axis=-1, keepdims=True)
    var = jnp.mean(jnp.square(y - mu), axis=-1, keepdims=True)
    y = (y - mu) * lax.rsqrt(var + EPS) * lg_ref[...] + lb_ref[...]
    o_ref[0] = _silu(y).astype(bf16)


def _conv(cv, dw_w, dw_b, ln_g, ln_b, n_tiles):
    return pl.pallas_call(
        _conv_kernel,
        grid=(B, n_tiles),
        in_specs=[pl.BlockSpec((1, TM, 512), lambda b, i: (b, jnp.maximum(i - 1, 0), 0)),
                  pl.BlockSpec((1, TM, 512), lambda b, i: (b, i, 0)),
                  pl.BlockSpec((1, TM, 512), lambda b, i: (b, jnp.minimum(i + 1, NT - 1), 0)),
                  _const_spec((CONV_K, CONV_W)), _const_spec((1, CONV_W)),
                  _const_spec((1, CONV_W)), _const_spec((1, CONV_W))],
        out_specs=pl.BlockSpec((1, TM, CONV_W), lambda b, i: (b, i, 0)),
        out_shape=jax.ShapeDtypeStruct((B, n_tiles * TM, CONV_W), bf16),
        scratch_shapes=[pltpu.VMEM((TM + 2 * HALO, CONV_W), f32),
                        pltpu.VMEM((8, TM + 2 * HALO - 8, CONV_W), f32)],
        compiler_params=_cparams("parallel", "parallel"),
        name="conv",
    )(cv, cv, cv, dw_w, dw_b, ln_g, ln_b)


def _fn_stage1_kernel(x_ref, g_ref, a_ref):
    for s in range(FN_STEP):
        res = _dot(g_ref[s], x_ref[0, :, s * 256:(s + 1) * 256])
        a_ref[0, 0, s] = res[:FN2].astype(bf16)
        a_ref[0, 1, s] = res[FN2:].astype(bf16)


def _fn_stage2_kernel(a_ref, m_ref, p_ref):
    res = _dot(m_ref[...], a_ref[0])
    p_ref[0, 0, :FN1] = res[:FN1].astype(bf16)
    p_ref[0, 1, :FN1] = res[FN1:].astype(bf16)
    if p_ref.shape[2] > FN1:
        p_ref[0, :, FN1:] = jnp.zeros((2, p_ref.shape[2] - FN1, p_ref.shape[3]), bf16)


def _fn_ctx_kernel(u_ref, f_ref, _p_in, p_ref):
    res = _dot(f_ref[...], u_ref[0])
    p_ref[0, 0] = res[:C].astype(bf16)
    p_ref[0, 1] = res[C:].astype(bf16)


def _fnet_dft(fn, g_tab, m2, fc, with_ctx):
    p_rows = T if with_ctx else S
    x = fn.reshape(B, T // FN1, FN1 * 256)
    a = pl.pallas_call(
        _fn_stage1_kernel,
        grid=(B, FN1 // FN_STEP),
        in_specs=[pl.BlockSpec((1, FN2, FN_STEP * 256), lambda b, j: (b, 0, j)),
                  pl.BlockSpec((FN_STEP, 2 * FN2, FN2), lambda b, j: (j, 0, 0))],
        out_specs=pl.BlockSpec((1, 2, FN_STEP, FN2, 256), lambda b, j: (b, 0, j, 0, 0)),
        out_shape=jax.ShapeDtypeStruct((B, 2, FN1, FN2, 256), bf16),
        compiler_params=_cparams("parallel", "parallel"),
        name="fnet_stage1",
    )(x, g_tab)
    a = a.reshape(B, 2 * FN1, FN2 * 256)
    p = pl.pallas_call(
        _fn_stage2_kernel,
        grid=(B, FN2 * 256 // FN_COLS),
        in_specs=[pl.BlockSpec((1, 2 * FN1, FN_COLS), lambda b, j: (b, 0, j)),
                  _const_spec((2 * FN1, 2 * FN1))],
        out_specs=pl.BlockSpec((1, 2, p_rows // FN2, FN_COLS), lambda b, j: (b, 0, 0, j)),
        out_shape=jax.ShapeDtypeStruct((B, 2, p_rows // FN2, FN2 * 256), bf16),
        compiler_params=_cparams("parallel", "parallel"),
        name="fnet_stage2",
    )(a, m2)
    p = p.reshape(B, 2, p_rows, 256)
    if with_ctx:
        p = pl.pallas_call(
            _fn_ctx_kernel,
            grid=(B,),
            in_specs=[pl.BlockSpec((1, C, 256), lambda b: (b, NL, 0)),
                      _const_spec((2 * C, C)),
                      pl.BlockSpec(memory_space=pl.ANY)],
            out_specs=pl.BlockSpec((1, 2, C, 256), lambda b: (b, 0, NL, 0)),
            out_shape=jax.ShapeDtypeStruct((B, 2, T, 256), bf16),
            input_output_aliases={2: 0},
            compiler_params=_cparams("parallel"),
            name="fnet_ctx",
        )(fn, fc, p)
    return p


def _merge_kernel(lat_ref, ctx_ref, gate_ref, p_ref, r_ref, o_ref, cv_ref, gt_ref, cbd_ref, sbd_ref,
                  fw_ref, rw_ref, aw_ref, cw_ref, ow_ref, out_ref):
    i = pl.program_id(1)
    scale = jnp.where(i == NL, (HD * C) ** -0.5, (HD * S) ** -0.5)
    yfn = (_dot(p_ref[0, 0], cbd_ref[...]) + _dot(p_ref[0, 1], sbd_ref[...])) * scale
    branches = (
        (yfn.astype(bf16), fw_ref),
        (r_ref[0], rw_ref),
        (o_ref[0], aw_ref),
        (cv_ref[0], cw_ref),
    )
    merged = jnp.zeros((TM, D), f32)
    for j, (xin, w_ref) in enumerate(branches):
        g = _sigmoid(gt_ref[0, :, j * D:(j + 1) * D].astype(f32))
        merged = merged + g * _dot(xin, w_ref[...])
    out_ref[0] = (_stream_tile(lat_ref, ctx_ref)
                  + gate_ref[0] * _dot(merged.astype(bf16), ow_ref[...]))


def _merge(h_lat, h_ctx, ctx_tile, mod, p, r, o, cvo, gt, cbd, sbd, fw, rw, aw, cw, ow, n_tiles):
    tile = lambda w: pl.BlockSpec((1, TM, w), lambda b, i: (b, i, 0))
    return pl.pallas_call(
        _merge_kernel,
        grid=(B, n_tiles),
        in_specs=_stream_specs(ctx_tile) + [
            _mod_spec(2),
            pl.BlockSpec((1, 2, TM, 256), lambda b, i: (b, 0, i, 0)),
            tile(256), tile(512), tile(256), tile(4 * D),
            _const_spec((256, 256)), _const_spec((256, 256)),
            _const_spec((256, D)), _const_spec((256, D)), _const_spec((512, D)),
            _const_spec((256, D)), _const_spec((D, D))],
        out_specs=tile(D),
        out_shape=jax.ShapeDtypeStruct((B, n_tiles * TM, D), f32),
        compiler_params=_cparams("parallel", "parallel"),
        name="merge",
    )(h_lat, h_ctx, mod, p, r, o, cvo, gt, cbd, sbd, fw, rw, aw, cw, ow)


def _swiglu_rows(a, wg_ref, wu_ref, wd_ref, h_ref):
    for c in range(FFN // FFN_CHUNK):
        cols = slice(c * FFN_CHUNK, (c + 1) * FFN_CHUNK)
        h_ref[:, cols] = (_silu(_dot(a, wg_ref[:, cols])) * _dot(a, wu_ref[:, cols])).astype(bf16)
    return _dot(h_ref[...], wd_ref[...])


def _ffn_kernel(h_ref, sh_ref, sc_ref, gate_ref, g_ref, wg_ref, wu_ref, wd_ref, out_ref, hid_ref):
    x = h_ref[0]
    a = _norm_mod(x, g_ref[...], sh_ref[0], sc_ref[0]).astype(bf16)
    out_ref[0] = x + gate_ref[0] * _swiglu_rows(a, wg_ref, wu_ref, wd_ref, hid_ref)


def _ffn(h, mod, g2, wg, wu, wd):
    tile = pl.BlockSpec((1, TM, D), lambda b, i: (b, i, 0))
    return pl.pallas_call(
        _ffn_kernel,
        grid=(B, NT),
        in_specs=[tile, _mod_spec(3), _mod_spec(4), _mod_spec(5), _const_spec((1, D)),
                  _const_spec((D, FFN)), _const_spec((D, FFN)), _const_spec((FFN, D))],
        out_specs=tile,
        out_shape=jax.ShapeDtypeStruct((B, T, D), f32),
        scratch_shapes=[pltpu.VMEM((TM, FFN), bf16)],
        compiler_params=_cparams("parallel", "parallel"),
        name="ffn",
    )(h, mod, mod, mod, g2, wg, wu, wd)


def _moe_prep_kernel(h_ref, sh_ref, sc_ref, g_ref, rw_ref, e_ref, w_ref):
    f = _norm_mod(h_ref[0], g_ref[...], sh_ref[0], sc_ref[0])
    rw = rw_ref[...]
    f_hi = f.astype(bf16)
    f_lo = (f - f_hi.astype(f32)).astype(bf16)
    w_hi = rw.astype(bf16)
    w_lo = (rw - w_hi.astype(f32)).astype(bf16)
    logits = (_dot(f_lo, w_lo) + _dot(f_lo, w_hi)) + (_dot(f_hi, w_lo) + _dot(f_hi, w_hi))
    lane = lax.broadcasted_iota(jnp.int32, logits.shape, 1)
    neg = jnp.float32(-jnp.inf)
    logits = jnp.where(lane < N_EXPERTS, logits, neg)
    m1 = jnp.max(logits, axis=-1, keepdims=True)
    i1 = jnp.min(jnp.where(logits == m1, lane, 128), axis=-1, keepdims=True)
    rest = jnp.where(lane == i1, neg, logits)
    m2 = jnp.max(rest, axis=-1, keepdims=True)
    i2 = jnp.min(jnp.where(rest == m2, lane, 128), axis=-1, keepdims=True)
    z = jnp.exp(m2 - m1)
    w1 = 1.0 / (1.0 + z)
    w2 = z / (1.0 + z)
    e_ref[...] = jnp.where(lane == 0, i1, jnp.where(lane == 1, i2, 0))
    w_ref[...] = jnp.where(lane == 0, w1, jnp.where(lane == 1, w2, 0.0))


def _moe_token_specs():
    tile = lambda t, *_: (t // NL, t % NL, 0)
    mod = lambda k: pl.BlockSpec((1, 1, D), lambda t, *_: ((t // NL) * 6 + k, 0, 0))
    return [pl.BlockSpec((1, TM, D), tile), mod(3), mod(4), _const_spec((1, D))]


def _moe_prep(h, mod, g2, router_pad):
    return pl.pallas_call(
        _moe_prep_kernel,
        grid=(B * NL,),
        in_specs=_moe_token_specs() + [_const_spec((D, 128))],
        out_specs=[pl.BlockSpec((TM, 128), lambda t: (t, 0)),
                   pl.BlockSpec((TM, 128), lambda t: (t, 0))],
        out_shape=[jax.ShapeDtypeStruct((N_TOK, 128), jnp.int32),
                   jax.ShapeDtypeStruct((N_TOK, 128), f32)],
        compiler_params=_cparams("parallel"),
        name="moe_prep",
    )(h, mod, mod, g2, router_pad)


def _moe_scatter_kernel(nvalid_ref, dest_ref, h_ref, sh_ref, sc_ref, g_ref, xb_hbm, x_ref, zero_ref, sem):
    x_ref[...] = _norm_mod(h_ref[0], g_ref[...], sh_ref[0], sc_ref[0])

    def zero_copy(slot):
        return pltpu.make_async_copy(zero_ref.at[pl.ds(0, 1), :], xb_hbm.at[pl.ds(slot, 1), :], sem.at[2])

    @pl.when(pl.program_id(0) == 0)
    def _():
        zero_ref[...] = jnp.zeros_like(zero_ref)

        def pad_rows(start):
            def each_block(blk, carry):
                def each_row(r, c):
                    copy = zero_copy(blk * MOE_BLOCK + r)
                    copy.start() if start else copy.wait()
                    return c

                return lax.fori_loop(nvalid_ref[blk], MOE_BLOCK, each_row, carry)

            lax.fori_loop(0, N_BLOCKS, each_block, 0)

        pad_rows(True)
        pad_rows(False)

    def issue(r, carry):
        src = x_ref.at[pl.ds(r, 1), :]
        pltpu.make_async_copy(src, xb_hbm.at[pl.ds(dest_ref[0, 0, 2 * r], 1), :], sem.at[0]).start()
        pltpu.make_async_copy(src, xb_hbm.at[pl.ds(dest_ref[0, 0, 2 * r + 1], 1), :], sem.at[1]).start()
        return carry

    lax.fori_loop(0, TM, issue, 0, unroll=DMA_UNROLL)
    for k in range(2):
        pltpu.make_async_copy(x_ref, xb_hbm.at[pl.ds(0, TM), :], sem.at[k]).wait()


def _moe_scatter(nvalid, dest, h, mod, g2):
    grid_spec = pltpu.PrefetchScalarGridSpec(
        num_scalar_prefetch=1,
        grid=(N_TOK // TM,),
        in_specs=[pl.BlockSpec((1, 1, 2 * TM), lambda i, nv: (i, 0, 0), memory_space=pltpu.SMEM)]
        + _moe_token_specs(),
        out_specs=pl.BlockSpec(memory_space=pl.ANY),
        scratch_shapes=[pltpu.VMEM((TM, D), f32), pltpu.VMEM((8, D), f32), pltpu.SemaphoreType.DMA((3,))],
    )
    return pl.pallas_call(
        _moe_scatter_kernel,
        grid_spec=grid_spec,
        out_shape=jax.ShapeDtypeStruct((N_ROWS, D), f32),
        compiler_params=_cparams("arbitrary"),
        name="moe_scatter",
    )(nvalid, dest.reshape(N_TOK // TM, 1, 2 * TM), h, mod, mod, g2)


def _moe_expert_kernel(be_ref, x_ref, wg_ref, wu_ref, wd_ref, y_ref, hid_ref):
    a = x_ref[...].astype(bf16)
    y_ref[...] = _swiglu_rows(a, wg_ref.at[0], wu_ref.at[0], wd_ref.at[0], hid_ref)


def _moe_experts(blk_e, xb, wg, wu, wd):
    grid_spec = pltpu.PrefetchScalarGridSpec(
        num_scalar_prefetch=1,
        grid=(N_BLOCKS,),
        in_specs=[pl.BlockSpec((MOE_BLOCK, D), lambda i, be: (i, 0)),
                  pl.BlockSpec((1, D, FFN), lambda i, be: (be[i], 0, 0)),
                  pl.BlockSpec((1, D, FFN), lambda i, be: (be[i], 0, 0)),
                  pl.BlockSpec((1, FFN, D), lambda i, be: (be[i], 0, 0))],
        out_specs=pl.BlockSpec((MOE_BLOCK, D), lambda i, be: (i, 0)),
        scratch_shapes=[pltpu.VMEM((MOE_BLOCK, FFN), bf16)],
    )
    return pl.pallas_call(
        _moe_expert_kernel,
        grid_spec=grid_spec,
        out_shape=jax.ShapeDtypeStruct((N_ROWS, D), f32),
        compiler_params=_cparams("arbitrary"),
        name="moe_experts",
    )(blk_e, xb, wg, wu, wd)


def _moe_combine_kernel(dest_ref, h_ref, gate_ref, w_ref, y_hbm, out_ref, y0_ref, y1_ref, sem):
    def issue(r, carry):
        pltpu.make_async_copy(y_hbm.at[pl.ds(dest_ref[0, 0, 2 * r], 1), :], y0_ref.at[pl.ds(r, 1), :],
                              sem.at[0]).start()
        pltpu.make_async_copy(y_hbm.at[pl.ds(dest_ref[0, 0, 2 * r + 1], 1), :], y1_ref.at[pl.ds(r, 1), :],
                              sem.at[1]).start()
        return carry

    lax.fori_loop(0, CMB, issue, 0, unroll=DMA_UNROLL)
    pltpu.make_async_copy(y_hbm.at[pl.ds(0, CMB), :], y0_ref, sem.at[0]).wait()
    pltpu.make_async_copy(y_hbm.at[pl.ds(0, CMB), :], y1_ref, sem.at[1]).wait()
    w = w_ref[...]
    y = w[:, 0:1] * y0_ref[...] + w[:, 1:2] * y1_ref[...]
    out_ref[0] = h_ref[0] + gate_ref[0] * y


def _moe_combine(dest, h, mod, w_pad, yb):
    n_per_b = S // CMB
    return pl.pallas_call(
        _moe_combine_kernel,
        grid=(B, n_per_b),
        in_specs=[pl.BlockSpec((1, 1, 2 * CMB), lambda b, i: (b * n_per_b + i, 0, 0), memory_space=pltpu.SMEM),
                  pl.BlockSpec((1, CMB, D), lambda b, i: (b, i, 0)),
                  pl.BlockSpec((1, 1, D), lambda b, i: (b * 6 + 5, 0, 0)),
                  pl.BlockSpec((CMB, 128), lambda b, i: (b * n_per_b + i, 0)),
                  pl.BlockSpec(memory_space=pl.ANY)],
        out_specs=pl.BlockSpec((1, CMB, D), lambda b, i: (b, i, 0)),
        out_shape=jax.ShapeDtypeStruct((B, S, D), f32),
        scratch_shapes=[pltpu.VMEM((CMB, D), f32), pltpu.VMEM((CMB, D), f32), pltpu.SemaphoreType.DMA((2,))],
        compiler_params=_cparams("arbitrary", "arbitrary"),
        name="moe_combine",
    )(dest.reshape(N_TOK // CMB, 1, 2 * CMB), h, mod, w_pad, yb)


def _moe_routing(top_e):
    e = top_e.reshape(-1)
    onehot = (e[:, None] == jnp.arange(N_EXPERTS, dtype=jnp.int32)[None, :]).astype(jnp.int32)
    counts = jnp.sum(onehot, axis=0)
    rank = jnp.sum((jnp.cumsum(onehot, axis=0) - 1) * onehot, axis=1)
    padded = (counts + MOE_BLOCK - 1) // MOE_BLOCK * MOE_BLOCK
    pad_end = jnp.cumsum(padded)
    pad_start = pad_end - padded
    dest = (pad_start[e] + rank).astype(jnp.int32)
    blk_start = jnp.arange(N_BLOCKS, dtype=jnp.int32) * MOE_BLOCK
    blk_e = jnp.minimum(jnp.sum((pad_end[None, :] <= blk_start[:, None]).astype(jnp.int32), axis=1),
                        N_EXPERTS - 1).astype(jnp.int32)
    nvalid = jnp.clip(pad_start[blk_e] + counts[blk_e] - blk_start, 0, MOE_BLOCK).astype(jnp.int32)
    nvalid = jnp.where(blk_start < pad_end[N_EXPERTS - 1], nvalid, 0)
    return dest, blk_e, nvalid


def _moe(h, mod, g2, router_w, wg, wu, wd):
    router_pad = jnp.pad(router_w, ((0, 0), (0, 128 - N_EXPERTS)))
    e_pad, w_pad = _moe_prep(h, mod, g2, router_pad)
    dest, blk_e, nvalid = _moe_routing(e_pad[:, :2])
    xb = _moe_scatter(nvalid, dest, h, mod, g2)
    yb = _moe_experts(blk_e, xb, wg, wu, wd)
    return _moe_combine(dest, h, mod, w_pad, yb)


def _token_mixers(h_lat, h_ctx, ctx_tile, mod, i, with_ctx, tabs, norm1_g, w_in, fnet_w, ret_decay_fwd,
                  ret_decay_bwd, ret_norm_g, ret_w, attn_qn_g, attn_kn_g, attn_w, conv_dw_w, conv_dw_b, conv_ln_g,
                  conv_ln_b, conv_w_out, w_out):
    cos128, sin128, g_tab, m2, fc, cbd, sbd, bd = tabs
    n_tiles = NT if with_ctx else NL
    qg = jnp.tile(attn_qn_g[i].astype(f32), ATT_Q_HEADS)[None, :]
    kg = jnp.tile(attn_kn_g[i].astype(f32), ATT_KV_HEADS)[None, :]
    fn, rq, rk, rv, rg, aq, ak, avt, cv, gt = _proj_in(
        h_lat, h_ctx, ctx_tile, mod, norm1_g[i][None, :], w_in[i].astype(bf16), cos128, sin128, qg, kg, bd)

    o = _attention(aq, ak, avt, n_tiles)

    dmask, xi_f, xi_b, zeta_f, zeta_b, g_f, g_b = _ret_tables(ret_decay_fwd[i], ret_decay_bwd[i])
    sb = _ret_bwd_states(rk, rv, zeta_b, g_b)
    r = _retention(rq, rk, rv, rg, sb, dmask, xi_f, xi_b, zeta_f, g_f, ret_norm_g[i][None, :].astype(f32))

    cvo = _conv(cv, conv_dw_w[i], conv_dw_b[i][None, :], conv_ln_g[i][None, :], conv_ln_b[i][None, :], n_tiles)
    p = _fnet_dft(fn, g_tab, m2, fc, with_ctx)
    return _merge(h_lat, h_ctx, ctx_tile, mod, p, r, o, cvo, gt, cbd, sbd, fnet_w[i].astype(bf16),
                  ret_w[i].astype(bf16), attn_w[i].astype(bf16), conv_w_out[i].astype(bf16),
                  w_out[i].astype(bf16), n_tiles)


def kernel(x, c, ctx, c_ctx, ada_w, ada_b, norm1_g, norm2_g, w_in, fnet_w, ret_decay_fwd, ret_decay_bwd, ret_norm_g, ret_w, attn_qn_g, attn_kn_g, attn_w, conv_dw_w, conv_dw_b, conv_ln_g, conv_ln_b, conv_w_out, w_out, ffn_w_gate, ffn_w_up, ffn_w_down, router_w, moe_w_gate, moe_w_up, moe_w_down):
    cos128, sin128 = _rope_tables()
    as_bf16 = lambda t: jnp.asarray(t, f32).astype(bf16)
    tabs = (cos128, sin128, as_bf16(_G_NP), as_bf16(_M2_NP), as_bf16(_FC_NP), as_bf16(_CBD_NP),
            as_bf16(_SBD_NP), as_bf16(_BD_NP))
    c_rows = jnp.concatenate([c, c_ctx[None, :], jnp.zeros((8 - B - 1, D), f32)], axis=0)
    mods = _ada(c_rows, ada_w, ada_b).reshape(DEPTH, 8 * 6, 1, D)
    h = None
    for i in range(DEPTH):
        with_ctx = i < DEPTH - 1
        mod = mods[i]
        stream = (x, ctx, 0) if i == 0 else (h, h, NL)
        h = _token_mixers(*stream, mod, i, with_ctx, tabs, norm1_g, w_in, fnet_w, ret_decay_fwd, ret_decay_bwd,
                          ret_norm_g, ret_w, attn_qn_g, attn_kn_g, attn_w, conv_dw_w, conv_dw_b, conv_ln_g,
                          conv_ln_b, conv_w_out, w_out)
        j = i // 2
        g2 = norm2_g[i][None, :]
        if i % 2 == 0:
            h = _ffn(h, mod, g2, ffn_w_gate[j].astype(bf16), ffn_w_up[j].astype(bf16),
                     ffn_w_down[j].astype(bf16))
        else:
            h = _moe(h, mod, g2, router_w[j], moe_w_gate[j].astype(bf16), moe_w_up[j].astype(bf16),
                     moe_w_down[j].astype(bf16))
    return h
```

```python
import functools
import math

import numpy as np
import jax
import jax.numpy as jnp
from jax import lax
from jax.experimental import pallas as pl
from jax.experimental.pallas import tpu as pltpu

f32 = jnp.float32
bf16 = jnp.bfloat16

D = 1024
B = 2
S = 8192
C = 256
T = S + C
DEPTH = 2
GRID_W = 64
HD = 64
EPS = 1e-6
RET_HEADS = 4
RET_CHUNK = 128
N_CHUNK = T // RET_CHUNK
ATT_Q_HEADS = 8
ATT_KV_HEADS = 2
ATT_GROUP = 4
CONV_K = 31
CONV_W = 256
HALO = 16
IN_COLS = 6656
FFN = 2816
FFN_CHUNK = 256
N_EXPERTS = 8
MOE_BLOCK = 128
ROPE_THETA = 10000.0

TM = 256
NT = T // TM
NL = S // TM
RET_STEP = TM // RET_CHUNK
TK = 256
VROWS = HD + 16
LOG2E = math.log2(math.e)
ATT_SAFE_BOUND = 48.0
ATT_UNROLL = 16

N_TOK = B * S
N_ASSIGN = N_TOK * 2
N_BLOCKS = N_ASSIGN // MOE_BLOCK + N_EXPERTS
N_ROWS = N_BLOCKS * MOE_BLOCK
CMB = 256
DMA_UNROLL = 8

FN1 = 64
FN2 = 128
FN_STEP = 8
FN_COLS = 4096

VMEM_LIMIT = 56 * 1024 * 1024


def _cparams(*sem):
    return pltpu.CompilerParams(dimension_semantics=sem, vmem_limit_bytes=VMEM_LIMIT)


def _const_spec(shape):
    nd = len(shape)
    return pl.BlockSpec(shape, lambda *_: (0,) * nd, pipeline_mode=pl.Buffered(1))


def _mod_spec(k):
    return pl.BlockSpec((1, 1, D), lambda b, i: (jnp.where(i == NL, 2, b) * 6 + k, 0, 0))


def _dot(a, b):
    return jnp.dot(a, b, preferred_element_type=f32)


def _silu(x):
    return x * _sigmoid(x)


def _sigmoid(x):
    return 0.5 * jnp.tanh(0.5 * x) + 0.5


def _dft_tables():
    n1 = np.arange(FN1)[:, None, None]
    k2 = np.arange(FN2)[None, :, None]
    n2 = np.arange(FN2)[None, None, :]
    ph = 2.0 * np.pi * (((n1 + FN1 * n2) * k2) % S) / S
    g = np.concatenate([np.cos(ph), -np.sin(ph)], axis=1)
    k1 = np.arange(FN1)[:, None]
    m1 = np.arange(FN1)[None, :]
    ph = 2.0 * np.pi * ((k1 * m1) % FN1) / FN1
    c64, s64 = np.cos(ph), np.sin(ph)
    m2 = np.block([[c64, s64], [-s64, c64]])
    kc = np.arange(C)[:, None]
    nc = np.arange(C)[None, :]
    ph = 2.0 * np.pi * ((kc * nc) % C) / C
    fc = np.concatenate([np.cos(ph), -np.sin(ph)], axis=0)
    eye4 = np.eye(4)
    cbd = np.kron(eye4, c64)
    sbd = np.kron(eye4, s64)
    bd = np.kron(np.eye(8), np.ones((HD, HD)))
    return g, m2, fc, cbd, sbd, bd


_G_NP, _M2_NP, _FC_NP, _CBD_NP, _SBD_NP, _BD_NP = _dft_tables()


def _rope_tables():
    rows = S // GRID_W
    row = jnp.broadcast_to(jnp.arange(rows)[:, None], (rows, GRID_W)).reshape(-1).astype(f32)
    col = jnp.broadcast_to(jnp.arange(GRID_W)[None, :], (rows, GRID_W)).reshape(-1).astype(f32)
    n_axis = HD // 4
    inv = ROPE_THETA ** (-jnp.arange(n_axis, dtype=f32) / n_axis)
    ang = jnp.concatenate([row[:, None] * inv, col[:, None] * inv], axis=-1)
    cos, sin = jnp.cos(ang), jnp.sin(ang)
    cos = jnp.concatenate([cos, jnp.ones((C, HD // 2), f32)], axis=0)
    sin = jnp.concatenate([sin, jnp.zeros((C, HD // 2), f32)], axis=0)
    cos128 = jnp.concatenate([cos, cos, cos, cos], axis=-1)
    sin128 = jnp.concatenate([-sin, sin, -sin, sin], axis=-1)
    return cos128, sin128


def _ada_kernel(c_ref, w_ref, b_ref, o_ref):
    s = _silu(c_ref[...])
    o_ref[0] = jnp.dot(s, w_ref[0], preferred_element_type=f32,
                       precision=lax.Precision.HIGHEST) + b_ref[0]


def _ada(c_rows, ada_w, ada_b):
    tn = 1536
    return pl.pallas_call(
        _ada_kernel,
        grid=(DEPTH, 6 * D // tn),
        in_specs=[pl.BlockSpec((8, D), lambda l, j: (0, 0)),
                  pl.BlockSpec((1, D, tn), lambda l, j: (l, 0, j)),
                  pl.BlockSpec((1, 1, tn), lambda l, j: (l, 0, j))],
        out_specs=pl.BlockSpec((1, 8, tn), lambda l, j: (l, 0, j)),
        out_shape=jax.ShapeDtypeStruct((DEPTH, 8, 6 * D), f32),
        compiler_params=_cparams("parallel", "parallel"),
        name="ada",
    )(c_rows, ada_w, ada_b.reshape(DEPTH, 1, 6 * D))


def _norm_mod(x, g, shift, scale):
    ms = jnp.mean(x * x, axis=-1, keepdims=True)
    return (x * lax.rsqrt(ms + EPS) * g) * (1.0 + scale) + shift


def _swap_halves(x):
    w = x.shape[-1]
    lane = lax.broadcasted_iota(jnp.int32, x.shape, 1)
    fwd = pltpu.roll(x, w - HD // 2, 1)
    bwd = pltpu.roll(x, HD // 2, 1)
    return jnp.where((lane % HD) < HD // 2, fwd, bwd)


def _rope(x, cos, sin):
    return x * cos + _swap_halves(x) * sin


def _head_rms(x, gain, bd):
    x2 = x * x
    hi = x2.astype(bf16)
    lo = (x2 - hi.astype(f32)).astype(bf16)
    ss = _dot(hi, bd) + _dot(lo, bd)
    return x * lax.rsqrt(ss * (1.0 / HD) + EPS) * gain


def _stream_specs(ctx_tile):
    return [pl.BlockSpec((1, TM, D), lambda b, i: (b, jnp.minimum(i, NL - 1), 0)),
            pl.BlockSpec((1, TM, D), lambda b, i: (b, ctx_tile, 0))]


def _stream_tile(lat_ref, ctx_ref):
    return jnp.where(pl.program_id(1) == NL, ctx_ref[0], lat_ref[0])


def _proj_in_kernel(lat_ref, ctx_ref, sh_ref, sc_ref, g_ref, w_ref, cos_ref, sin_ref, qg_ref, kg_ref, bd_ref,
                    fn_ref, rq_ref, rk_ref, rv_ref, rg_ref, aq_ref, ak_ref, avt_ref, cv_ref, gt_ref):
    a = _norm_mod(_stream_tile(lat_ref, ctx_ref), g_ref[...], sh_ref[0], sc_ref[0]).astype(bf16)

    def proj(c0, c1):
        return _dot(a, w_ref[:, c0:c1])

    cos = cos_ref[...]
    sin = sin_ref[...]
    cos2 = jnp.concatenate([cos, cos], axis=-1)
    sin2 = jnp.concatenate([sin, sin], axis=-1)
    cos4 = jnp.concatenate([cos2, cos2], axis=-1)
    sin4 = jnp.concatenate([sin2, sin2], axis=-1)

    fn_ref[0] = proj(0, 256).astype(bf16)
    rq_ref[0] = _rope(proj(256, 512), cos2, sin2).astype(bf16)
    rk_ref[0] = (_rope(proj(512, 768), cos2, sin2) * (HD ** -0.5)).astype(bf16)
    rv_ref[0] = proj(768, 1024).astype(bf16)
    rg_ref[0] = proj(1024, 1280).astype(bf16)

    q = _head_rms(proj(1280, 1792), qg_ref[...], bd_ref[...])
    qt = (_rope(q, cos4, sin4) * (HD ** -0.5 * LOG2E)).T
    for h in range(ATT_Q_HEADS):
        aq_ref[0, h] = qt[h * HD:(h + 1) * HD, :].astype(bf16)
    k = _head_rms(proj(1792, 1920), kg_ref[...], bd_ref[:2 * HD, :2 * HD])
    k = _rope(k, cos, sin).astype(bf16)
    vt = proj(1920, 2048).T
    row = lax.broadcasted_iota(jnp.int32, (VROWS - HD, TM), 0)
    ones_row = jnp.where(row == 0, 1.0, 0.0).astype(bf16)
    for g in range(ATT_KV_HEADS):
        ak_ref[0, g] = k[:, g * HD:(g + 1) * HD]
        avt_ref[0, g, 0, :HD] = vt[g * HD:(g + 1) * HD, :].astype(bf16)
        avt_ref[0, g, 0, HD:] = ones_row
    cv_ref[0] = proj(2048, 2560).astype(bf16)
    for j in range(4):
        gt_ref[0, :, j * D:(j + 1) * D] = proj(2560 + j * D, 2560 + (j + 1) * D).astype(bf16)


def _proj_in(h_lat, h_ctx, ctx_tile, mod, g1, w_in, cos128, sin128, qg, kg, bd):
    tile = lambda w: pl.BlockSpec((1, TM, w), lambda b, i: (b, i, 0))
    out_shapes = [
        jax.ShapeDtypeStruct((B, T, 256), bf16),
        jax.ShapeDtypeStruct((B, T, 256), bf16),
        jax.ShapeDtypeStruct((B, T, 256), bf16),
        jax.ShapeDtypeStruct((B, T, 256), bf16),
        jax.ShapeDtypeStruct((B, T, 256), bf16),
        jax.ShapeDtypeStruct((B, ATT_Q_HEADS, HD, T), bf16),
        jax.ShapeDtypeStruct((B, ATT_KV_HEADS, T, HD), bf16),
        jax.ShapeDtypeStruct((B, ATT_KV_HEADS, NT, VROWS, TM), bf16),
        jax.ShapeDtypeStruct((B, T, 512), bf16),
        jax.ShapeDtypeStruct((B, T, 4 * D), bf16),
    ]
    out_specs = [
        tile(256), tile(256), tile(256), tile(256), tile(256),
        pl.BlockSpec((1, ATT_Q_HEADS, HD, TM), lambda b, i: (b, 0, 0, i)),
        pl.BlockSpec((1, ATT_KV_HEADS, TM, HD), lambda b, i: (b, 0, i, 0)),
        pl.BlockSpec((1, ATT_KV_HEADS, 1, VROWS, TM), lambda b, i: (b, 0, i, 0, 0)),
        tile(512), tile(4 * D),
    ]
    return pl.pallas_call(
        _proj_in_kernel,
        grid=(B, NT),
        in_specs=_stream_specs(ctx_tile) + [
            _mod_spec(0), _mod_spec(1), _const_spec((1, D)), _const_spec((D, IN_COLS)),
            pl.BlockSpec((TM, 128), lambda b, i: (i, 0)),
            pl.BlockSpec((TM, 128), lambda b, i: (i, 0)),
            _const_spec((1, 512)), _const_spec((1, 128)), _const_spec((512, 512))],
        out_specs=out_specs,
        out_shape=out_shapes,
        compiler_params=_cparams("parallel", "parallel"),
        name="proj_in",
    )(h_lat, h_ctx, mod, mod, g1, w_in, cos128, sin128, qg, kg, bd)


def _attn_kernel(q_ref, k_ref, v_ref, kn_ref, o_ref, qt_ref, s_ref, p_ref, acc_ref):
    nq = ATT_GROUP * TM
    groups = range(ATT_KV_HEADS)
    last = NT - 1
    is_ctx = pl.program_id(1) == NL
    j0 = jnp.where(is_ctx, NL, 0)
    n_pairs = jnp.where(is_ctx, 0, (NT - 1) // 2)
    for g in groups:
        for h in range(ATT_GROUP):
            qt_ref[g, :, h * TM:(h + 1) * TM] = q_ref[0, g * ATT_GROUP + h]

    bounds = []
    for g in groups:
        qf = qt_ref[g].astype(f32)
        qn = jnp.sqrt(jnp.sum(qf * qf, axis=0, keepdims=True))
        bounds.append(qn * jnp.concatenate([kn_ref[0, g, 0:1, :]] * (nq // 128), axis=-1))
    bound_max = jnp.max(jnp.maximum(bounds[0], bounds[1]))

    def key_block(g, t):
        return k_ref[0, g, pl.ds(pl.multiple_of(t * TK, TK), TK), :]

    @pl.when(bound_max <= ATT_SAFE_BOUND)
    def _():
        def probs(t, slot):
            for g in groups:
                p_ref[g, slot] = jnp.exp2(_dot(key_block(g, t), qt_ref[g]) - bounds[g]).astype(bf16)

        def add_values(t, slot, accs):
            return [accs[g] + _dot(v_ref[0, g, t], p_ref[g, slot]) for g in groups]

        def pair(t, accs):
            probs(t + 1, 1)
            accs = add_values(t, 0, accs)
            probs(t + 2, 0)
            return add_values(t + 1, 1, accs)

        def pairs(i, accs):
            t = j0 + 2 * ATT_UNROLL * i
            for u in range(ATT_UNROLL):
                accs = pair(t + 2 * u, accs)
            return accs

        probs(j0, 0)
        accs = lax.fori_loop(0, n_pairs // ATT_UNROLL, pairs, [jnp.zeros((VROWS, nq), f32) for _ in groups])
        accs = add_values(last, 0, accs)
        for g in groups:
            acc_ref[g] = accs[g]

    @pl.when(bound_max > ATT_SAFE_BOUND)
    def _():
        _attn_online_softmax(k_ref, v_ref, qt_ref, s_ref, p_ref, acc_ref, j0, n_pairs)

    outs = []
    for g in groups:
        ot = acc_ref[g, :HD, :] / acc_ref[g, HD:HD + 1, :]
        outs += [ot[:, h * TM:(h + 1) * TM].T for h in range(ATT_GROUP)]
    o_ref[0] = jnp.concatenate(outs, axis=-1).astype(bf16)


def _attn_online_softmax(k_ref, v_ref, qt_ref, s_ref, p_ref, acc_ref, j0, n_pairs):
    nq = ATT_GROUP * TM
    groups = range(ATT_KV_HEADS)

    def scores(t, slot):
        for g in groups:
            k = k_ref[0, g, pl.ds(pl.multiple_of(t * TK, TK), TK), :]
            s_ref[g, slot] = _dot(k, qt_ref[g])

    def weighted_values(t, slot, alphas, accs):
        return [alphas[g] * accs[g] + _dot(v_ref[0, g, t], p_ref[g, slot]) for g in groups]

    def softmax(slot, ms):
        new_ms, alphas = [], []
        for g in groups:
            s = s_ref[g, slot]
            m_new = jnp.maximum(ms[g], jnp.max(s, axis=0, keepdims=True))
            p_ref[g, slot] = jnp.exp2(s - m_new).astype(bf16)
            new_ms.append(m_new)
            alphas.append(jnp.exp2(ms[g] - m_new))
        return new_ms, alphas

    def pair(i, carry):
        ms, alphas, accs = carry
        t = j0 + 2 * i
        scores(t + 1, 1)
        accs = weighted_values(jnp.maximum(t - 1, j0), 1, alphas, accs)
        ms, alphas = softmax(0, ms)
        scores(t + 2, 0)
        accs = weighted_values(t, 0, alphas, accs)
        ms, alphas = softmax(1, ms)
        return ms, alphas, accs

    scores(j0, 0)
    for g in groups:
        p_ref[g, 1] = jnp.zeros((TK, nq), bf16)
    init = ([jnp.full((1, nq), -1e30, f32) for _ in groups], [jnp.ones((1, nq), f32) for _ in groups],
            [jnp.zeros((VROWS, nq), f32) for _ in groups])
    ms, alphas, accs = lax.fori_loop(0, n_pairs, pair, init)
    last = NT - 1
    accs = weighted_values(jnp.maximum(last - 1, j0), 1, alphas, accs)
    ms, alphas = softmax(0, ms)
    accs = weighted_values(last, 0, alphas, accs)
    for g in groups:
        acc_ref[g] = accs[g]


def _attention(aq, ak, avt, n_tiles):
    nq = ATT_GROUP * TM
    akf = ak.astype(f32)
    k_norm = jnp.sqrt(jnp.max(jnp.sum(akf * akf, axis=-1), axis=-1))
    k_norm = jnp.broadcast_to(k_norm[:, :, None, None], (B, ATT_KV_HEADS, 8, 128))
    return pl.pallas_call(
        _attn_kernel,
        grid=(B, n_tiles),
        in_specs=[pl.BlockSpec((1, ATT_Q_HEADS, HD, TM), lambda b, i: (b, 0, 0, i)),
                  pl.BlockSpec((1, ATT_KV_HEADS, T, HD), lambda b, i: (b, 0, 0, 0)),
                  pl.BlockSpec((1, ATT_KV_HEADS, NT, VROWS, TK), lambda b, i: (b, 0, 0, 0, 0)),
                  pl.BlockSpec((1, ATT_KV_HEADS, 8, 128), lambda b, i: (b, 0, 0, 0))],
        out_specs=pl.BlockSpec((1, TM, ATT_Q_HEADS * HD), lambda b, i: (b, i, 0)),
        out_shape=jax.ShapeDtypeStruct((B, n_tiles * TM, ATT_Q_HEADS * HD), bf16),
        scratch_shapes=[pltpu.VMEM((ATT_KV_HEADS, HD, nq), bf16),
                        pltpu.VMEM((ATT_KV_HEADS, 2, TK, nq), f32), pltpu.VMEM((ATT_KV_HEADS, 2, TK, nq), bf16),
                        pltpu.VMEM((ATT_KV_HEADS, VROWS, nq), f32)],
        compiler_params=_cparams("parallel", "arbitrary"),
        name="attention",
    )(aq, ak, avt, k_norm)


def _ret_state_update(s_ref, g_ref, k, v, zeta):
    kzt = (k.astype(f32) * zeta).T
    for h in range(RET_HEADS):
        rows = slice(h * HD, (h + 1) * HD)
        upd = _dot(kzt[rows, :].astype(bf16), v[:, rows])
        s_ref[rows, :] = g_ref[rows, :] * s_ref[rows, :] + upd


def _ret_bwd_kernel(k_ref, v_ref, zeta_ref, g_ref, sb_ref, s_ref):
    @pl.when(pl.program_id(1) == 0)
    def _():
        s_ref[...] = jnp.zeros_like(s_ref)

    for c in reversed(range(RET_STEP)):
        rows = slice(c * RET_CHUNK, (c + 1) * RET_CHUNK)
        sb_ref[0, c] = s_ref[...]
        _ret_state_update(s_ref, g_ref, k_ref[0, rows, :], v_ref[0, rows, :], zeta_ref[...])


def _ret_bwd_states(rk, rv, zeta_b, g_b):
    tile = lambda b, t: (b, NT - 1 - t, 0)
    return pl.pallas_call(
        _ret_bwd_kernel,
        grid=(B, NT),
        in_specs=[pl.BlockSpec((1, TM, 256), tile),
                  pl.BlockSpec((1, TM, 256), tile),
                  _const_spec((RET_CHUNK, 256)), _const_spec((256, HD))],
        out_specs=pl.BlockSpec((1, RET_STEP, 256, HD), lambda b, t: (b, NT - 1 - t, 0, 0)),
        out_shape=jax.ShapeDtypeStruct((B, N_CHUNK, 256, HD), f32),
        scratch_shapes=[pltpu.VMEM((256, HD), f32)],
        compiler_params=_cparams("parallel", "arbitrary"),
        name="ret_bwd_states",
    )(rk, rv, zeta_b, g_b)


def _ret_fwd_kernel(q_ref, k_ref, v_ref, gate_ref, sb_ref, dmask_ref, xif_ref, xib_ref, zeta_ref, g_ref,
                    ng_ref, o_ref, s_ref):
    @pl.when(pl.program_id(1) == 0)
    def _():
        s_ref[...] = jnp.zeros_like(s_ref)

    for c in range(RET_STEP):
        rows = slice(c * RET_CHUNK, (c + 1) * RET_CHUNK)
        q = q_ref[0, rows, :]
        k = k_ref[0, rows, :]
        v = v_ref[0, rows, :]
        qf = q.astype(f32)
        q_xf = (qf * xif_ref[...]).astype(bf16)
        q_xb = (qf * xib_ref[...]).astype(bf16)
        outs = []
        for h in range(RET_HEADS):
            cols = slice(h * HD, (h + 1) * HD)
            s = lax.dot_general(q[:, cols], k[:, cols], (((1,), (1,)), ((), ())), preferred_element_type=f32)
            att = (s * dmask_ref[h]).astype(bf16)
            o = (_dot(att, v[:, cols])
                 + _dot(q_xf[:, cols], s_ref[cols, :].astype(bf16))
                 + _dot(q_xb[:, cols], sb_ref[0, c, cols, :].astype(bf16)))
            ms = jnp.mean(o * o, axis=-1, keepdims=True)
            outs.append(o * lax.rsqrt(ms + EPS))
        y = jnp.concatenate(outs, axis=-1) * ng_ref[...]
        o_ref[0, rows, :] = (y * _silu(gate_ref[0, rows, :].astype(f32))).astype(bf16)
        _ret_state_update(s_ref, g_ref, k, v, zeta_ref[...])


def _retention(rq, rk, rv, rg, sb, dmask, xi_f, xi_b, zeta_f, g_f, ng):
    tile = lambda b, t: (b, (t + NL) % NT, 0)
    blk = pl.BlockSpec((1, TM, 256), tile)
    return pl.pallas_call(
        _ret_fwd_kernel,
        grid=(B, NT),
        in_specs=[blk, blk, blk, blk,
                  pl.BlockSpec((1, RET_STEP, 256, HD), lambda b, t: (b, (t + NL) % NT, 0, 0)),
                  _const_spec((RET_HEADS, RET_CHUNK, RET_CHUNK)),
                  _const_spec((RET_CHUNK, 256)), _const_spec((RET_CHUNK, 256)), _const_spec((RET_CHUNK, 256)),
                  _const_spec((256, HD)), _const_spec((1, 256))],
        out_specs=blk,
        out_shape=jax.ShapeDtypeStruct((B, T, 256), bf16),
        scratch_shapes=[pltpu.VMEM((256, HD), f32)],
        compiler_params=_cparams("parallel", "arbitrary"),
        name="retention",
    )(rq, rk, rv, rg, sb, dmask, xi_f, xi_b, zeta_f, g_f, ng)


def _ret_tables(dec_f, dec_b):
    lg_f = jax.nn.log_sigmoid(dec_f.astype(f32))
    lg_b = jax.nn.log_sigmoid(dec_b.astype(f32))
    pos = jnp.arange(RET_CHUNK, dtype=f32)
    diff = pos[:, None] - pos[None, :]
    d_f = jnp.where(diff[None] >= 0.0, jnp.exp(jnp.maximum(diff, 0.0)[None] * lg_f[:, None, None]), 0.0)
    d_b = jnp.where(diff[None] <= 0.0, jnp.exp(jnp.maximum(-diff, 0.0)[None] * lg_b[:, None, None]), 0.0)
    lanes = lambda t: jnp.repeat(t, HD, axis=1)
    xi_f = lanes(jnp.exp((pos[:, None] + 1.0) * lg_f[None, :]))
    xi_b = lanes(jnp.exp((RET_CHUNK - pos[:, None]) * lg_b[None, :]))
    zeta_f = lanes(jnp.exp((RET_CHUNK - 1.0 - pos[:, None]) * lg_f[None, :]))
    zeta_b = lanes(jnp.exp(pos[:, None] * lg_b[None, :]))
    rows = lambda t: jnp.broadcast_to(jnp.repeat(t, HD)[:, None], (RET_HEADS * HD, HD))
    g_f = rows(jnp.exp(RET_CHUNK * lg_f))
    g_b = rows(jnp.exp(RET_CHUNK * lg_b))
    return d_f + d_b, xi_f, xi_b, zeta_f, zeta_b, g_f, g_b


def _conv_kernel(prev_ref, cur_ref, next_ref, w_ref, b_ref, lg_ref, lb_ref, o_ref, win_ref, sh_ref):
    i = pl.program_id(1)

    def glu(u):
        u = u.astype(f32)
        return u[:, :CONV_W] * _sigmoid(u[:, CONV_W:])

    has_prev = jnp.logical_and(i >= 1, i < NL).astype(f32)
    has_next = (i < NL - 1).astype(f32)
    win_ref[0:HALO, :] = glu(prev_ref[0, TM - HALO:TM, :]) * has_prev
    win_ref[HALO:HALO + TM, :] = glu(cur_ref[0])
    win_ref[HALO + TM:2 * HALO + TM, :] = glu(next_ref[0, 0:HALO, :]) * has_next
    sh_rows = sh_ref.shape[1]
    for j in range(8):
        sh_ref[j] = win_ref[j:j + sh_rows, :]
    acc = jnp.zeros((TM, CONV_W), f32)
    off = HALO - CONV_K // 2
    for k in range(CONV_K):
        a, j = divmod(off + k, 8)
        acc = acc + w_ref[k:k + 1, :] * sh_ref[j, 8 * a:8 * a + TM, :]
    y = acc + b_ref[...]
    mu = jnp.mean(y, axis=-1, keepdims=True)
    var = jnp.mean(jnp.square(y - mu), axis=-1, keepdims=True)
    y = (y - mu) * lax.rsqrt(var + EPS) * lg_ref[...] + lb_ref[...]
    o_ref[0] = _silu(y).astype(bf16)


def _conv(cv, dw_w, dw_b, ln_g, ln_b, n_tiles):
    return pl.pallas_call(
        _conv_kernel,
        grid=(B, n_tiles),
        in_specs=[pl.BlockSpec((1, TM, 512), lambda b, i: (b, jnp.maximum(i - 1, 0), 0)),
                  pl.BlockSpec((1, TM, 512), lambda b, i: (b, i, 0)),
                  pl.BlockSpec((1, TM, 512), lambda b, i: (b, jnp.minimum(i + 1, NT - 1), 0)),
                  _const_spec((CONV_K, CONV_W)), _const_spec((1, CONV_W)),
                  _const_spec((1, CONV_W)), _const_spec((1, CONV_W))],
        out_specs=pl.BlockSpec((1, TM, CONV_W), lambda b, i: (b, i, 0)),
        out_shape=jax.ShapeDtypeStruct((B, n_tiles * TM, CONV_W), bf16),
        scratch_shapes=[pltpu.VMEM((TM + 2 * HALO, CONV_W), f32),
                        pltpu.VMEM((8, TM + 2 * HALO - 8, CONV_W), f32)],
        compiler_params=_cparams("parallel", "parallel"),
        name="conv",
    )(cv, cv, cv, dw_w, dw_b, ln_g, ln_b)


def _fn_stage1_kernel(x_ref, g_ref, a_ref):
    for s in range(FN_STEP):
        res = _dot(g_ref[s], x_ref[0, :, s * 256:(s + 1) * 256])
        a_ref[0, 0, s] = res[:FN2].astype(bf16)
        a_ref[0, 1, s] = res[FN2:].astype(bf16)


def _fn_stage2_kernel(a_ref, m_ref, p_ref):
    res = _dot(m_ref[...], a_ref[0])
    p_ref[0, 0, :FN1] = res[:FN1].astype(bf16)
    p_ref[0, 1, :FN1] = res[FN1:].astype(bf16)
    if p_ref.shape[2] > FN1:
        p_ref[0, :, FN1:] = jnp.zeros((2, p_ref.shape[2] - FN1, p_ref.shape[3]), bf16)


def _fn_ctx_kernel(u_ref, f_ref, _p_in, p_ref):
    res = _dot(f_ref[...], u_ref[0])
    p_ref[0, 0] = res[:C].astype(bf16)
    p_ref[0, 1] = res[C:].astype(bf16)


def _fnet_dft(fn, g_tab, m2, fc, with_ctx):
    p_rows = T if with_ctx else S
    x = fn.reshape(B, T // FN1, FN1 * 256)
    a = pl.pallas_call(
        _fn_stage1_kernel,
        grid=(B, FN1 // FN_STEP),
        in_specs=[pl.BlockSpec((1, FN2, FN_STEP * 256), lambda b, j: (b, 0, j)),
                  pl.BlockSpec((FN_STEP, 2 * FN2, FN2), lambda b, j: (j, 0, 0))],
        out_specs=pl.BlockSpec((1, 2, FN_STEP, FN2, 256), lambda b, j: (b, 0, j, 0, 0)),
        out_shape=jax.ShapeDtypeStruct((B, 2, FN1, FN2, 256), bf16),
        compiler_params=_cparams("parallel", "parallel"),
        name="fnet_stage1",
    )(x, g_tab)
    a = a.reshape(B, 2 * FN1, FN2 * 256)
    p = pl.pallas_call(
        _fn_stage2_kernel,
        grid=(B, FN2 * 256 // FN_COLS),
        in_specs=[pl.BlockSpec((1, 2 * FN1, FN_COLS), lambda b, j: (b, 0, j)),
                  _const_spec((2 * FN1, 2 * FN1))],
        out_specs=pl.BlockSpec((1, 2, p_rows // FN2, FN_COLS), lambda b, j: (b, 0, 0, j)),
        out_shape=jax.ShapeDtypeStruct((B, 2, p_rows // FN2, FN2 * 256), bf16),
        compiler_params=_cparams("parallel", "parallel"),
        name="fnet_stage2",
    )(a, m2)
    p = p.reshape(B, 2, p_rows, 256)
    if with_ctx:
        p = pl.pallas_call(
            _fn_ctx_kernel,
            grid=(B,),
            in_specs=[pl.BlockSpec((1, C, 256), lambda b: (b, NL, 0)),
                      _const_spec((2 * C, C)),
                      pl.BlockSpec(memory_space=pl.ANY)],
            out_specs=pl.BlockSpec((1, 2, C, 256), lambda b: (b, 0, NL, 0)),
            out_shape=jax.ShapeDtypeStruct((B, 2, T, 256), bf16),
            input_output_aliases={2: 0},
            compiler_params=_cparams("parallel"),
            name="fnet_ctx",
        )(fn, fc, p)
    return p


def _merge_kernel(lat_ref, ctx_ref, gate_ref, p_ref, r_ref, o_ref, cv_ref, gt_ref, cbd_ref, sbd_ref,
                  fw_ref, rw_ref, aw_ref, cw_ref, ow_ref, out_ref):
    i = pl.program_id(1)
    scale = jnp.where(i == NL, (HD * C) ** -0.5, (HD * S) ** -0.5)
    yfn = (_dot(p_ref[0, 0], cbd_ref[...]) + _dot(p_ref[0, 1], sbd_ref[...])) * scale
    branches = (
        (yfn.astype(bf16), fw_ref),
        (r_ref[0], rw_ref),
        (o_ref[0], aw_ref),
        (cv_ref[0], cw_ref),
    )
    merged = jnp.zeros((TM, D), f32)
    for j, (xin, w_ref) in enumerate(branches):
        g = _sigmoid(gt_ref[0, :, j * D:(j + 1) * D].astype(f32))
        merged = merged + g * _dot(xin, w_ref[...])
    out_ref[0] = (_stream_tile(lat_ref, ctx_ref)
                  + gate_ref[0] * _dot(merged.astype(bf16), ow_ref[...]))


def _merge(h_lat, h_ctx, ctx_tile, mod, p, r, o, cvo, gt, cbd, sbd, fw, rw, aw, cw, ow, n_tiles):
    tile = lambda w: pl.BlockSpec((1, TM, w), lambda b, i: (b, i, 0))
    return pl.pallas_call(
        _merge_kernel,
        grid=(B, n_tiles),
        in_specs=_stream_specs(ctx_tile) + [
            _mod_spec(2),
            pl.BlockSpec((1, 2, TM, 256), lambda b, i: (b, 0, i, 0)),
            tile(256), tile(512), tile(256), tile(4 * D),
            _const_spec((256, 256)), _const_spec((256, 256)),
            _const_spec((256, D)), _const_spec((256, D)), _const_spec((512, D)),
            _const_spec((256, D)), _const_spec((D, D))],
        out_specs=tile(D),
        out_shape=jax.ShapeDtypeStruct((B, n_tiles * TM, D), f32),
        compiler_params=_cparams("parallel", "parallel"),
        name="merge",
    )(h_lat, h_ctx, mod, p, r, o, cvo, gt, cbd, sbd, fw, rw, aw, cw, ow)


def _swiglu_rows(a, wg_ref, wu_ref, wd_ref, h_ref):
    for c in range(FFN // FFN_CHUNK):
        cols = slice(c * FFN_CHUNK, (c + 1) * FFN_CHUNK)
        h_ref[:, cols] = (_silu(_dot(a, wg_ref[:, cols])) * _dot(a, wu_ref[:, cols])).astype(bf16)
    return _dot(h_ref[...], wd_ref[...])


def _ffn_kernel(h_ref, sh_ref, sc_ref, gate_ref, g_ref, wg_ref, wu_ref, wd_ref, out_ref, hid_ref):
    x = h_ref[0]
    a = _norm_mod(x, g_ref[...], sh_ref[0], sc_ref[0]).astype(bf16)
    out_ref[0] = x + gate_ref[0] * _swiglu_rows(a, wg_ref, wu_ref, wd_ref, hid_ref)


def _ffn(h, mod, g2, wg, wu, wd):
    tile = pl.BlockSpec((1, TM, D), lambda b, i: (b, i, 0))
    return pl.pallas_call(
        _ffn_kernel,
        grid=(B, NT),
        in_specs=[tile, _mod_spec(3), _mod_spec(4), _mod_spec(5), _const_spec((1, D)),
                  _const_spec((D, FFN)), _const_spec((D, FFN)), _const_spec((FFN, D))],
        out_specs=tile,
        out_shape=jax.ShapeDtypeStruct((B, T, D), f32),
        scratch_shapes=[pltpu.VMEM((TM, FFN), bf16)],
        compiler_params=_cparams("parallel", "parallel"),
        name="ffn",
    )(h, mod, mod, mod, g2, wg, wu, wd)


def _moe_prep_kernel(h_ref, sh_ref, sc_ref, g_ref, rw_ref, x_ref, e_ref, w_ref):
    f = _norm_mod(h_ref[0], g_ref[...], sh_ref[0], sc_ref[0])
    x_ref[...] = f
    rw = rw_ref[...]
    f_hi = f.astype(bf16)
    f_lo = (f - f_hi.astype(f32)).astype(bf16)
    w_hi = rw.astype(bf16)
    w_lo = (rw - w_hi.astype(f32)).astype(bf16)
    logits = (_dot(f_lo, w_lo) + _dot(f_lo, w_hi)) + (_dot(f_hi, w_lo) + _dot(f_hi, w_hi))
    lane = lax.broadcasted_iota(jnp.int32, logits.shape, 1)
    neg = jnp.float32(-jnp.inf)
    logits = jnp.where(lane < N_EXPERTS, logits, neg)
    m1 = jnp.max(logits, axis=-1, keepdims=True)
    i1 = jnp.min(jnp.where(logits == m1, lane, 128), axis=-1, keepdims=True)
    rest = jnp.where(lane == i1, neg, logits)
    m2 = jnp.max(rest, axis=-1, keepdims=True)
    i2 = jnp.min(jnp.where(rest == m2, lane, 128), axis=-1, keepdims=True)
    z = jnp.exp(m2 - m1)
    w1 = 1.0 / (1.0 + z)
    w2 = z / (1.0 + z)
    e_ref[...] = jnp.where(lane == 0, i1, jnp.where(lane == 1, i2, 0))
    w_ref[...] = jnp.where(lane == 0, w1, jnp.where(lane == 1, w2, 0.0))


def _moe_prep(h, mod, g2, router_pad):
    n_steps = B * NL

    def mod_spec(k):
        return pl.BlockSpec((1, 1, D), lambda t: ((t // NL) * 6 + k, 0, 0))

    return pl.pallas_call(
        _moe_prep_kernel,
        grid=(n_steps,),
        in_specs=[pl.BlockSpec((1, TM, D), lambda t: (t // NL, t % NL, 0)),
                  mod_spec(3), mod_spec(4), _const_spec((1, D)), _const_spec((D, 128))],
        out_specs=[pl.BlockSpec((TM, D), lambda t: (t, 0)),
                   pl.BlockSpec((TM, 128), lambda t: (t, 0)),
                   pl.BlockSpec((TM, 128), lambda t: (t, 0))],
        out_shape=[jax.ShapeDtypeStruct((N_TOK, D), f32),
                   jax.ShapeDtypeStruct((N_TOK, 128), jnp.int32),
                   jax.ShapeDtypeStruct((N_TOK, 128), f32)],
        compiler_params=_cparams("parallel"),
        name="moe_prep",
    )(h, mod, mod, g2, router_pad)


def _moe_scatter_kernel(nvalid_ref, dest_ref, x_ref, xb_hbm, zero_ref, sem):
    def zero_copy(slot):
        return pltpu.make_async_copy(zero_ref.at[pl.ds(0, 1), :], xb_hbm.at[pl.ds(slot, 1), :], sem.at[2])

    @pl.when(pl.program_id(0) == 0)
    def _():
        zero_ref[...] = jnp.zeros_like(zero_ref)

        def pad_rows(start):
            def each_block(blk, carry):
                def each_row(r, c):
                    copy = zero_copy(blk * MOE_BLOCK + r)
                    copy.start() if start else copy.wait()
                    return c

                return lax.fori_loop(nvalid_ref[blk], MOE_BLOCK, each_row, carry)

            lax.fori_loop(0, N_BLOCKS, each_block, 0)

        pad_rows(True)
        pad_rows(False)

    def issue(r, carry):
        src = x_ref.at[pl.ds(r, 1), :]
        pltpu.make_async_copy(src, xb_hbm.at[pl.ds(dest_ref[0, 0, 2 * r], 1), :], sem.at[0]).start()
        pltpu.make_async_copy(src, xb_hbm.at[pl.ds(dest_ref[0, 0, 2 * r + 1], 1), :], sem.at[1]).start()
        return carry

    lax.fori_loop(0, TM, issue, 0, unroll=DMA_UNROLL)
    for k in range(2):
        pltpu.make_async_copy(x_ref, xb_hbm.at[pl.ds(0, TM), :], sem.at[k]).wait()


def _moe_scatter(nvalid, dest, xt):
    grid_spec = pltpu.PrefetchScalarGridSpec(
        num_scalar_prefetch=1,
        grid=(N_TOK // TM,),
        in_specs=[pl.BlockSpec((1, 1, 2 * TM), lambda i, nv: (i, 0, 0), memory_space=pltpu.SMEM),
                  pl.BlockSpec((TM, D), lambda i, nv: (i, 0))],
        out_specs=pl.BlockSpec(memory_space=pl.ANY),
        scratch_shapes=[pltpu.VMEM((8, D), f32), pltpu.SemaphoreType.DMA((3,))],
    )
    return pl.pallas_call(
        _moe_scatter_kernel,
        grid_spec=grid_spec,
        out_shape=jax.ShapeDtypeStruct((N_ROWS, D), f32),
        compiler_params=_cparams("arbitrary"),
        name="moe_scatter",
    )(nvalid, dest.reshape(N_TOK // TM, 1, 2 * TM), xt)


def _moe_expert_kernel(be_ref, x_ref, wg_ref, wu_ref, wd_ref, y_ref, hid_ref):
    a = x_ref[...].astype(bf16)
    y_ref[...] = _swiglu_rows(a, wg_ref.at[0], wu_ref.at[0], wd_ref.at[0], hid_ref)


def _moe_experts(blk_e, xb, wg, wu, wd):
    grid_spec = pltpu.PrefetchScalarGridSpec(
        num_scalar_prefetch=1,
        grid=(N_BLOCKS,),
        in_specs=[pl.BlockSpec((MOE_BLOCK, D), lambda i, be: (i, 0)),
                  pl.BlockSpec((1, D, FFN), lambda i, be: (be[i], 0, 0)),
                  pl.BlockSpec((1, D, FFN), lambda i, be: (be[i], 0, 0)),
                  pl.BlockSpec((1, FFN, D), lambda i, be: (be[i], 0, 0))],
        out_specs=pl.BlockSpec((MOE_BLOCK, D), lambda i, be: (i, 0)),
        scratch_shapes=[pltpu.VMEM((MOE_BLOCK, FFN), bf16)],
    )
    return pl.pallas_call(
        _moe_expert_kernel,
        grid_spec=grid_spec,
        out_shape=jax.ShapeDtypeStruct((N_ROWS, D), f32),
        compiler_params=_cparams("arbitrary"),
        name="moe_experts",
    )(blk_e, xb, wg, wu, wd)


def _moe_combine_kernel(dest_ref, h_ref, gate_ref, w_ref, y_hbm, out_ref, y0_ref, y1_ref, sem):
    def issue(r, carry):
        pltpu.make_async_copy(y_hbm.at[pl.ds(dest_ref[0, 0, 2 * r], 1), :], y0_ref.at[pl.ds(r, 1), :],
                              sem.at[0]).start()
        pltpu.make_async_copy(y_hbm.at[pl.ds(dest_ref[0, 0, 2 * r + 1], 1), :], y1_ref.at[pl.ds(r, 1), :],
                              sem.at[1]).start()
        return carry

    lax.fori_loop(0, CMB, issue, 0, unroll=DMA_UNROLL)
    pltpu.make_async_copy(y_hbm.at[pl.ds(0, CMB), :], y0_ref, sem.at[0]).wait()
    pltpu.make_async_copy(y_hbm.at[pl.ds(0, CMB), :], y1_ref, sem.at[1]).wait()
    w = w_ref[...]
    y = w[:, 0:1] * y0_ref[...] + w[:, 1:2] * y1_ref[...]
    out_ref[0] = h_ref[0] + gate_ref[0] * y


def _moe_combine(dest, h, mod, w_pad, yb):
    n_per_b = S // CMB
    return pl.pallas_call(
        _moe_combine_kernel,
        grid=(B, n_per_b),
        in_specs=[pl.BlockSpec((1, 1, 2 * CMB), lambda b, i: (b * n_per_b + i, 0, 0), memory_space=pltpu.SMEM),
                  pl.BlockSpec((1, CMB, D), lambda b, i: (b, i, 0)),
                  pl.BlockSpec((1, 1, D), lambda b, i: (b * 6 + 5, 0, 0)),
                  pl.BlockSpec((CMB, 128), lambda b, i: (b * n_per_b + i, 0)),
                  pl.BlockSpec(memory_space=pl.ANY)],
        out_specs=pl.BlockSpec((1, CMB, D), lambda b, i: (b, i, 0)),
        out_shape=jax.ShapeDtypeStruct((B, S, D), f32),
        scratch_shapes=[pltpu.VMEM((CMB, D), f32), pltpu.VMEM((CMB, D), f32), pltpu.SemaphoreType.DMA((2,))],
        compiler_params=_cparams("arbitrary", "arbitrary"),
        name="moe_combine",
    )(dest.reshape(N_TOK // CMB, 1, 2 * CMB), h, mod, w_pad, yb)


def _moe_routing(top_e):
    e = top_e.reshape(-1)
    onehot = (e[:, None] == jnp.arange(N_EXPERTS, dtype=jnp.int32)[None, :]).astype(jnp.int32)
    counts = jnp.sum(onehot, axis=0)
    rank = jnp.sum((jnp.cumsum(onehot, axis=0) - 1) * onehot, axis=1)
    padded = (counts + MOE_BLOCK - 1) // MOE_BLOCK * MOE_BLOCK
    pad_end = jnp.cumsum(padded)
    pad_start = pad_end - padded
    dest = (pad_start[e] + rank).astype(jnp.int32)
    blk_start = jnp.arange(N_BLOCKS, dtype=jnp.int32) * MOE_BLOCK
    blk_e = jnp.minimum(jnp.sum((pad_end[None, :] <= blk_start[:, None]).astype(jnp.int32), axis=1),
                        N_EXPERTS - 1).astype(jnp.int32)
    nvalid = jnp.clip(pad_start[blk_e] + counts[blk_e] - blk_start, 0, MOE_BLOCK).astype(jnp.int32)
    nvalid = jnp.where(blk_start < pad_end[N_EXPERTS - 1], nvalid, 0)
    return dest, blk_e, nvalid


def _moe(h, mod, g2, router_w, wg, wu, wd):
    router_pad = jnp.pad(router_w, ((0, 0), (0, 128 - N_EXPERTS)))
    xt, e_pad, w_pad = _moe_prep(h, mod, g2, router_pad)
    dest, blk_e, nvalid = _moe_routing(e_pad[:, :2])
    xb = _moe_scatter(nvalid, dest, xt)
    yb = _moe_experts(blk_e, xb, wg, wu, wd)
    return _moe_combine(dest, h, mod, w_pad, yb)


def _token_mixers(h_lat, h_ctx, ctx_tile, mod, i, with_ctx, tabs, norm1_g, w_in, fnet_w, ret_decay_fwd,
                  ret_decay_bwd, ret_norm_g, ret_w, attn_qn_g, attn_kn_g, attn_w, conv_dw_w, conv_dw_b, conv_ln_g,
                  conv_ln_b, conv_w_out, w_out):
    cos128, sin128, g_tab, m2, fc, cbd, sbd, bd = tabs
    n_tiles = NT if with_ctx else NL
    qg = jnp.tile(attn_qn_g[i].astype(f32), ATT_Q_HEADS)[None, :]
    kg = jnp.tile(attn_kn_g[i].astype(f32), ATT_KV_HEADS)[None, :]
    fn, rq, rk, rv, rg, aq, ak, avt, cv, gt = _proj_in(
        h_lat, h_ctx, ctx_tile, mod, norm1_g[i][None, :], w_in[i].astype(bf16), cos128, sin128, qg, kg, bd)

    o = _attention(aq, ak, avt, n_tiles)

    dmask, xi_f, xi_b, zeta_f, zeta_b, g_f, g_b = _ret_tables(ret_decay_fwd[i], ret_decay_bwd[i])
    sb = _ret_bwd_states(rk, rv, zeta_b, g_b)
    r = _retention(rq, rk, rv, rg, sb, dmask, xi_f, xi_b, zeta_f, g_f, ret_norm_g[i][None, :].astype(f32))

    cvo = _conv(cv, conv_dw_w[i], conv_dw_b[i][None, :], conv_ln_g[i][None, :], conv_ln_b[i][None, :], n_tiles)
    p = _fnet_dft(fn, g_tab, m2, fc, with_ctx)
    return _merge(h_lat, h_ctx, ctx_tile, mod, p, r, o, cvo, gt, cbd, sbd, fnet_w[i].astype(bf16),
                  ret_w[i].astype(bf16), attn_w[i].astype(bf16), conv_w_out[i].astype(bf16),
                  w_out[i].astype(bf16), n_tiles)


def kernel(x, c, ctx, c_ctx, ada_w, ada_b, norm1_g, norm2_g, w_in, fnet_w, ret_decay_fwd, ret_decay_bwd, ret_norm_g, ret_w, attn_qn_g, attn_kn_g, attn_w, conv_dw_w, conv_dw_b, conv_ln_g, conv_ln_b, conv_w_out, w_out, ffn_w_gate, ffn_w_up, ffn_w_down, router_w, moe_w_gate, moe_w_up, moe_w_down):
    cos128, sin128 = _rope_tables()
    as_bf16 = lambda t: jnp.asarray(t, f32).astype(bf16)
    tabs = (cos128, sin128, as_bf16(_G_NP), as_bf16(_M2_NP), as_bf16(_FC_NP), as_bf16(_CBD_NP),
            as_bf16(_SBD_NP), as_bf16(_BD_NP))
    c_rows = jnp.concatenate([c, c_ctx[None, :], jnp.zeros((8 - B - 1, D), f32)], axis=0)
    mods = _ada(c_rows, ada_w, ada_b).reshape(DEPTH, 8 * 6, 1, D)
    h = None
    for i in range(DEPTH):
        with_ctx = i < DEPTH - 1
        mod = mods[i]
        stream = (x, ctx, 0) if i == 0 else (h, h, NL)
        h = _token_mixers(*stream, mod, i, with_ctx, tabs, norm1_g, w_in, fnet_w, ret_decay_fwd, ret_decay_bwd,
                          ret_norm_g, ret_w, attn_qn_g, attn_kn_g, attn_w, conv_dw_w, conv_dw_b, conv_ln_g,
                          conv_ln_b, conv_w_out, w_out)
        j = i // 2
        g2 = norm2_g[i][None, :]
        if i % 2 == 0:
            h = _ffn(h, mod, g2, ffn_w_gate[j].astype(bf16), ffn_w_up[j].astype(bf16),
                     ffn_w_down[j].astype(bf16))
        else:
            h = _moe(h, mod, g2, router_w[j], moe_w_gate[j].astype(bf16), moe_w_up[j].astype(bf16),
                     moe_w_down[j].astype(bf16))
    return h
```

```python
import functools
import math

import numpy as np
import jax
import jax.numpy as jnp
from jax import lax
from jax.experimental import pallas as pl
from jax.experimental.pallas import tpu as pltpu

f32 = jnp.float32
bf16 = jnp.bfloat16

D = 1024
B = 2
S = 8192
C = 256
T = S + C
DEPTH = 2
GRID_W = 64
HD = 64
EPS = 1e-6
RET_HEADS = 4
RET_CHUNK = 128
N_CHUNK = T // RET_CHUNK
ATT_Q_HEADS = 8
ATT_KV_HEADS = 2
ATT_GROUP = 4
CONV_K = 31
CONV_W = 256
HALO = 16
IN_COLS = 6656
FFN = 2816
FFN_CHUNK = 256
N_EXPERTS = 8
MOE_BLOCK = 128
ROPE_THETA = 10000.0

TM = 256
NT = T // TM
NL = S // TM
RET_STEP = TM // RET_CHUNK
TK = 256
VROWS = HD + 16
LOG2E = math.log2(math.e)
ATT_SAFE_BOUND = 48.0
ATT_UNROLL = 16

N_TOK = B * S
N_ASSIGN = N_TOK * 2
N_BLOCKS = N_ASSIGN // MOE_BLOCK + N_EXPERTS
N_ROWS = N_BLOCKS * MOE_BLOCK
CMB = 512
DMA_UNROLL = 8

FN1 = 64
FN2 = 128
FN_STEP = 8
FN_COLS = 4096

VMEM_LIMIT = 56 * 1024 * 1024


def _cparams(*sem):
    return pltpu.CompilerParams(dimension_semantics=sem, vmem_limit_bytes=VMEM_LIMIT)


def _const_spec(shape):
    nd = len(shape)
    return pl.BlockSpec(shape, lambda *_: (0,) * nd, pipeline_mode=pl.Buffered(1))


def _mod_spec(k):
    return pl.BlockSpec((1, 1, D), lambda b, i: (jnp.where(i == NL, 2, b) * 6 + k, 0, 0))


def _dot(a, b):
    return jnp.dot(a, b, preferred_element_type=f32)


def _silu(x):
    return x * _sigmoid(x)


def _sigmoid(x):
    return 0.5 * jnp.tanh(0.5 * x) + 0.5


def _dft_tables():
    n1 = np.arange(FN1)[:, None, None]
    k2 = np.arange(FN2)[None, :, None]
    n2 = np.arange(FN2)[None, None, :]
    ph = 2.0 * np.pi * (((n1 + FN1 * n2) * k2) % S) / S
    g = np.concatenate([np.cos(ph), -np.sin(ph)], axis=1)
    k1 = np.arange(FN1)[:, None]
    m1 = np.arange(FN1)[None, :]
    ph = 2.0 * np.pi * ((k1 * m1) % FN1) / FN1
    c64, s64 = np.cos(ph), np.sin(ph)
    m2 = np.block([[c64, s64], [-s64, c64]])
    kc = np.arange(C)[:, None]
    nc = np.arange(C)[None, :]
    ph = 2.0 * np.pi * ((kc * nc) % C) / C
    fc = np.concatenate([np.cos(ph), -np.sin(ph)], axis=0)
    eye4 = np.eye(4)
    cbd = np.kron(eye4, c64)
    sbd = np.kron(eye4, s64)
    bd = np.kron(np.eye(8), np.ones((HD, HD)))
    return g, m2, fc, cbd, sbd, bd


_G_NP, _M2_NP, _FC_NP, _CBD_NP, _SBD_NP, _BD_NP = _dft_tables()


def _rope_tables():
    rows = S // GRID_W
    row = jnp.broadcast_to(jnp.arange(rows)[:, None], (rows, GRID_W)).reshape(-1).astype(f32)
    col = jnp.broadcast_to(jnp.arange(GRID_W)[None, :], (rows, GRID_W)).reshape(-1).astype(f32)
    n_axis = HD // 4
    inv = ROPE_THETA ** (-jnp.arange(n_axis, dtype=f32) / n_axis)
    ang = jnp.concatenate([row[:, None] * inv, col[:, None] * inv], axis=-1)
    cos, sin = jnp.cos(ang), jnp.sin(ang)
    cos = jnp.concatenate([cos, jnp.ones((C, HD // 2), f32)], axis=0)
    sin = jnp.concatenate([sin, jnp.zeros((C, HD // 2), f32)], axis=0)
    cos128 = jnp.concatenate([cos, cos, cos, cos], axis=-1)
    sin128 = jnp.concatenate([-sin, sin, -sin, sin], axis=-1)
    return cos128, sin128


def _ada_kernel(c_ref, w_ref, b_ref, o_ref):
    s = _silu(c_ref[...])
    o_ref[0] = jnp.dot(s, w_ref[0], preferred_element_type=f32,
                       precision=lax.Precision.HIGHEST) + b_ref[0]


def _ada(c_rows, ada_w, ada_b):
    tn = 1536
    return pl.pallas_call(
        _ada_kernel,
        grid=(DEPTH, 6 * D // tn),
        in_specs=[pl.BlockSpec((8, D), lambda l, j: (0, 0)),
                  pl.BlockSpec((1, D, tn), lambda l, j: (l, 0, j)),
                  pl.BlockSpec((1, 1, tn), lambda l, j: (l, 0, j))],
        out_specs=pl.BlockSpec((1, 8, tn), lambda l, j: (l, 0, j)),
        out_shape=jax.ShapeDtypeStruct((DEPTH, 8, 6 * D), f32),
        compiler_params=_cparams("parallel", "parallel"),
        name="ada",
    )(c_rows, ada_w, ada_b.reshape(DEPTH, 1, 6 * D))


def _norm_mod(x, g, shift, scale):
    ms = jnp.mean(x * x, axis=-1, keepdims=True)
    return (x * lax.rsqrt(ms + EPS) * g) * (1.0 + scale) + shift


def _swap_halves(x):
    w = x.shape[-1]
    lane = lax.broadcasted_iota(jnp.int32, x.shape, 1)
    fwd = pltpu.roll(x, w - HD // 2, 1)
    bwd = pltpu.roll(x, HD // 2, 1)
    return jnp.where((lane % HD) < HD // 2, fwd, bwd)


def _rope(x, cos, sin):
    return x * cos + _swap_halves(x) * sin


def _head_rms(x, gain, bd):
    x2 = x * x
    hi = x2.astype(bf16)
    lo = (x2 - hi.astype(f32)).astype(bf16)
    ss = _dot(hi, bd) + _dot(lo, bd)
    return x * lax.rsqrt(ss * (1.0 / HD) + EPS) * gain


def _stream_specs(ctx_tile):
    return [pl.BlockSpec((1, TM, D), lambda b, i: (b, jnp.minimum(i, NL - 1), 0)),
            pl.BlockSpec((1, TM, D), lambda b, i: (b, ctx_tile, 0))]


def _stream_tile(lat_ref, ctx_ref):
    return jnp.where(pl.program_id(1) == NL, ctx_ref[0], lat_ref[0])


def _proj_in_kernel(lat_ref, ctx_ref, sh_ref, sc_ref, g_ref, w_ref, cos_ref, sin_ref, qg_ref, kg_ref, bd_ref,
                    fn_ref, rq_ref, rk_ref, rv_ref, rg_ref, aq_ref, ak_ref, avt_ref, cv_ref, gt_ref):
    a = _norm_mod(_stream_tile(lat_ref, ctx_ref), g_ref[...], sh_ref[0], sc_ref[0]).astype(bf16)

    def proj(c0, c1):
        return _dot(a, w_ref[:, c0:c1])

    cos = cos_ref[...]
    sin = sin_ref[...]
    cos2 = jnp.concatenate([cos, cos], axis=-1)
    sin2 = jnp.concatenate([sin, sin], axis=-1)
    cos4 = jnp.concatenate([cos2, cos2], axis=-1)
    sin4 = jnp.concatenate([sin2, sin2], axis=-1)

    fn_ref[0] = proj(0, 256).astype(bf16)
    rq_ref[0] = _rope(proj(256, 512), cos2, sin2).astype(bf16)
    rk_ref[0] = (_rope(proj(512, 768), cos2, sin2) * (HD ** -0.5)).astype(bf16)
    rv_ref[0] = proj(768, 1024).astype(bf16)
    rg_ref[0] = proj(1024, 1280).astype(bf16)

    q = _head_rms(proj(1280, 1792), qg_ref[...], bd_ref[...])
    qt = (_rope(q, cos4, sin4) * (HD ** -0.5 * LOG2E)).T
    for h in range(ATT_Q_HEADS):
        aq_ref[0, h] = qt[h * HD:(h + 1) * HD, :].astype(bf16)
    k = _head_rms(proj(1792, 1920), kg_ref[...], bd_ref[:2 * HD, :2 * HD])
    k = _rope(k, cos, sin).astype(bf16)
    vt = proj(1920, 2048).T
    row = lax.broadcasted_iota(jnp.int32, (VROWS - HD, TM), 0)
    ones_row = jnp.where(row == 0, 1.0, 0.0).astype(bf16)
    for g in range(ATT_KV_HEADS):
        ak_ref[0, g] = k[:, g * HD:(g + 1) * HD]
        avt_ref[0, g, 0, :HD] = vt[g * HD:(g + 1) * HD, :].astype(bf16)
        avt_ref[0, g, 0, HD:] = ones_row
    cv_ref[0] = proj(2048, 2560).astype(bf16)
    for j in range(4):
        gt_ref[0, :, j * D:(j + 1) * D] = proj(2560 + j * D, 2560 + (j + 1) * D).astype(bf16)


def _proj_in(h_lat, h_ctx, ctx_tile, mod, g1, w_in, cos128, sin128, qg, kg, bd):
    tile = lambda w: pl.BlockSpec((1, TM, w), lambda b, i: (b, i, 0))
    out_shapes = [
        jax.ShapeDtypeStruct((B, T, 256), bf16),
        jax.ShapeDtypeStruct((B, T, 256), bf16),
        jax.ShapeDtypeStruct((B, T, 256), bf16),
        jax.ShapeDtypeStruct((B, T, 256), bf16),
        jax.ShapeDtypeStruct((B, T, 256), bf16),
        jax.ShapeDtypeStruct((B, ATT_Q_HEADS, HD, T), bf16),
        jax.ShapeDtypeStruct((B, ATT_KV_HEADS, T, HD), bf16),
        jax.ShapeDtypeStruct((B, ATT_KV_HEADS, NT, VROWS, TM), bf16),
        jax.ShapeDtypeStruct((B, T, 512), bf16),
        jax.ShapeDtypeStruct((B, T, 4 * D), bf16),
    ]
    out_specs = [
        tile(256), tile(256), tile(256), tile(256), tile(256),
        pl.BlockSpec((1, ATT_Q_HEADS, HD, TM), lambda b, i: (b, 0, 0, i)),
        pl.BlockSpec((1, ATT_KV_HEADS, TM, HD), lambda b, i: (b, 0, i, 0)),
        pl.BlockSpec((1, ATT_KV_HEADS, 1, VROWS, TM), lambda b, i: (b, 0, i, 0, 0)),
        tile(512), tile(4 * D),
    ]
    return pl.pallas_call(
        _proj_in_kernel,
        grid=(B, NT),
        in_specs=_stream_specs(ctx_tile) + [
            _mod_spec(0), _mod_spec(1), _const_spec((1, D)), _const_spec((D, IN_COLS)),
            pl.BlockSpec((TM, 128), lambda b, i: (i, 0)),
            pl.BlockSpec((TM, 128), lambda b, i: (i, 0)),
            _const_spec((1, 512)), _const_spec((1, 128)), _const_spec((512, 512))],
        out_specs=out_specs,
        out_shape=out_shapes,
        compiler_params=_cparams("parallel", "parallel"),
        name="proj_in",
    )(h_lat, h_ctx, mod, mod, g1, w_in, cos128, sin128, qg, kg, bd)


def _attn_kernel(q_ref, k_ref, v_ref, kn_ref, o_ref, qt_ref, s_ref, p_ref, acc_ref):
    nq = ATT_GROUP * TM
    groups = range(ATT_KV_HEADS)
    last = NT - 1
    is_ctx = pl.program_id(1) == NL
    j0 = jnp.where(is_ctx, NL, 0)
    n_pairs = jnp.where(is_ctx, 0, (NT - 1) // 2)
    for g in groups:
        for h in range(ATT_GROUP):
            qt_ref[g, :, h * TM:(h + 1) * TM] = q_ref[0, g * ATT_GROUP + h]

    bounds = []
    for g in groups:
        qf = qt_ref[g].astype(f32)
        qn = jnp.sqrt(jnp.sum(qf * qf, axis=0, keepdims=True))
        bounds.append(qn * jnp.concatenate([kn_ref[0, g, 0:1, :]] * (nq // 128), axis=-1))
    bound_max = jnp.max(jnp.maximum(bounds[0], bounds[1]))

    def key_block(g, t):
        return k_ref[0, g, pl.ds(pl.multiple_of(t * TK, TK), TK), :]

    @pl.when(bound_max <= ATT_SAFE_BOUND)
    def _():
        def probs(t, slot):
            for g in groups:
                p_ref[g, slot] = jnp.exp2(_dot(key_block(g, t), qt_ref[g]) - bounds[g]).astype(bf16)

        def add_values(t, slot, accs):
            return [accs[g] + _dot(v_ref[0, g, t], p_ref[g, slot]) for g in groups]

        def pair(t, accs):
            probs(t + 1, 1)
            accs = add_values(t, 0, accs)
            probs(t + 2, 0)
            return add_values(t + 1, 1, accs)

        def pairs(i, accs):
            t = j0 + 2 * ATT_UNROLL * i
            for u in range(ATT_UNROLL):
                accs = pair(t + 2 * u, accs)
            return accs

        probs(j0, 0)
        accs = lax.fori_loop(0, n_pairs // ATT_UNROLL, pairs, [jnp.zeros((VROWS, nq), f32) for _ in groups])
        accs = add_values(last, 0, accs)
        for g in groups:
            acc_ref[g] = accs[g]

    @pl.when(bound_max > ATT_SAFE_BOUND)
    def _():
        _attn_online_softmax(k_ref, v_ref, qt_ref, s_ref, p_ref, acc_ref, j0, n_pairs)

    outs = []
    for g in groups:
        ot = acc_ref[g, :HD, :] / acc_ref[g, HD:HD + 1, :]
        outs += [ot[:, h * TM:(h + 1) * TM].T for h in range(ATT_GROUP)]
    o_ref[0] = jnp.concatenate(outs, axis=-1).astype(bf16)


def _attn_online_softmax(k_ref, v_ref, qt_ref, s_ref, p_ref, acc_ref, j0, n_pairs):
    nq = ATT_GROUP * TM
    groups = range(ATT_KV_HEADS)

    def scores(t, slot):
        for g in groups:
            k = k_ref[0, g, pl.ds(pl.multiple_of(t * TK, TK), TK), :]
            s_ref[g, slot] = _dot(k, qt_ref[g])

    def weighted_values(t, slot, alphas, accs):
        return [alphas[g] * accs[g] + _dot(v_ref[0, g, t], p_ref[g, slot]) for g in groups]

    def softmax(slot, ms):
        new_ms, alphas = [], []
        for g in groups:
            s = s_ref[g, slot]
            m_new = jnp.maximum(ms[g], jnp.max(s, axis=0, keepdims=True))
            p_ref[g, slot] = jnp.exp2(s - m_new).astype(bf16)
            new_ms.append(m_new)
            alphas.append(jnp.exp2(ms[g] - m_new))
        return new_ms, alphas

    def pair(i, carry):
        ms, alphas, accs = carry
        t = j0 + 2 * i
        scores(t + 1, 1)
        accs = weighted_values(jnp.maximum(t - 1, j0), 1, alphas, accs)
        ms, alphas = softmax(0, ms)
        scores(t + 2, 0)
        accs = weighted_values(t, 0, alphas, accs)
        ms, alphas = softmax(1, ms)
        return ms, alphas, accs

    scores(j0, 0)
    for g in groups:
        p_ref[g, 1] = jnp.zeros((TK, nq), bf16)
    init = ([jnp.full((1, nq), -1e30, f32) for _ in groups], [jnp.ones((1, nq), f32) for _ in groups],
            [jnp.zeros((VROWS, nq), f32) for _ in groups])
    ms, alphas, accs = lax.fori_loop(0, n_pairs, pair, init)
    last = NT - 1
    accs = weighted_values(jnp.maximum(last - 1, j0), 1, alphas, accs)
    ms, alphas = softmax(0, ms)
    accs = weighted_values(last, 0, alphas, accs)
    for g in groups:
        acc_ref[g] = accs[g]


def _attention(aq, ak, avt, n_tiles):
    nq = ATT_GROUP * TM
    akf = ak.astype(f32)
    k_norm = jnp.sqrt(jnp.max(jnp.sum(akf * akf, axis=-1), axis=-1))
    k_norm = jnp.broadcast_to(k_norm[:, :, None, None], (B, ATT_KV_HEADS, 8, 128))
    return pl.pallas_call(
        _attn_kernel,
        grid=(B, n_tiles),
        in_specs=[pl.BlockSpec((1, ATT_Q_HEADS, HD, TM), lambda b, i: (b, 0, 0, i)),
                  pl.BlockSpec((1, ATT_KV_HEADS, T, HD), lambda b, i: (b, 0, 0, 0)),
                  pl.BlockSpec((1, ATT_KV_HEADS, NT, VROWS, TK), lambda b, i: (b, 0, 0, 0, 0)),
                  pl.BlockSpec((1, ATT_KV_HEADS, 8, 128), lambda b, i: (b, 0, 0, 0))],
        out_specs=pl.BlockSpec((1, TM, ATT_Q_HEADS * HD), lambda b, i: (b, i, 0)),
        out_shape=jax.ShapeDtypeStruct((B, n_tiles * TM, ATT_Q_HEADS * HD), bf16),
        scratch_shapes=[pltpu.VMEM((ATT_KV_HEADS, HD, nq), bf16),
                        pltpu.VMEM((ATT_KV_HEADS, 2, TK, nq), f32), pltpu.VMEM((ATT_KV_HEADS, 2, TK, nq), bf16),
                        pltpu.VMEM((ATT_KV_HEADS, VROWS, nq), f32)],
        compiler_params=_cparams("parallel", "arbitrary"),
        name="attention",
    )(aq, ak, avt, k_norm)


def _ret_state_update(s_ref, g_ref, k, v, zeta):
    kzt = (k.astype(f32) * zeta).T
    for h in range(RET_HEADS):
        rows = slice(h * HD, (h + 1) * HD)
        upd = _dot(kzt[rows, :].astype(bf16), v[:, rows])
        s_ref[rows, :] = g_ref[rows, :] * s_ref[rows, :] + upd


def _ret_bwd_kernel(k_ref, v_ref, zeta_ref, g_ref, sb_ref, s_ref):
    @pl.when(pl.program_id(1) == 0)
    def _():
        s_ref[...] = jnp.zeros_like(s_ref)

    for c in reversed(range(RET_STEP)):
        rows = slice(c * RET_CHUNK, (c + 1) * RET_CHUNK)
        sb_ref[0, c] = s_ref[...]
        _ret_state_update(s_ref, g_ref, k_ref[0, rows, :], v_ref[0, rows, :], zeta_ref[...])


def _ret_bwd_states(rk, rv, zeta_b, g_b):
    tile = lambda b, t: (b, NT - 1 - t, 0)
    return pl.pallas_call(
        _ret_bwd_kernel,
        grid=(B, NT),
        in_specs=[pl.BlockSpec((1, TM, 256), tile),
                  pl.BlockSpec((1, TM, 256), tile),
                  _const_spec((RET_CHUNK, 256)), _const_spec((256, HD))],
        out_specs=pl.BlockSpec((1, RET_STEP, 256, HD), lambda b, t: (b, NT - 1 - t, 0, 0)),
        out_shape=jax.ShapeDtypeStruct((B, N_CHUNK, 256, HD), f32),
        scratch_shapes=[pltpu.VMEM((256, HD), f32)],
        compiler_params=_cparams("parallel", "arbitrary"),
        name="ret_bwd_states",
    )(rk, rv, zeta_b, g_b)


def _ret_fwd_kernel(q_ref, k_ref, v_ref, gate_ref, sb_ref, dmask_ref, xif_ref, xib_ref, zeta_ref, g_ref,
                    ng_ref, o_ref, s_ref):
    @pl.when(pl.program_id(1) == 0)
    def _():
        s_ref[...] = jnp.zeros_like(s_ref)

    for c in range(RET_STEP):
        rows = slice(c * RET_CHUNK, (c + 1) * RET_CHUNK)
        q = q_ref[0, rows, :]
        k = k_ref[0, rows, :]
        v = v_ref[0, rows, :]
        qf = q.astype(f32)
        q_xf = (qf * xif_ref[...]).astype(bf16)
        q_xb = (qf * xib_ref[...]).astype(bf16)
        outs = []
        for h in range(RET_HEADS):
            cols = slice(h * HD, (h + 1) * HD)
            s = lax.dot_general(q[:, cols], k[:, cols], (((1,), (1,)), ((), ())), preferred_element_type=f32)
            att = (s * dmask_ref[h]).astype(bf16)
            o = (_dot(att, v[:, cols])
                 + _dot(q_xf[:, cols], s_ref[cols, :].astype(bf16))
                 + _dot(q_xb[:, cols], sb_ref[0, c, cols, :].astype(bf16)))
            ms = jnp.mean(o * o, axis=-1, keepdims=True)
            outs.append(o * lax.rsqrt(ms + EPS))
        y = jnp.concatenate(outs, axis=-1) * ng_ref[...]
        o_ref[0, rows, :] = (y * _silu(gate_ref[0, rows, :].astype(f32))).astype(bf16)
        _ret_state_update(s_ref, g_ref, k, v, zeta_ref[...])


def _retention(rq, rk, rv, rg, sb, dmask, xi_f, xi_b, zeta_f, g_f, ng):
    tile = lambda b, t: (b, (t + NL) % NT, 0)
    blk = pl.BlockSpec((1, TM, 256), tile)
    return pl.pallas_call(
        _ret_fwd_kernel,
        grid=(B, NT),
        in_specs=[blk, blk, blk, blk,
                  pl.BlockSpec((1, RET_STEP, 256, HD), lambda b, t: (b, (t + NL) % NT, 0, 0)),
                  _const_spec((RET_HEADS, RET_CHUNK, RET_CHUNK)),
                  _const_spec((RET_CHUNK, 256)), _const_spec((RET_CHUNK, 256)), _const_spec((RET_CHUNK, 256)),
                  _const_spec((256, HD)), _const_spec((1, 256))],
        out_specs=blk,
        out_shape=jax.ShapeDtypeStruct((B, T, 256), bf16),
        scratch_shapes=[pltpu.VMEM((256, HD), f32)],
        compiler_params=_cparams("parallel", "arbitrary"),
        name="retention",
    )(rq, rk, rv, rg, sb, dmask, xi_f, xi_b, zeta_f, g_f, ng)


def _ret_tables(dec_f, dec_b):
    lg_f = jax.nn.log_sigmoid(dec_f.astype(f32))
    lg_b = jax.nn.log_sigmoid(dec_b.astype(f32))
    pos = jnp.arange(RET_CHUNK, dtype=f32)
    diff = pos[:, None] - pos[None, :]
    d_f = jnp.where(diff[None] >= 0.0, jnp.exp(jnp.maximum(diff, 0.0)[None] * lg_f[:, None, None]), 0.0)
    d_b = jnp.where(diff[None] <= 0.0, jnp.exp(jnp.maximum(-diff, 0.0)[None] * lg_b[:, None, None]), 0.0)
    lanes = lambda t: jnp.repeat(t, HD, axis=1)
    xi_f = lanes(jnp.exp((pos[:, None] + 1.0) * lg_f[None, :]))
    xi_b = lanes(jnp.exp((RET_CHUNK - pos[:, None]) * lg_b[None, :]))
    zeta_f = lanes(jnp.exp((RET_CHUNK - 1.0 - pos[:, None]) * lg_f[None, :]))
    zeta_b = lanes(jnp.exp(pos[:, None] * lg_b[None, :]))
    rows = lambda t: jnp.broadcast_to(jnp.repeat(t, HD)[:, None], (RET_HEADS * HD, HD))
    g_f = rows(jnp.exp(RET_CHUNK * lg_f))
    g_b = rows(jnp.exp(RET_CHUNK * lg_b))
    return d_f + d_b, xi_f, xi_b, zeta_f, zeta_b, g_f, g_b


def _conv_kernel(prev_ref, cur_ref, next_ref, w_ref, b_ref, lg_ref, lb_ref, o_ref, win_ref, sh_ref):
    i = pl.program_id(1)

    def glu(u):
        u = u.astype(f32)
        return u[:, :CONV_W] * _sigmoid(u[:, CONV_W:])

    has_prev = jnp.logical_and(i >= 1, i < NL).astype(f32)
    has_next = (i < NL - 1).astype(f32)
    win_ref[0:HALO, :] = glu(prev_ref[0, TM - HALO:TM, :]) * has_prev
    win_ref[HALO:HALO + TM, :] = glu(cur_ref[0])
    win_ref[HALO + TM:2 * HALO + TM, :] = glu(next_ref[0, 0:HALO, :]) * has_next
    sh_rows = sh_ref.shape[1]
    for j in range(8):
        sh_ref[j] = win_ref[j:j + sh_rows, :]
    acc = jnp.zeros((TM, CONV_W), f32)
    off = HALO - CONV_K // 2
    for k in range(CONV_K):
        a, j = divmod(off + k, 8)
        acc = acc + w_ref[k:k + 1, :] * sh_ref[j, 8 * a:8 * a + TM, :]
    y = acc + b_ref[...]
    mu = jnp.mean(y, axis=-1, keepdims=True)
    var = jnp.mean(jnp.square(y - mu), axis=-1, keepdims=True)
    y = (y - mu) * lax.rsqrt(var + EPS) * lg_ref[...] + lb_ref[...]
    o_ref[0] = _silu(y).astype(bf16)


def _conv(cv, dw_w, dw_b, ln_g, ln_b, n_tiles):
    return pl.pallas_call(
        _conv_kernel,
        grid=(B, n_tiles),
        in_specs=[pl.BlockSpec((1, TM, 512), lambda b, i: (b, jnp.maximum(i - 1, 0), 0)),
                  pl.BlockSpec((1, TM, 512), lambda b, i: (b, i, 0)),
                  pl.BlockSpec((1, TM, 512), lambda b, i: (b, jnp.minimum(i + 1, NT - 1), 0)),
                  _const_spec((CONV_K, CONV_W)), _const_spec((1, CONV_W)),
                  _const_spec((1, CONV_W)), _const_spec((1, CONV_W))],
        out_specs=pl.BlockSpec((1, TM, CONV_W), lambda b, i: (b, i, 0)),
        out_shape=jax.ShapeDtypeStruct((B, n_tiles * TM, CONV_W), bf16),
        scratch_shapes=[pltpu.VMEM((TM + 2 * HALO, CONV_W), f32),
                        pltpu.VMEM((8, TM + 2 * HALO - 8, CONV_W), f32)],
        compiler_params=_cparams("parallel", "parallel"),
        name="conv",
    )(cv, cv, cv, dw_w, dw_b, ln_g, ln_b)


def _fn_stage1_kernel(x_ref, g_ref, a_ref):
    for s in range(FN_STEP):
        res = _dot(g_ref[s], x_ref[0, :, s * 256:(s + 1) * 256])
        a_ref[0, 0, s] = res[:FN2].astype(bf16)
        a_ref[0, 1, s] = res[FN2:].astype(bf16)


def _fn_stage2_kernel(a_ref, m_ref, p_ref):
    res = _dot(m_ref[...], a_ref[0])
    p_ref[0, 0, :FN1] = res[:FN1].astype(bf16)
    p_ref[0, 1, :FN1] = res[FN1:].astype(bf16)
    if p_ref.shape[2] > FN1:
        p_ref[0, :, FN1:] = jnp.zeros((2, p_ref.shape[2] - FN1, p_ref.shape[3]), bf16)


def _fn_ctx_kernel(u_ref, f_ref, _p_in, p_ref):
    res = _dot(f_ref[...], u_ref[0])
    p_ref[0, 0] = res[:C].astype(bf16)
    p_ref[0, 1] = res[C:].astype(bf16)


def _fnet_dft(fn, g_tab, m2, fc, with_ctx):
    p_rows = T if with_ctx else S
    x = fn.reshape(B, T // FN1, FN1 * 256)
    a = pl.pallas_call(
        _fn_stage1_kernel,
        grid=(B, FN1 // FN_STEP),
        in_specs=[pl.BlockSpec((1, FN2, FN_STEP * 256), lambda b, j: (b, 0, j)),
                  pl.BlockSpec((FN_STEP, 2 * FN2, FN2), lambda b, j: (j, 0, 0))],
        out_specs=pl.BlockSpec((1, 2, FN_STEP, FN2, 256), lambda b, j: (b, 0, j, 0, 0)),
        out_shape=jax.ShapeDtypeStruct((B, 2, FN1, FN2, 256), bf16),
        compiler_params=_cparams("parallel", "parallel"),
        name="fnet_stage1",
    )(x, g_tab)
    a = a.reshape(B, 2 * FN1, FN2 * 256)
    p = pl.pallas_call(
        _fn_stage2_kernel,
        grid=(B, FN2 * 256 // FN_COLS),
        in_specs=[pl.BlockSpec((1, 2 * FN1, FN_COLS), lambda b, j: (b, 0, j)),
                  _const_spec((2 * FN1, 2 * FN1))],
        out_specs=pl.BlockSpec((1, 2, p_rows // FN2, FN_COLS), lambda b, j: (b, 0, 0, j)),
        out_shape=jax.ShapeDtypeStruct((B, 2, p_rows // FN2, FN2 * 256), bf16),
        compiler_params=_cparams("parallel", "parallel"),
        name="fnet_stage2",
    )(a, m2)
    p = p.reshape(B, 2, p_rows, 256)
    if with_ctx:
        p = pl.pallas_call(
            _fn_ctx_kernel,
            grid=(B,),
            in_specs=[pl.BlockSpec((1, C, 256), lambda b: (b, NL, 0)),
                      _const_spec((2 * C, C)),
                      pl.BlockSpec(memory_space=pl.ANY)],
            out_specs=pl.BlockSpec((1, 2, C, 256), lambda b: (b, 0, NL, 0)),
            out_shape=jax.ShapeDtypeStruct((B, 2, T, 256), bf16),
            input_output_aliases={2: 0},
            compiler_params=_cparams("parallel"),
            name="fnet_ctx",
        )(fn, fc, p)
    return p


def _merge_kernel(lat_ref, ctx_ref, gate_ref, p_ref, r_ref, o_ref, cv_ref, gt_ref, cbd_ref, sbd_ref,
                  fw_ref, rw_ref, aw_ref, cw_ref, ow_ref, out_ref):
    i = pl.program_id(1)
    scale = jnp.where(i == NL, (HD * C) ** -0.5, (HD * S) ** -0.5)
    yfn = (_dot(p_ref[0, 0], cbd_ref[...]) + _dot(p_ref[0, 1], sbd_ref[...])) * scale
    branches = (
        (yfn.astype(bf16), fw_ref),
        (r_ref[0], rw_ref),
        (o_ref[0], aw_ref),
        (cv_ref[0], cw_ref),
    )
    merged = jnp.zeros((TM, D), f32)
    for j, (xin, w_ref) in enumerate(branches):
        g = _sigmoid(gt_ref[0, :, j * D:(j + 1) * D].astype(f32))
        merged = merged + g * _dot(xin, w_ref[...])
    out_ref[0] = (_stream_tile(lat_ref, ctx_ref)
                  + gate_ref[0] * _dot(merged.astype(bf16), ow_ref[...]))


def _merge(h_lat, h_ctx, ctx_tile, mod, p, r, o, cvo, gt, cbd, sbd, fw, rw, aw, cw, ow, n_tiles):
    tile = lambda w: pl.BlockSpec((1, TM, w), lambda b, i: (b, i, 0))
    return pl.pallas_call(
        _merge_kernel,
        grid=(B, n_tiles),
        in_specs=_stream_specs(ctx_tile) + [
            _mod_spec(2),
            pl.BlockSpec((1, 2, TM, 256), lambda b, i: (b, 0, i, 0)),
            tile(256), tile(512), tile(256), tile(4 * D),
            _const_spec((256, 256)), _const_spec((256, 256)),
            _const_spec((256, D)), _const_spec((256, D)), _const_spec((512, D)),
            _const_spec((256, D)), _const_spec((D, D))],
        out_specs=tile(D),
        out_shape=jax.ShapeDtypeStruct((B, n_tiles * TM, D), f32),
        compiler_params=_cparams("parallel", "parallel"),
        name="merge",
    )(h_lat, h_ctx, mod, p, r, o, cvo, gt, cbd, sbd, fw, rw, aw, cw, ow)


def _swiglu_rows(a, wg_ref, wu_ref, wd_ref, h_ref):
    for c in range(FFN // FFN_CHUNK):
        cols = slice(c * FFN_CHUNK, (c + 1) * FFN_CHUNK)
        h_ref[:, cols] = (_silu(_dot(a, wg_ref[:, cols])) * _dot(a, wu_ref[:, cols])).astype(bf16)
    return _dot(h_ref[...], wd_ref[...])


def _ffn_kernel(h_ref, sh_ref, sc_ref, gate_ref, g_ref, wg_ref, wu_ref, wd_ref, out_ref, hid_ref):
    x = h_ref[0]
    a = _norm_mod(x, g_ref[...], sh_ref[0], sc_ref[0]).astype(bf16)
    out_ref[0] = x + gate_ref[0] * _swiglu_rows(a, wg_ref, wu_ref, wd_ref, hid_ref)


def _ffn(h, mod, g2, wg, wu, wd):
    tile = pl.BlockSpec((1, TM, D), lambda b, i: (b, i, 0))
    return pl.pallas_call(
        _ffn_kernel,
        grid=(B, NT),
        in_specs=[tile, _mod_spec(3), _mod_spec(4), _mod_spec(5), _const_spec((1, D)),
                  _const_spec((D, FFN)), _const_spec((D, FFN)), _const_spec((FFN, D))],
        out_specs=tile,
        out_shape=jax.ShapeDtypeStruct((B, T, D), f32),
        scratch_shapes=[pltpu.VMEM((TM, FFN), bf16)],
        compiler_params=_cparams("parallel", "parallel"),
        name="ffn",
    )(h, mod, mod, mod, g2, wg, wu, wd)


def _moe_prep_kernel(h_ref, sh_ref, sc_ref, g_ref, rw_ref, x_ref, e_ref, w_ref):
    f = _norm_mod(h_ref[0], g_ref[...], sh_ref[0], sc_ref[0])
    x_ref[...] = f
    rw = rw_ref[...]
    f_hi = f.astype(bf16)
    f_lo = (f - f_hi.astype(f32)).astype(bf16)
    w_hi = rw.astype(bf16)
    w_lo = (rw - w_hi.astype(f32)).astype(bf16)
    logits = (_dot(f_lo, w_lo) + _dot(f_lo, w_hi)) + (_dot(f_hi, w_lo) + _dot(f_hi, w_hi))
    lane = lax.broadcasted_iota(jnp.int32, logits.shape, 1)
    neg = jnp.float32(-jnp.inf)
    logits = jnp.where(lane < N_EXPERTS, logits, neg)
    m1 = jnp.max(logits, axis=-1, keepdims=True)
    i1 = jnp.min(jnp.where(logits == m1, lane, 128), axis=-1, keepdims=True)
    rest = jnp.where(lane == i1, neg, logits)
    m2 = jnp.max(rest, axis=-1, keepdims=True)
    i2 = jnp.min(jnp.where(rest == m2, lane, 128), axis=-1, keepdims=True)
    z = jnp.exp(m2 - m1)
    w1 = 1.0 / (1.0 + z)
    w2 = z / (1.0 + z)
    e_ref[...] = jnp.where(lane == 0, i1, jnp.where(lane == 1, i2, 0))
    w_ref[...] = jnp.where(lane == 0, w1, jnp.where(lane == 1, w2, 0.0))


def _moe_prep(h, mod, g2, router_pad):
    n_steps = B * NL

    def mod_spec(k):
        return pl.BlockSpec((1, 1, D), lambda t: ((t // NL) * 6 + k, 0, 0))

    return pl.pallas_call(
        _moe_prep_kernel,
        grid=(n_steps,),
        in_specs=[pl.BlockSpec((1, TM, D), lambda t: (t // NL, t % NL, 0)),
                  mod_spec(3), mod_spec(4), _const_spec((1, D)), _const_spec((D, 128))],
        out_specs=[pl.BlockSpec((TM, D), lambda t: (t, 0)),
                   pl.BlockSpec((TM, 128), lambda t: (t, 0)),
                   pl.BlockSpec((TM, 128), lambda t: (t, 0))],
        out_shape=[jax.ShapeDtypeStruct((N_TOK, D), f32),
                   jax.ShapeDtypeStruct((N_TOK, 128), jnp.int32),
                   jax.ShapeDtypeStruct((N_TOK, 128), f32)],
        compiler_params=_cparams("parallel"),
        name="moe_prep",
    )(h, mod, mod, g2, router_pad)


def _moe_scatter_kernel(nvalid_ref, dest_ref, x_ref, xb_hbm, zero_ref, sem):
    def zero_copy(slot):
        return pltpu.make_async_copy(zero_ref.at[pl.ds(0, 1), :], xb_hbm.at[pl.ds(slot, 1), :], sem.at[2])

    @pl.when(pl.program_id(0) == 0)
    def _():
        zero_ref[...] = jnp.zeros_like(zero_ref)

        def pad_rows(start):
            def each_block(blk, carry):
                def each_row(r, c):
                    copy = zero_copy(blk * MOE_BLOCK + r)
                    copy.start() if start else copy.wait()
                    return c

                return lax.fori_loop(nvalid_ref[blk], MOE_BLOCK, each_row, carry)

            lax.fori_loop(0, N_BLOCKS, each_block, 0)

        pad_rows(True)
        pad_rows(False)

    def issue(r, carry):
        src = x_ref.at[pl.ds(r, 1), :]
        pltpu.make_async_copy(src, xb_hbm.at[pl.ds(dest_ref[0, 0, 2 * r], 1), :], sem.at[0]).start()
        pltpu.make_async_copy(src, xb_hbm.at[pl.ds(dest_ref[0, 0, 2 * r + 1], 1), :], sem.at[1]).start()
        return carry

    lax.fori_loop(0, TM, issue, 0, unroll=DMA_UNROLL)
    for k in range(2):
        pltpu.make_async_copy(x_ref, xb_hbm.at[pl.ds(0, TM), :], sem.at[k]).wait()


def _moe_scatter(nvalid, dest, xt):
    grid_spec = pltpu.PrefetchScalarGridSpec(
        num_scalar_prefetch=1,
        grid=(N_TOK // TM,),
        in_specs=[pl.BlockSpec((1, 1, 2 * TM), lambda i, nv: (i, 0, 0), memory_space=pltpu.SMEM),
                  pl.BlockSpec((TM, D), lambda i, nv: (i, 0))],
        out_specs=pl.BlockSpec(memory_space=pl.ANY),
        scratch_shapes=[pltpu.VMEM((8, D), f32), pltpu.SemaphoreType.DMA((3,))],
    )
    return pl.pallas_call(
        _moe_scatter_kernel,
        grid_spec=grid_spec,
        out_shape=jax.ShapeDtypeStruct((N_ROWS, D), f32),
        compiler_params=_cparams("arbitrary"),
        name="moe_scatter",
    )(nvalid, dest.reshape(N_TOK // TM, 1, 2 * TM), xt)


def _moe_expert_kernel(be_ref, x_ref, wg_ref, wu_ref, wd_ref, y_ref, hid_ref):
    a = x_ref[...].astype(bf16)
    y_ref[...] = _swiglu_rows(a, wg_ref.at[0], wu_ref.at[0], wd_ref.at[0], hid_ref)


def _moe_experts(blk_e, xb, wg, wu, wd):
    grid_spec = pltpu.PrefetchScalarGridSpec(
        num_scalar_prefetch=1,
        grid=(N_BLOCKS,),
        in_specs=[pl.BlockSpec((MOE_BLOCK, D), lambda i, be: (i, 0)),
                  pl.BlockSpec((1, D, FFN), lambda i, be: (be[i], 0, 0)),
                  pl.BlockSpec((1, D, FFN), lambda i, be: (be[i], 0, 0)),
                  pl.BlockSpec((1, FFN, D), lambda i, be: (be[i], 0, 0))],
        out_specs=pl.BlockSpec((MOE_BLOCK, D), lambda i, be: (i, 0)),
        scratch_shapes=[pltpu.VMEM((MOE_BLOCK, FFN), bf16)],
    )
    return pl.pallas_call(
        _moe_expert_kernel,
        grid_spec=grid_spec,
        out_shape=jax.ShapeDtypeStruct((N_ROWS, D), f32),
        compiler_params=_cparams("arbitrary"),
        name="moe_experts",
    )(blk_e, xb, wg, wu, wd)


def _moe_combine_kernel(dest_ref, h_ref, gate_ref, w_ref, y_hbm, out_ref, y0_ref, y1_ref, sem):
    def issue(r, carry):
        pltpu.make_async_copy(y_hbm.at[pl.ds(dest_ref[0, 0, 2 * r], 1), :], y0_ref.at[pl.ds(r, 1), :],
                              sem.at[0]).start()
        pltpu.make_async_copy(y_hbm.at[pl.ds(dest_ref[0, 0, 2 * r + 1], 1), :], y1_ref.at[pl.ds(r, 1), :],
                              sem.at[1]).start()
        return carry

    lax.fori_loop(0, CMB, issue, 0, unroll=DMA_UNROLL)
    pltpu.make_async_copy(y_hbm.at[pl.ds(0, CMB), :], y0_ref, sem.at[0]).wait()
    pltpu.make_async_copy(y_hbm.at[pl.ds(0, CMB), :], y1_ref, sem.at[1]).wait()
    w = w_ref[...]
    y = w[:, 0:1] * y0_ref[...] + w[:, 1:2] * y1_ref[...]
    out_ref[0] = h_ref[0] + gate_ref[0] * y


def _moe_combine(dest, h, mod, w_pad, yb):
    n_per_b = S // CMB
    return pl.pallas_call(
        _moe_combine_kernel,
        grid=(B, n_per_b),
        in_specs=[pl.BlockSpec((1, 1, 2 * CMB), lambda b, i: (b * n_per_b + i, 0, 0), memory_space=pltpu.SMEM),
                  pl.BlockSpec((1, CMB, D), lambda b, i: (b, i, 0)),
                  pl.BlockSpec((1, 1, D), lambda b, i: (b * 6 + 5, 0, 0)),
                  pl.BlockSpec((CMB, 128), lambda b, i: (b * n_per_b + i, 0)),
                  pl.BlockSpec(memory_space=pl.ANY)],
        out_specs=pl.BlockSpec((1, CMB, D), lambda b, i: (b, i, 0)),
        out_shape=jax.ShapeDtypeStruct((B, S, D), f32),
        scratch_shapes=[pltpu.VMEM((CMB, D), f32), pltpu.VMEM((CMB, D), f32), pltpu.SemaphoreType.DMA((2,))],
        compiler_params=_cparams("arbitrary", "arbitrary"),
        name="moe_combine",
    )(dest.reshape(N_TOK // CMB, 1, 2 * CMB), h, mod, w_pad, yb)


def _moe_routing(top_e):
    e = top_e.reshape(-1)
    onehot = (e[:, None] == jnp.arange(N_EXPERTS, dtype=jnp.int32)[None, :]).astype(jnp.int32)
    counts = jnp.sum(onehot, axis=0)
    rank = jnp.sum((jnp.cumsum(onehot, axis=0) - 1) * onehot, axis=1)
    padded = (counts + MOE_BLOCK - 1) // MOE_BLOCK * MOE_BLOCK
    pad_end = jnp.cumsum(padded)
    pad_start = pad_end - padded
    dest = (pad_start[e] + rank).astype(jnp.int32)
    blk_start = jnp.arange(N_BLOCKS, dtype=jnp.int32) * MOE_BLOCK
    blk_e = jnp.minimum(jnp.sum((pad_end[None, :] <= blk_start[:, None]).astype(jnp.int32), axis=1),
                        N_EXPERTS - 1).astype(jnp.int32)
    nvalid = jnp.clip(pad_start[blk_e] + counts[blk_e] - blk_start, 0, MOE_BLOCK).astype(jnp.int32)
    nvalid = jnp.where(blk_start < pad_end[N_EXPERTS - 1], nvalid, 0)
    return dest, blk_e, nvalid


def _moe(h, mod, g2, router_w, wg, wu, wd):
    router_pad = jnp.pad(router_w, ((0, 0), (0, 128 - N_EXPERTS)))
    xt, e_pad, w_pad = _moe_prep(h, mod, g2, router_pad)
    dest, blk_e, nvalid = _moe_routing(e_pad[:, :2])
    xb = _moe_scatter(nvalid, dest, xt)
    yb = _moe_experts(blk_e, xb, wg, wu, wd)
    return _moe_combine(dest, h, mod, w_pad, yb)


def _token_mixers(h_lat, h_ctx, ctx_tile, mod, i, with_ctx, tabs, norm1_g, w_in, fnet_w, ret_decay_fwd,
                  ret_decay_bwd, ret_norm_g, ret_w, attn_qn_g, attn_kn_g, attn_w, conv_dw_w, conv_dw_b, conv_ln_g,
                  conv_ln_b, conv_w_out, w_out):
    cos128, sin128, g_tab, m2, fc, cbd, sbd, bd = tabs
    n_tiles = NT if with_ctx else NL
    qg = jnp.tile(attn_qn_g[i].astype(f32), ATT_Q_HEADS)[None, :]
    kg = jnp.tile(attn_kn_g[i].astype(f32), ATT_KV_HEADS)[None, :]
    fn, rq, rk, rv, rg, aq, ak, avt, cv, gt = _proj_in(
        h_lat, h_ctx, ctx_tile, mod, norm1_g[i][None, :], w_in[i].astype(bf16), cos128, sin128, qg, kg, bd)

    o = _attention(aq, ak, avt, n_tiles)

    dmask, xi_f, xi_b, zeta_f, zeta_b, g_f, g_b = _ret_tables(ret_decay_fwd[i], ret_decay_bwd[i])
    sb = _ret_bwd_states(rk, rv, zeta_b, g_b)
    r = _retention(rq, rk, rv, rg, sb, dmask, xi_f, xi_b, zeta_f, g_f, ret_norm_g[i][None, :].astype(f32))

    cvo = _conv(cv, conv_dw_w[i], conv_dw_b[i][None, :], conv_ln_g[i][None, :], conv_ln_b[i][None, :], n_tiles)
    p = _fnet_dft(fn, g_tab, m2, fc, with_ctx)
    return _merge(h_lat, h_ctx, ctx_tile, mod, p, r, o, cvo, gt, cbd, sbd, fnet_w[i].astype(bf16),
                  ret_w[i].astype(bf16), attn_w[i].astype(bf16), conv_w_out[i].astype(bf16),
                  w_out[i].astype(bf16), n_tiles)


def kernel(x, c, ctx, c_ctx, ada_w, ada_b, norm1_g, norm2_g, w_in, fnet_w, ret_decay_fwd, ret_decay_bwd, ret_norm_g, ret_w, attn_qn_g, attn_kn_g, attn_w, conv_dw_w, conv_dw_b, conv_ln_g, conv_ln_b, conv_w_out, w_out, ffn_w_gate, ffn_w_up, ffn_w_down, router_w, moe_w_gate, moe_w_up, moe_w_down):
    cos128, sin128 = _rope_tables()
    as_bf16 = lambda t: jnp.asarray(t, f32).astype(bf16)
    tabs = (cos128, sin128, as_bf16(_G_NP), as_bf16(_M2_NP), as_bf16(_FC_NP), as_bf16(_CBD_NP),
            as_bf16(_SBD_NP), as_bf16(_BD_NP))
    c_rows = jnp.concatenate([c, c_ctx[None, :], jnp.zeros((8 - B - 1, D), f32)], axis=0)
    mods = _ada(c_rows, ada_w, ada_b).reshape(DEPTH, 8 * 6, 1, D)
    h = None
    for i in range(DEPTH):
        with_ctx = i < DEPTH - 1
        mod = mods[i]
        stream = (x, ctx, 0) if i == 0 else (h, h, NL)
        h = _token_mixers(*stream, mod, i, with_ctx, tabs, norm1_g, w_in, fnet_w, ret_decay_fwd, ret_decay_bwd,
                          ret_norm_g, ret_w, attn_qn_g, attn_kn_g, attn_w, conv_dw_w, conv_dw_b, conv_ln_g,
                          conv_ln_b, conv_w_out, w_out)
        j = i // 2
        g2 = norm2_g[i][None, :]
        if i % 2 == 0:
            h = _ffn(h, mod, g2, ffn_w_gate[j].astype(bf16), ffn_w_up[j].astype(bf16),
                     ffn_w_down[j].astype(bf16))
        else:
            h = _moe(h, mod, g2, router_w[j], moe_w_gate[j].astype(bf16), moe_w_up[j].astype(bf16),
                     moe_w_down[j].astype(bf16))
    return h
```
